```python
import math
import jax
import jax.numpy as jnp
from jax import lax
import numpy as np


D_MODEL = 1024
BATCH = 8
SEQ = 4096
DEPTH = 2
DEC_BATCH = 8
DEC_SEQ = 8192
PAST_LEN = 128

GRID_W = 64
D_MIX = D_MODEL
ATT_WIDTH = D_MIX // 2
HEAD_DIM = 64
N_HEADS = ATT_WIDTH // HEAD_DIM
N_KV_HEADS = 2
KV_GROUP = N_HEADS // N_KV_HEADS
AXIS_DIM = HEAD_DIM // 2
ROPE_THETA = 10000.0
Q_BLOCK = 128
GLA_WIDTH = D_MIX - ATT_WIDTH
GLA_HEADS = 4
GLA_DV = GLA_WIDTH // GLA_HEADS
GLA_DK = GLA_DV // 2
GLA_KEY_WIDTH = GLA_HEADS * GLA_DK
GATE_RANK = 16
GATE_TAU = 16.0
CHUNK = 64
D_FF = 2816
EPS = 1e-6

SPLIT_SIZES = (ATT_WIDTH, N_KV_HEADS * HEAD_DIM, N_KV_HEADS * HEAD_DIM,
               GLA_KEY_WIDTH, GLA_KEY_WIDTH, GLA_WIDTH, GLA_WIDTH, GATE_RANK, GATE_RANK)
D_IN_PROJ = ATT_WIDTH + 2 * N_KV_HEADS * HEAD_DIM + 2 * GLA_KEY_WIDTH + 2 * GLA_WIDTH + 2 * GATE_RANK

kernel_name = 'hymba_axial_gqa_bigla_macaron_encoder'


def rmsnorm(x, gain):
    xf = x.astype(jnp.float32)
    y = xf * lax.rsqrt(jnp.mean(xf * xf, axis=-1, keepdims=True) + EPS)
    return (y * gain.astype(jnp.float32)).astype(x.dtype)


def swiglu(x, w_gate_up, w_down):
    g, u = jnp.split(x @ w_gate_up, 2, axis=-1)
    return (jax.nn.silu(g) * u) @ w_down


def split_columns(u):
    parts = []
    off = 0
    for w in SPLIT_SIZES:
        parts.append(u[..., off:off + w])
        off += w
    return parts


def axial_rope_tables(seq_len):
    rows = seq_len // GRID_W
    row = jnp.repeat(jnp.arange(rows, dtype=jnp.float32), GRID_W)
    col = jnp.tile(jnp.arange(GRID_W, dtype=jnp.float32), rows)
    inv_freq = 1.0 / (ROPE_THETA ** (jnp.arange(0, AXIS_DIM, 2, dtype=jnp.float32) / AXIS_DIM))
    ang_r = row[:, None] * inv_freq[None, :]
    ang_c = col[:, None] * inv_freq[None, :]
    ang = jnp.concatenate([ang_r, ang_r, ang_c, ang_c], axis=-1)
    return jnp.cos(ang), jnp.sin(ang)


def _rotate_half(t):
    a, b = jnp.split(t, 2, axis=-1)
    return jnp.concatenate([-b, a], axis=-1)


def rotate_half_axial(x):
    xr, xc = jnp.split(x, 2, axis=-1)
    return jnp.concatenate([_rotate_half(xr), _rotate_half(xc)], axis=-1)


def axial_gqa_attention(q, k, v, q_gain, k_gain, cos, sin):
    bsz, seq, _ = q.shape
    out_dtype = q.dtype
    q = rmsnorm(q.reshape(bsz, seq, N_KV_HEADS, KV_GROUP, HEAD_DIM), q_gain).astype(jnp.float32)
    k = rmsnorm(k.reshape(bsz, seq, N_KV_HEADS, HEAD_DIM), k_gain).astype(jnp.float32)
    v = v.reshape(bsz, seq, N_KV_HEADS, HEAD_DIM).astype(jnp.float32)
    q = (q * cos[None, :, None, None, :] + rotate_half_axial(q) * sin[None, :, None, None, :]) * (HEAD_DIM ** -0.5)
    k = k * cos[None, :, None, :] + rotate_half_axial(k) * sin[None, :, None, :]
    n_blocks = seq // Q_BLOCK
    q_blocks = jnp.moveaxis(q.reshape(bsz, n_blocks, Q_BLOCK, N_KV_HEADS, KV_GROUP, HEAD_DIM), 1, 0)

    def attend_block(qb):
        s = jnp.einsum('bqkgd,bskd->bkgqs', qb, k)
        p = jax.nn.softmax(s, axis=-1)
        return jnp.einsum('bkgqs,bskd->bqkgd', p, v)

    o = lax.map(attend_block, q_blocks)
    o = jnp.moveaxis(o, 0, 1).reshape(bsz, seq, ATT_WIDTH)
    return o.astype(out_dtype)


def gla_chunked(q, k, v, log_a, inclusive):
    bsz, nh, seq, dk = q.shape
    dv = v.shape[-1]
    n = seq // CHUNK
    q = q.reshape(bsz, nh, n, CHUNK, dk)
    k = k.reshape(bsz, nh, n, CHUNK, dk)
    v = v.reshape(bsz, nh, n, CHUNK, dv)
    b = jnp.cumsum(log_a.reshape(bsz, nh, n, CHUNK, dk), axis=-2)
    b_last = b[..., -1:, :]
    b_ref = b[..., CHUNK // 2:CHUNK // 2 + 1, :]
    qr = q * jnp.exp(b - b_ref)
    kr = k * jnp.exp(b_ref - b)
    a_intra = jnp.einsum('bhnid,bhnjd->bhnij', qr, kr)
    mask = jnp.tril(jnp.ones((CHUNK, CHUNK), dtype=bool), k=0 if inclusive else -1)
    a_intra = jnp.where(mask, a_intra, 0.0)
    o_intra = jnp.einsum('bhnij,bhnje->bhnie', a_intra, v)
    d_state = jnp.einsum('bhncd,bhnce->bhnde', k * jnp.exp(b_last - b), v)
    chunk_decay = jnp.exp(b_last[..., 0, :])

    def step(state, xs):
        dec, ds = xs
        return dec[..., None] * state + ds, state

    s0 = jnp.zeros((bsz, nh, dk, dv), jnp.float32)
    _, s_before = lax.scan(step, s0, (jnp.moveaxis(chunk_decay, 2, 0), jnp.moveaxis(d_state, 2, 0)))
    s_before = jnp.moveaxis(s_before, 0, 2)
    o_inter = jnp.einsum('bhncd,bhnde->bhnce', q * jnp.exp(b), s_before)
    return (o_intra + o_inter).reshape(bsz, nh, seq, dv)


def gla_mixer(q, k, v, r, gf_low, gb_low, w_gf, b_gf, w_gb, b_gb, gla_gain):
    bsz, seq, _ = q.shape
    out_dtype = q.dtype

    def to_heads(t, d):
        return t.reshape(bsz, seq, GLA_HEADS, d).transpose(0, 2, 1, 3).astype(jnp.float32)

    qh = to_heads(q, GLA_DK) * (GLA_DK ** -0.5)
    kh = to_heads(k, GLA_DK)
    vh = to_heads(v, GLA_DV)
    log_af = jax.nn.log_sigmoid((gf_low @ w_gf + b_gf).astype(jnp.float32)) / GATE_TAU
    log_ab = jax.nn.log_sigmoid((gb_low @ w_gb + b_gb).astype(jnp.float32)) / GATE_TAU
    o_f = gla_chunked(qh, kh, vh, to_heads(log_af, GLA_DK), True)

    def flip(t):
        return jnp.flip(t, axis=2)

    o_b = flip(gla_chunked(flip(qh), flip(kh), flip(vh), flip(to_heads(log_ab, GLA_DK)), False))
    o = rmsnorm(o_f + o_b, gla_gain)
    o = o.transpose(0, 2, 1, 3).reshape(bsz, seq, GLA_WIDTH)
    return (o * jax.nn.silu(r.astype(jnp.float32))).astype(out_dtype)


def encoder_layer(x, cos, sin, n_ffn1, w_ffn1_gu, w_ffn1_down, n_mix, w_in, q_gain, k_gain,
                  w_gf, b_gf, w_gb, b_gb, gla_gain, w_out, n_ffn2, w_ffn2_gu, w_ffn2_down, n_out):
    h = x + 0.5 * swiglu(rmsnorm(x, n_ffn1), w_ffn1_gu, w_ffn1_down)
    u = rmsnorm(h, n_mix) @ w_in
    q_a, k_a, v_a, q_l, k_l, v_l, r_l, gf_low, gb_low = split_columns(u)
    o_att = axial_gqa_attention(q_a, k_a, v_a, q_gain, k_gain, cos, sin)
    o_gla = gla_mixer(q_l, k_l, v_l, r_l, gf_low, gb_low, w_gf, b_gf, w_gb, b_gb, gla_gain)
    h = h + jnp.concatenate([o_att, o_gla], axis=-1) @ w_out
    h = h + 0.5 * swiglu(rmsnorm(h, n_ffn2), w_ffn2_gu, w_ffn2_down)
    return rmsnorm(h, n_out)


def trunk(x, norm_ffn1, w_ffn1_gu, w_ffn1_down, norm_mix, w_in, q_norm, k_norm,
          w_gate_f, b_gate_f, w_gate_b, b_gate_b, gla_norm, w_out,
          norm_ffn2, w_ffn2_gu, w_ffn2_down, norm_out):
    cos, sin = axial_rope_tables(x.shape[1])
    h = x
    for l in range(DEPTH):
        h = encoder_layer(h, cos, sin, norm_ffn1[l], w_ffn1_gu[l], w_ffn1_down[l], norm_mix[l], w_in[l],
                          q_norm[l], k_norm[l], w_gate_f[l], b_gate_f[l], w_gate_b[l], b_gate_b[l],
                          gla_norm[l], w_out[l], norm_ffn2[l], w_ffn2_gu[l], w_ffn2_down[l], norm_out[l])
    return h


def setup_inputs(seed: int = 0) -> dict:
    key = jax.random.key(seed)
    ks = jax.random.split(key, 20)

    def w(k, shape, fan_in):
        return jax.random.normal(k, shape, jnp.float32) * (fan_in ** -0.5)

    def gain(k, shape):
        return 1.0 + 0.05 * jax.random.normal(k, shape, jnp.float32)

    return {
        'x_prompt': jax.random.normal(ks[0], (BATCH, SEQ, D_MODEL), jnp.float32),
        'x_sample': jax.random.normal(ks[1], (DEC_BATCH, DEC_SEQ, D_MODEL), jnp.float32),
        'norm_ffn1': gain(ks[2], (DEPTH, D_MODEL)),
        'w_ffn1_gu': w(ks[3], (DEPTH, D_MODEL, 2 * D_FF), D_MODEL),
        'w_ffn1_down': w(ks[4], (DEPTH, D_FF, D_MODEL), D_FF),
        'norm_mix': gain(ks[5], (DEPTH, D_MODEL)),
        'w_in': w(ks[6], (DEPTH, D_MODEL, D_IN_PROJ), D_MODEL),
        'q_norm': gain(ks[7], (DEPTH, HEAD_DIM)),
        'k_norm': gain(ks[8], (DEPTH, HEAD_DIM)),
        'w_gate_f': w(ks[9], (DEPTH, GATE_RANK, GLA_KEY_WIDTH), GATE_RANK),
        'b_gate_f': 0.1 * jax.random.normal(ks[10], (DEPTH, GLA_KEY_WIDTH), jnp.float32),
        'w_gate_b': w(ks[11], (DEPTH, GATE_RANK, GLA_KEY_WIDTH), GATE_RANK),
        'b_gate_b': 0.1 * jax.random.normal(ks[12], (DEPTH, GLA_KEY_WIDTH), jnp.float32),
        'gla_norm': gain(ks[13], (DEPTH, GLA_DV)),
        'w_out': w(ks[14], (DEPTH, D_MIX, D_MODEL), D_MIX),
        'norm_ffn2': gain(ks[15], (DEPTH, D_MODEL)),
        'w_ffn2_gu': w(ks[16], (DEPTH, D_MODEL, 2 * D_FF), D_MODEL),
        'w_ffn2_down': w(ks[17], (DEPTH, D_FF, D_MODEL), D_FF),
        'norm_out': gain(ks[18], (DEPTH, D_MODEL)),
    }


def reference(x_prompt, x_sample, norm_ffn1, w_ffn1_gu, w_ffn1_down, norm_mix, w_in, q_norm, k_norm,
              w_gate_f, b_gate_f, w_gate_b, b_gate_b, gla_norm, w_out,
              norm_ffn2, w_ffn2_gu, w_ffn2_down, norm_out):
    y_prompt = trunk(x_prompt, norm_ffn1, w_ffn1_gu, w_ffn1_down, norm_mix, w_in, q_norm, k_norm,
                     w_gate_f, b_gate_f, w_gate_b, b_gate_b, gla_norm, w_out,
                     norm_ffn2, w_ffn2_gu, w_ffn2_down, norm_out)
    y_sample = trunk(x_sample, norm_ffn1, w_ffn1_gu, w_ffn1_down, norm_mix, w_in, q_norm, k_norm,
                     w_gate_f, b_gate_f, w_gate_b, b_gate_b, gla_norm, w_out,
                     norm_ffn2, w_ffn2_gu, w_ffn2_down, norm_out)
    return (y_prompt, y_sample)
```

```python
import functools

import jax
import jax.numpy as jnp
from jax import lax
from jax.experimental import pallas as pl
from jax.experimental.pallas import tpu as pltpu

F32 = jnp.float32
BF16 = jnp.bfloat16

D_MODEL = 1024
GRID_W = 64
ATT_WIDTH = 512
HEAD_DIM = 64
N_HEADS = 8
N_KV_HEADS = 2
KV_GROUP = 4
AXIS_DIM = 32
ROPE_THETA = 10000.0
GLA_WIDTH = 512
GLA_HEADS = 4
GLA_DV = 128
GLA_DK = 64
GLA_KEY_WIDTH = 256
GATE_RANK = 16
GATE_TAU = 16.0
CHUNK = 64
D_FF = 2816
EPS = 1e-6

LANES = 128
KV_W = N_KV_HEADS * HEAD_DIM
QV_ROWS = ATT_WIDTH + KV_W
REST_W = KV_W + 2 * GLA_KEY_WIDTH + 2 * GLA_WIDTH + LANES
GATE_OFF = REST_W - LANES

TM = 512
TQ = 256
FF_CHUNK = 256
N_FF_CHUNKS = D_FF // FF_CHUNK
GLA_BLOCK = 512
NEG_BIG = -1e30
VMEM_LIMIT = 56 * 1024 * 1024


def _cparams(sem):
    return pltpu.CompilerParams(dimension_semantics=sem, vmem_limit_bytes=VMEM_LIMIT)


def _rms_rows(x, gain_row):
    ms = jnp.mean(x * x, axis=-1, keepdims=True)
    return x * lax.rsqrt(ms + EPS) * gain_row


def _bdot(a, b):
    return jnp.dot(a, b, preferred_element_type=F32)


def _nt_dot(a, b):
    return lax.dot_general(a, b, (((1,), (1,)), ((), ())), preferred_element_type=F32)


def _ffn_body(x_ref, g_ref, wgu_ref, wd_ref, og_ref, o_ref, *, final_norm):
    x = x_ref[...]
    xn = _rms_rows(x, g_ref[...]).astype(BF16)
    acc = None
    for c in range(N_FF_CHUNKS):
        gu = _bdot(xn, wgu_ref[:, c * 2 * FF_CHUNK:(c + 1) * 2 * FF_CHUNK])
        g = gu[:, :FF_CHUNK]
        u = gu[:, FF_CHUNK:]
        a = (g * (1.0 / (1.0 + jnp.exp(-g))) * u).astype(BF16)
        d = _bdot(a, wd_ref[c * FF_CHUNK:(c + 1) * FF_CHUNK, :])
        acc = d if acc is None else acc + d
    h = x + 0.5 * acc
    if final_norm:
        h = _rms_rows(h, og_ref[...])
    o_ref[...] = h


def _ffn(x, gain, wgu, wd, out_gain, final_norm):
    t = x.shape[0]
    const = lambda i: (0, 0)
    return pl.pallas_call(
        functools.partial(_ffn_body, final_norm=final_norm),
        grid=(t // TM,),
        in_specs=[
            pl.BlockSpec((TM, D_MODEL), lambda i: (i, 0)),
            pl.BlockSpec((1, D_MODEL), const),
            pl.BlockSpec((D_MODEL, 2 * D_FF), const, pipeline_mode=pl.Buffered(1)),
            pl.BlockSpec((D_FF, D_MODEL), const, pipeline_mode=pl.Buffered(1)),
            pl.BlockSpec((1, D_MODEL), const),
        ],
        out_specs=pl.BlockSpec((TM, D_MODEL), lambda i: (i, 0)),
        out_shape=jax.ShapeDtypeStruct((t, D_MODEL), F32),
        compiler_params=_cparams(("parallel",)),
        name="ffn_final" if final_norm else "ffn",
    )(x, gain, wgu, wd, out_gain)


def _swap16(t, axis):
    n = t.shape[axis] // 16
    parts = [lax.slice_in_dim(t, (i ^ 1) * 16, (i ^ 1) * 16 + 16, axis=axis) for i in range(n)]
    return jnp.concatenate(parts, axis=axis)


def _inproj_body(h_ref, g_ref, wqv_ref, wrest_ref, qg_ref, kg_ref, cost_ref, sint_ref, cosn_ref, sinn_ref,
                 grp_ref, wgate_ref, bgate_ref,
                 qt_ref, k_ref, vt_ref, ql_ref, kl_ref, vl_ref, rl_ref, laf_ref, lab_ref):
    xn = _rms_rows(h_ref[0], g_ref[...]).astype(BF16)
    tm = xn.shape[0]

    ut = _nt_dot(wqv_ref[...], xn)
    q = ut[:ATT_WIDTH].reshape(N_HEADS, HEAD_DIM, tm)
    q = q * lax.rsqrt(jnp.mean(q * q, axis=1, keepdims=True) + EPS) * qg_ref[...][None]
    q = q * cost_ref[...][None] + _swap16(q, 1) * sint_ref[...][None]
    qt_ref[0] = (q * (HEAD_DIM ** -0.5)).reshape(ATT_WIDTH, tm).astype(BF16)

    row = lax.broadcasted_iota(jnp.int32, (HEAD_DIM, tm), 0)
    ones_rows = jnp.where(row == 0, 1.0, 0.0).astype(BF16)
    for kvh in range(N_KV_HEADS):
        v = ut[ATT_WIDTH + kvh * HEAD_DIM:ATT_WIDTH + (kvh + 1) * HEAD_DIM].astype(BF16)
        vt_ref[0, kvh, 0] = jnp.concatenate([v, ones_rows], axis=0)

    rest = _bdot(xn, wrest_ref[...])

    k = rest[:, :KV_W]
    ss = k * k
    ss_hi = ss.astype(BF16)
    ss_lo = (ss - ss_hi.astype(F32)).astype(BF16)
    ms = _bdot(ss_hi, grp_ref[...]) + _bdot(ss_lo, grp_ref[...])
    k = k * lax.rsqrt(ms + EPS) * kg_ref[...]
    lane = lax.broadcasted_iota(jnp.int32, k.shape, 1)
    k_sw = jnp.where((lane % 32) < 16, pltpu.roll(k, LANES - 16, 1), pltpu.roll(k, 16, 1))
    k_ref[0] = (k * cosn_ref[...] + k_sw * sinn_ref[...]).astype(BF16)

    o = KV_W
    ql_ref[0] = rest[:, o:o + GLA_KEY_WIDTH]
    o += GLA_KEY_WIDTH
    kl_ref[0] = rest[:, o:o + GLA_KEY_WIDTH]
    o += GLA_KEY_WIDTH
    vl_ref[0] = rest[:, o:o + GLA_WIDTH].astype(BF16)
    o += GLA_WIDTH
    rl_ref[0] = rest[:, o:o + GLA_WIDTH]

    pre = _bdot(rest[:, GATE_OFF:].astype(BF16), wgate_ref[...]) + bgate_ref[...]
    logsig = jnp.minimum(pre, 0.0) - jnp.log(1.0 + jnp.exp(-jnp.abs(pre)))
    la = logsig * (1.0 / GATE_TAU)
    laf_ref[0] = la[:, :GLA_KEY_WIDTH]
    lab_ref[0] = la[:, GLA_KEY_WIDTH:]


def _inproj(h, p, tabs):
    b, s, _ = h.shape
    nt = s // TM
    c2 = lambda i, j: (0, 0)
    tok = lambda w: pl.BlockSpec((1, TM, w), lambda i, j: (i, j, 0))
    out_shape = (
        jax.ShapeDtypeStruct((b, ATT_WIDTH, s), BF16),
        jax.ShapeDtypeStruct((b, s, KV_W), BF16),
        jax.ShapeDtypeStruct((b, N_KV_HEADS, nt, 2 * HEAD_DIM, TM), BF16),
        jax.ShapeDtypeStruct((b, s, GLA_KEY_WIDTH), F32),
        jax.ShapeDtypeStruct((b, s, GLA_KEY_WIDTH), F32),
        jax.ShapeDtypeStruct((b, s, GLA_WIDTH), BF16),
        jax.ShapeDtypeStruct((b, s, GLA_WIDTH), F32),
        jax.ShapeDtypeStruct((b, s, GLA_KEY_WIDTH), F32),
        jax.ShapeDtypeStruct((b, s, GLA_KEY_WIDTH), F32),
    )
    out_specs = (
        pl.BlockSpec((1, ATT_WIDTH, TM), lambda i, j: (i, 0, j)),
        tok(KV_W),
        pl.BlockSpec((1, N_KV_HEADS, 1, 2 * HEAD_DIM, TM), lambda i, j: (i, 0, j, 0, 0)),
        tok(GLA_KEY_WIDTH), tok(GLA_KEY_WIDTH), tok(GLA_WIDTH), tok(GLA_WIDTH), tok(GLA_KEY_WIDTH),
        tok(GLA_KEY_WIDTH),
    )
    in_specs = [
        tok(D_MODEL),
        pl.BlockSpec((1, D_MODEL), c2),
        pl.BlockSpec((QV_ROWS, D_MODEL), c2),
        pl.BlockSpec((D_MODEL, REST_W), c2),
        pl.BlockSpec((HEAD_DIM, 1), c2),
        pl.BlockSpec((1, KV_W), c2),
        pl.BlockSpec((HEAD_DIM, TM), lambda i, j: (0, j)),
        pl.BlockSpec((HEAD_DIM, TM), lambda i, j: (0, j)),
        pl.BlockSpec((TM, KV_W), lambda i, j: (j, 0)),
        pl.BlockSpec((TM, KV_W), lambda i, j: (j, 0)),
        pl.BlockSpec((KV_W, KV_W), c2),
        pl.BlockSpec((LANES, 2 * GLA_KEY_WIDTH), c2),
        pl.BlockSpec((1, 2 * GLA_KEY_WIDTH), c2),
    ]
    return pl.pallas_call(
        _inproj_body,
        grid=(b, nt),
        in_specs=in_specs,
        out_specs=out_specs,
        out_shape=out_shape,
        compiler_params=_cparams(("parallel", "parallel")),
        name="inproj",
    )(h, p["norm_mix"], p["w_qv_t"], p["w_rest"], p["q_gain_col"], p["k_gain_row"],
      tabs["cos_t"], tabs["sin_t"], tabs["cos_n"], tabs["sin_n"], p["grp_avg"], p["w_gate"], p["b_gate"])


def _attn_body(qt_ref, k_ref, vt_ref, o_ref, qcat_ref, m_ref, acc_ref, *, n_chunks):
    kvh = pl.program_id(1)
    tq = qt_ref.shape[2]

    qblk = qt_ref[0]
    top = jnp.concatenate([qblk[g * HEAD_DIM:(g + 1) * HEAD_DIM] for g in range(KV_GROUP)], axis=1)
    zero = jnp.zeros_like(top)
    qcat_ref[...] = jnp.where(kvh == 0, jnp.concatenate([top, zero], axis=0), jnp.concatenate([zero, top], axis=0))
    m_ref[...] = jnp.full(m_ref.shape, NEG_BIG, F32)
    acc_ref[...] = jnp.zeros(acc_ref.shape, F32)

    def chunk(c, carry):
        kc = k_ref[0, pl.ds(pl.multiple_of(c * TM, TM), TM), :]
        s = _bdot(kc, qcat_ref[...])
        m_old = m_ref[...]
        m_new = jnp.maximum(m_old, jnp.max(s, axis=0, keepdims=True))
        alpha = jnp.exp(m_old - m_new)
        p = jnp.exp(s - m_new).astype(BF16)
        acc_ref[...] = alpha * acc_ref[...] + _bdot(vt_ref[0, 0, c], p)
        m_ref[...] = m_new
        return carry

    lax.fori_loop(0, n_chunks, chunk, 0)

    acc = acc_ref[...]
    out_t = acc[:HEAD_DIM] / acc[HEAD_DIM:HEAD_DIM + 1]
    heads = [out_t[:, g * tq:(g + 1) * tq].T for g in range(KV_GROUP)]
    o_ref[0] = jnp.concatenate(heads, axis=1).astype(BF16)


def _attn(qt, k, vt):
    b, _, s = qt.shape
    n_chunks = s // TM
    tq = min(TQ, s)
    mq = KV_GROUP * tq
    return pl.pallas_call(
        functools.partial(_attn_body, n_chunks=n_chunks),
        grid=(b, N_KV_HEADS, s // tq),
        in_specs=[
            pl.BlockSpec((1, KV_GROUP * HEAD_DIM, tq), lambda i, h, j: (i, h, j)),
            pl.BlockSpec((1, s, KV_W), lambda i, h, j: (i, 0, 0)),
            pl.BlockSpec((1, 1, n_chunks, 2 * HEAD_DIM, TM), lambda i, h, j: (i, h, 0, 0, 0)),
        ],
        out_specs=pl.BlockSpec((1, tq, KV_GROUP * HEAD_DIM), lambda i, h, j: (i, j, h)),
        out_shape=jax.ShapeDtypeStruct((b, s, ATT_WIDTH), BF16),
        scratch_shapes=[
            pltpu.VMEM((KV_W, mq), BF16),
            pltpu.VMEM((1, mq), F32),
            pltpu.VMEM((2 * HEAD_DIM, mq), F32),
        ],
        compiler_params=_cparams(("parallel", "parallel", "arbitrary")),
        name="attn",
    )(qt, k, vt)


def _gla_chunk(q, k, v, la, cum, keep, eye, state_ref, *, ref_idx, last_idx):
    la_hi = la.astype(BF16)
    la_lo = (la - la_hi.astype(F32)).astype(BF16)
    b = _bdot(cum, la_hi) + _bdot(cum, la_lo)
    b_ref = b[ref_idx:ref_idx + 1]
    b_last = b[last_idx:last_idx + 1]
    qs = q * (GLA_DK ** -0.5)
    qr = (qs * jnp.exp(b - b_ref)).astype(BF16)
    kr = (k * jnp.exp(b_ref - b)).astype(BF16)
    kd = (k * jnp.exp(b_last - b)).astype(BF16)
    qb = (qs * jnp.exp(b)).astype(BF16)
    dec = jnp.exp(b_last)
    outs = []
    for h in range(GLA_HEADS):
        ks = slice(h * GLA_DK, (h + 1) * GLA_DK)
        vh = v[:, h * GLA_DV:(h + 1) * GLA_DV]
        a = jnp.where(keep, _nt_dot(qr[:, ks], kr[:, ks]), 0.0).astype(BF16)
        st = state_ref[h]
        outs.append(_bdot(a, vh) + _nt_dot(qb[:, ks], st.astype(BF16)))
        vh_t = _nt_dot(eye, vh).astype(BF16)
        state_ref[h] = dec[:, ks] * st + _bdot(vh_t, kd[:, ks])
    return jnp.concatenate(outs, axis=1)


def _gla_body(qf_ref, kf_ref, vf_ref, laf_ref, qb_ref, kb_ref, vb_ref, lab_ref, of_ref, ob_ref, sf_ref, sb_ref,
              *, n_chunks):
    @pl.when(pl.program_id(1) == 0)
    def _():
        sf_ref[...] = jnp.zeros(sf_ref.shape, F32)
        sb_ref[...] = jnp.zeros(sb_ref.shape, F32)

    r = lax.broadcasted_iota(jnp.int32, (CHUNK, CHUNK), 0)
    c = lax.broadcasted_iota(jnp.int32, (CHUNK, CHUNK), 1)
    cum_f = jnp.where(c <= r, 1.0, 0.0).astype(BF16)
    cum_b = jnp.where(c >= r, 1.0, 0.0).astype(BF16)
    keep_f = c <= r
    keep_b = c > r
    er = lax.broadcasted_iota(jnp.int32, (GLA_DV, GLA_DV), 0)
    ec = lax.broadcasted_iota(jnp.int32, (GLA_DV, GLA_DV), 1)
    eye = jnp.where(er == ec, 1.0, 0.0).astype(BF16)

    def step(i, carry):
        lo = pl.multiple_of(i * CHUNK, CHUNK)
        sl = pl.ds(lo, CHUNK)
        of_ref[0, sl, :] = _gla_chunk(qf_ref[0, sl, :], kf_ref[0, sl, :], vf_ref[0, sl, :], laf_ref[0, sl, :],
                                      cum_f, keep_f, eye, sf_ref, ref_idx=CHUNK // 2, last_idx=CHUNK - 1)
        hi = pl.multiple_of((n_chunks - 1 - i) * CHUNK, CHUNK)
        sh = pl.ds(hi, CHUNK)
        ob_ref[0, sh, :] = _gla_chunk(qb_ref[0, sh, :], kb_ref[0, sh, :], vb_ref[0, sh, :], lab_ref[0, sh, :],
                                      cum_b, keep_b, eye, sb_ref, ref_idx=CHUNK // 2 - 1, last_idx=0)
        return carry

    lax.fori_loop(0, n_chunks, step, 0)


def _gla(ql, kl, vl, laf, lab):
    b, s, _ = ql.shape
    tb = min(GLA_BLOCK, s)
    nb = s // tb
    fwd = lambda w: pl.BlockSpec((1, tb, w), lambda i, j: (i, j, 0))
    bwd = lambda w: pl.BlockSpec((1, tb, w), lambda i, j: (i, nb - 1 - j, 0))
    return pl.pallas_call(
        functools.partial(_gla_body, n_chunks=tb // CHUNK),
        grid=(b, nb),
        in_specs=[fwd(GLA_KEY_WIDTH), fwd(GLA_KEY_WIDTH), fwd(GLA_WIDTH), fwd(GLA_KEY_WIDTH),
                  bwd(GLA_KEY_WIDTH), bwd(GLA_KEY_WIDTH), bwd(GLA_WIDTH), bwd(GLA_KEY_WIDTH)],
        out_specs=(fwd(GLA_WIDTH), bwd(GLA_WIDTH)),
        out_shape=(jax.ShapeDtypeStruct((b, s, GLA_WIDTH), F32), jax.ShapeDtypeStruct((b, s, GLA_WIDTH), F32)),
        scratch_shapes=[pltpu.VMEM((GLA_HEADS, GLA_DV, GLA_DK), F32), pltpu.VMEM((GLA_HEADS, GLA_DV, GLA_DK), F32)],
        compiler_params=_cparams(("parallel", "arbitrary")),
        name="gla",
    )(ql, kl, vl, laf, ql, kl, vl, lab)


def _outproj_body(h_ref, oa_ref, of_ref, ob_ref, r_ref, gg_ref, wa_ref, wg_ref, o_ref):
    o = of_ref[...] + ob_ref[...]
    gain = gg_ref[...]
    r = r_ref[...]
    parts = []
    for hh in range(GLA_HEADS):
        sl = slice(hh * GLA_DV, (hh + 1) * GLA_DV)
        rh = r[:, sl]
        parts.append((_rms_rows(o[:, sl], gain[:, sl]) * (rh * (1.0 / (1.0 + jnp.exp(-rh))))).astype(BF16))
    og = jnp.concatenate(parts, axis=1)
    o_ref[...] = h_ref[...] + _bdot(oa_ref[...], wa_ref[...]) + _bdot(og, wg_ref[...])


def _outproj(h, o_att, o_f, o_b, r, gla_gain, w_att, w_gla):
    t = h.shape[0]
    const = lambda i: (0, 0)
    tok = lambda w: pl.BlockSpec((TM, w), lambda i: (i, 0))
    return pl.pallas_call(
        _outproj_body,
        grid=(t // TM,),
        in_specs=[tok(D_MODEL), tok(ATT_WIDTH), tok(GLA_WIDTH), tok(GLA_WIDTH), tok(GLA_WIDTH),
                  pl.BlockSpec((1, GLA_WIDTH), const),
                  pl.BlockSpec((ATT_WIDTH, D_MODEL), const),
                  pl.BlockSpec((GLA_WIDTH, D_MODEL), const)],
        out_specs=tok(D_MODEL),
        out_shape=jax.ShapeDtypeStruct((t, D_MODEL), F32),
        compiler_params=_cparams(("parallel",)),
        name="outproj",
    )(h, o_att, o_f, o_b, r, gla_gain, w_att, w_gla)


def _rope_tables(seq_len):
    rows = seq_len // GRID_W
    row = jnp.repeat(jnp.arange(rows, dtype=F32), GRID_W)
    col = jnp.tile(jnp.arange(GRID_W, dtype=F32), rows)
    inv_freq = 1.0 / (ROPE_THETA ** (jnp.arange(0, AXIS_DIM, 2, dtype=F32) / AXIS_DIM))
    ang_r = row[:, None] * inv_freq[None, :]
    ang_c = col[:, None] * inv_freq[None, :]
    ang = jnp.concatenate([ang_r, ang_r, ang_c, ang_c], axis=-1)
    sign = jnp.where((jnp.arange(HEAD_DIM) % 32) < 16, -1.0, 1.0).astype(F32)
    cos, sin = jnp.cos(ang), jnp.sin(ang) * sign[None, :]
    return {
        "cos_t": cos.T, "sin_t": sin.T,
        "cos_n": jnp.tile(cos, (1, N_KV_HEADS)), "sin_n": jnp.tile(sin, (1, N_KV_HEADS)),
    }


def _prep_layer(l, norm_ffn1, w_ffn1_gu, w_ffn1_down, norm_mix, w_in, q_norm, k_norm, w_gate_f, b_gate_f,
                w_gate_b, b_gate_b, gla_norm, w_out, norm_ffn2, w_ffn2_gu, w_ffn2_down, norm_out):
    def gu_layout(w):
        g = w[:, :D_FF].reshape(D_MODEL, N_FF_CHUNKS, 1, FF_CHUNK)
        u = w[:, D_FF:].reshape(D_MODEL, N_FF_CHUNKS, 1, FF_CHUNK)
        return jnp.concatenate([g, u], axis=2).reshape(D_MODEL, 2 * D_FF).astype(BF16)

    w = w_in[l]
    o_q, o_k, o_v = 0, ATT_WIDTH, ATT_WIDTH + KV_W
    o_rest = ATT_WIDTH + 2 * KV_W
    n_rest = 2 * GLA_KEY_WIDTH + 2 * GLA_WIDTH
    gates = jnp.pad(w[:, o_rest + n_rest:], ((0, 0), (0, LANES - 2 * GATE_RANK)))
    w_rest = jnp.concatenate([w[:, o_k:o_v], w[:, o_rest:o_rest + n_rest], gates], axis=1).astype(BF16)
    w_qv_t = jnp.concatenate([w[:, o_q:o_k], w[:, o_v:o_rest]], axis=1).T.astype(BF16)
    w_gate = jnp.zeros((LANES, 2 * GLA_KEY_WIDTH), F32)
    w_gate = w_gate.at[:GATE_RANK, :GLA_KEY_WIDTH].set(w_gate_f[l])
    w_gate = w_gate.at[GATE_RANK:2 * GATE_RANK, GLA_KEY_WIDTH:].set(w_gate_b[l])
    hid = jnp.arange(KV_W) // HEAD_DIM
    grp_avg = jnp.where(hid[:, None] == hid[None, :], 1.0 / HEAD_DIM, 0.0).astype(BF16)
    row = lambda v: v.reshape(1, -1).astype(F32)
    return {
        "norm_ffn1": row(norm_ffn1[l]), "w_ffn1_gu": gu_layout(w_ffn1_gu[l]), "w_ffn1_down": w_ffn1_down[l].astype(BF16),
        "norm_mix": row(norm_mix[l]), "w_qv_t": w_qv_t, "w_rest": w_rest,
        "q_gain_col": q_norm[l].reshape(HEAD_DIM, 1).astype(F32),
        "k_gain_row": row(jnp.tile(k_norm[l], N_KV_HEADS)),
        "grp_avg": grp_avg, "w_gate": w_gate.astype(BF16),
        "b_gate": row(jnp.concatenate([b_gate_f[l], b_gate_b[l]])),
        "gla_gain": row(jnp.tile(gla_norm[l], GLA_HEADS)),
        "w_out_att": w_out[l][:ATT_WIDTH].astype(BF16), "w_out_gla": w_out[l][ATT_WIDTH:].astype(BF16),
        "norm_ffn2": row(norm_ffn2[l]), "w_ffn2_gu": gu_layout(w_ffn2_gu[l]), "w_ffn2_down": w_ffn2_down[l].astype(BF16),
        "norm_out": row(norm_out[l]),
    }


def _layer(x, p, tabs):
    b, s, d = x.shape
    t = b * s
    h = _ffn(x.reshape(t, d), p["norm_ffn1"], p["w_ffn1_gu"], p["w_ffn1_down"], p["norm_out"], False)
    qt, k, vt, ql, kl, vl, rl, laf, lab = _inproj(h.reshape(b, s, d), p, tabs)
    o_att = _attn(qt, k, vt)
    o_f, o_b = _gla(ql, kl, vl, laf, lab)
    h = _outproj(h, o_att.reshape(t, ATT_WIDTH), o_f.reshape(t, GLA_WIDTH), o_b.reshape(t, GLA_WIDTH),
                 rl.reshape(t, GLA_WIDTH), p["gla_gain"], p["w_out_att"], p["w_out_gla"])
    y = _ffn(h, p["norm_ffn2"], p["w_ffn2_gu"], p["w_ffn2_down"], p["norm_out"], True)
    return y.reshape(b, s, d)


def _trunk(x, layers):
    tabs = _rope_tables(x.shape[1])
    h = x
    for p in layers:
        h = _layer(h, p, tabs)
    return h


def kernel(x_prompt, x_sample, norm_ffn1, w_ffn1_gu, w_ffn1_down, norm_mix, w_in, q_norm, k_norm, w_gate_f, b_gate_f,
           w_gate_b, b_gate_b, gla_norm, w_out, norm_ffn2, w_ffn2_gu, w_ffn2_down, norm_out):
    params = (norm_ffn1, w_ffn1_gu, w_ffn1_down, norm_mix, w_in, q_norm, k_norm, w_gate_f, b_gate_f,
              w_gate_b, b_gate_b, gla_norm, w_out, norm_ffn2, w_ffn2_gu, w_ffn2_down, norm_out)
    layers = [_prep_layer(l, *params) for l in range(norm_ffn1.shape[0])]
    return (_trunk(x_prompt, layers), _trunk(x_sample, layers))
```

```python
import functools

import jax
import jax.numpy as jnp
from jax import lax
from jax.experimental import pallas as pl
from jax.experimental.pallas import tpu as pltpu

F32 = jnp.float32
BF16 = jnp.bfloat16

D_MODEL = 1024
GRID_W = 64
ATT_WIDTH = 512
HEAD_DIM = 64
N_HEADS = 8
N_KV_HEADS = 2
KV_GROUP = 4
AXIS_DIM = 32
ROPE_THETA = 10000.0
GLA_WIDTH = 512
GLA_HEADS = 4
GLA_DV = 128
GLA_DK = 64
GLA_KEY_WIDTH = 256
GATE_RANK = 16
GATE_TAU = 16.0
CHUNK = 64
D_FF = 2816
EPS = 1e-6

LANES = 128
KV_W = N_KV_HEADS * HEAD_DIM
QV_ROWS = ATT_WIDTH + KV_W
REST_W = KV_W + 2 * GLA_KEY_WIDTH + 2 * GLA_WIDTH + LANES
GATE_OFF = REST_W - LANES

TM = 512
TQ = 256
FF_CHUNK = 256
N_FF_CHUNKS = D_FF // FF_CHUNK
GLA_BLOCK = 512
NEG_BIG = -1e30
LOG2E = 1.4426950408889634
V_ROWS = HEAD_DIM + 16
VMEM_LIMIT = 56 * 1024 * 1024


def _cparams(sem):
    return pltpu.CompilerParams(dimension_semantics=sem, vmem_limit_bytes=VMEM_LIMIT)


def _rms_rows(x, gain_row):
    ms = jnp.mean(x * x, axis=-1, keepdims=True)
    return x * lax.rsqrt(ms + EPS) * gain_row


def _bdot(a, b):
    return jnp.dot(a, b, preferred_element_type=F32)


def _nt_dot(a, b):
    return lax.dot_general(a, b, (((1,), (1,)), ((), ())), preferred_element_type=F32)


def _ffn_body(x_ref, g_ref, wgu_ref, wd_ref, og_ref, o_ref, *, final_norm):
    x = x_ref[...]
    xn = _rms_rows(x, g_ref[...]).astype(BF16)
    acc = None
    for c in range(N_FF_CHUNKS):
        gu = _bdot(xn, wgu_ref[:, c * 2 * FF_CHUNK:(c + 1) * 2 * FF_CHUNK])
        g = gu[:, :FF_CHUNK]
        u = gu[:, FF_CHUNK:]
        a = (g * (1.0 / (1.0 + jnp.exp(-g))) * u).astype(BF16)
        d = _bdot(a, wd_ref[c * FF_CHUNK:(c + 1) * FF_CHUNK, :])
        acc = d if acc is None else acc + d
    h = x + 0.5 * acc
    if final_norm:
        h = _rms_rows(h, og_ref[...])
    o_ref[...] = h


def _ffn(x, gain, wgu, wd, out_gain, final_norm):
    t = x.shape[0]
    const = lambda i: (0, 0)
    return pl.pallas_call(
        functools.partial(_ffn_body, final_norm=final_norm),
        grid=(t // TM,),
        in_specs=[
            pl.BlockSpec((TM, D_MODEL), lambda i: (i, 0)),
            pl.BlockSpec((1, D_MODEL), const),
            pl.BlockSpec((D_MODEL, 2 * D_FF), const, pipeline_mode=pl.Buffered(1)),
            pl.BlockSpec((D_FF, D_MODEL), const, pipeline_mode=pl.Buffered(1)),
            pl.BlockSpec((1, D_MODEL), const),
        ],
        out_specs=pl.BlockSpec((TM, D_MODEL), lambda i: (i, 0)),
        out_shape=jax.ShapeDtypeStruct((t, D_MODEL), F32),
        compiler_params=_cparams(("parallel",)),
        name="ffn_final" if final_norm else "ffn",
    )(x, gain, wgu, wd, out_gain)


def _swap16(t, axis):
    n = t.shape[axis] // 16
    parts = [lax.slice_in_dim(t, (i ^ 1) * 16, (i ^ 1) * 16 + 16, axis=axis) for i in range(n)]
    return jnp.concatenate(parts, axis=axis)


def _inproj_body(h_ref, g_ref, wqv_ref, wrest_ref, qg_ref, kg_ref, cost_ref, sint_ref, cosn_ref, sinn_ref,
                 grp_ref, wgate_ref, bgate_ref,
                 qt_ref, k_ref, vt_ref, ql_ref, kl_ref, vl_ref, rl_ref, laf_ref, lab_ref):
    xn = _rms_rows(h_ref[0], g_ref[...]).astype(BF16)
    tm = xn.shape[0]

    ut = _nt_dot(wqv_ref[...], xn)
    q = ut[:ATT_WIDTH].reshape(N_HEADS, HEAD_DIM, tm)
    q = q * lax.rsqrt(jnp.mean(q * q, axis=1, keepdims=True) + EPS) * qg_ref[...][None]
    q = q * cost_ref[...][None] + _swap16(q, 1) * sint_ref[...][None]
    qt_ref[0] = (q * (HEAD_DIM ** -0.5 * LOG2E)).reshape(ATT_WIDTH, tm).astype(BF16)

    row = lax.broadcasted_iota(jnp.int32, (V_ROWS - HEAD_DIM, tm), 0)
    ones_rows = jnp.where(row == 0, 1.0, 0.0).astype(BF16)
    for kvh in range(N_KV_HEADS):
        v = ut[ATT_WIDTH + kvh * HEAD_DIM:ATT_WIDTH + (kvh + 1) * HEAD_DIM].astype(BF16)
        vt_ref[0, kvh, 0] = jnp.concatenate([v, ones_rows], axis=0)

    rest = _bdot(xn, wrest_ref[...])

    k = rest[:, :KV_W]
    ss = k * k
    ss_hi = ss.astype(BF16)
    ss_lo = (ss - ss_hi.astype(F32)).astype(BF16)
    ms = _bdot(ss_hi, grp_ref[...]) + _bdot(ss_lo, grp_ref[...])
    k = k * lax.rsqrt(ms + EPS) * kg_ref[...]
    lane = lax.broadcasted_iota(jnp.int32, k.shape, 1)
    k_sw = jnp.where((lane % 32) < 16, pltpu.roll(k, LANES - 16, 1), pltpu.roll(k, 16, 1))
    k_ref[0] = (k * cosn_ref[...] + k_sw * sinn_ref[...]).astype(BF16)

    o = KV_W
    ql_ref[0] = rest[:, o:o + GLA_KEY_WIDTH]
    o += GLA_KEY_WIDTH
    kl_ref[0] = rest[:, o:o + GLA_KEY_WIDTH]
    o += GLA_KEY_WIDTH
    vl_ref[0] = rest[:, o:o + GLA_WIDTH].astype(BF16)
    o += GLA_WIDTH
    rl_ref[0] = rest[:, o:o + GLA_WIDTH]

    pre = _bdot(rest[:, GATE_OFF:].astype(BF16), wgate_ref[...]) + bgate_ref[...]
    logsig = jnp.minimum(pre, 0.0) - jnp.log(1.0 + jnp.exp(-jnp.abs(pre)))
    la = logsig * (1.0 / GATE_TAU)
    laf_ref[0] = la[:, :GLA_KEY_WIDTH]
    lab_ref[0] = la[:, GLA_KEY_WIDTH:]


def _inproj(h, p, tabs):
    b, s, _ = h.shape
    nt = s // TM
    c2 = lambda i, j: (0, 0)
    tok = lambda w: pl.BlockSpec((1, TM, w), lambda i, j: (i, j, 0))
    out_shape = (
        jax.ShapeDtypeStruct((b, ATT_WIDTH, s), BF16),
        jax.ShapeDtypeStruct((b, s, KV_W), BF16),
        jax.ShapeDtypeStruct((b, N_KV_HEADS, nt, V_ROWS, TM), BF16),
        jax.ShapeDtypeStruct((b, s, GLA_KEY_WIDTH), F32),
        jax.ShapeDtypeStruct((b, s, GLA_KEY_WIDTH), F32),
        jax.ShapeDtypeStruct((b, s, GLA_WIDTH), BF16),
        jax.ShapeDtypeStruct((b, s, GLA_WIDTH), F32),
        jax.ShapeDtypeStruct((b, s, GLA_KEY_WIDTH), F32),
        jax.ShapeDtypeStruct((b, s, GLA_KEY_WIDTH), F32),
    )
    out_specs = (
        pl.BlockSpec((1, ATT_WIDTH, TM), lambda i, j: (i, 0, j)),
        tok(KV_W),
        pl.BlockSpec((1, N_KV_HEADS, 1, V_ROWS, TM), lambda i, j: (i, 0, j, 0, 0)),
        tok(GLA_KEY_WIDTH), tok(GLA_KEY_WIDTH), tok(GLA_WIDTH), tok(GLA_WIDTH), tok(GLA_KEY_WIDTH),
        tok(GLA_KEY_WIDTH),
    )
    in_specs = [
        tok(D_MODEL),
        pl.BlockSpec((1, D_MODEL), c2),
        pl.BlockSpec((QV_ROWS, D_MODEL), c2),
        pl.BlockSpec((D_MODEL, REST_W), c2),
        pl.BlockSpec((HEAD_DIM, 1), c2),
        pl.BlockSpec((1, KV_W), c2),
        pl.BlockSpec((HEAD_DIM, TM), lambda i, j: (0, j)),
        pl.BlockSpec((HEAD_DIM, TM), lambda i, j: (0, j)),
        pl.BlockSpec((TM, KV_W), lambda i, j: (j, 0)),
        pl.BlockSpec((TM, KV_W), lambda i, j: (j, 0)),
        pl.BlockSpec((KV_W, KV_W), c2),
        pl.BlockSpec((LANES, 2 * GLA_KEY_WIDTH), c2),
        pl.BlockSpec((1, 2 * GLA_KEY_WIDTH), c2),
    ]
    return pl.pallas_call(
        _inproj_body,
        grid=(b, nt),
        in_specs=in_specs,
        out_specs=out_specs,
        out_shape=out_shape,
        compiler_params=_cparams(("parallel", "parallel")),
        name="inproj",
    )(h, p["norm_mix"], p["w_qv_t"], p["w_rest"], p["q_gain_col"], p["k_gain_row"],
      tabs["cos_t"], tabs["sin_t"], tabs["cos_n"], tabs["sin_n"], p["grp_avg"], p["w_gate"], p["b_gate"])


def _attn_body(qt_ref, k_ref, vt_ref, o_ref, qcat_ref, acc_ref, s0_ref, s1_ref, *, n_chunks, unroll):
    kvh = pl.program_id(1)
    tq = qt_ref.shape[2]
    cols = [slice(g * tq, (g + 1) * tq) for g in range(KV_GROUP)]

    qblk = qt_ref[0]
    top = jnp.concatenate([qblk[g * HEAD_DIM:(g + 1) * HEAD_DIM] for g in range(KV_GROUP)], axis=1)
    zero = jnp.zeros_like(top)
    qcat_ref[...] = jnp.where(kvh == 0, jnp.concatenate([top, zero], axis=0), jnp.concatenate([zero, top], axis=0))
    acc_ref[...] = jnp.zeros(acc_ref.shape, F32)

    def scores(c, dst_ref):
        kc = k_ref[0, pl.ds(pl.multiple_of(c * TM, TM), TM), :]
        maxes = []
        for sl in cols:
            s = _bdot(kc, qcat_ref[:, sl])
            dst_ref[:, sl] = s
            maxes.append(jnp.max(s, axis=0, keepdims=True))
        return jnp.concatenate(maxes, axis=1)

    def absorb(c, src_ref, chunk_max, m_old):
        m_new = jnp.maximum(m_old, chunk_max)
        alpha = jnp.exp2(m_old - m_new)
        vt = vt_ref[0, 0, c]
        for sl in cols:
            p = jnp.exp2(src_ref[:, sl] - m_new[:, sl]).astype(BF16)
            acc_ref[:, sl] = alpha[:, sl] * acc_ref[:, sl] + _bdot(vt, p)
        return m_new

    bufs = (s0_ref, s1_ref)

    def group(first, carry, last_group):
        m, chunk_max = carry
        for j in range(unroll):
            c = first + j
            final = last_group and j == unroll - 1
            next_max = None if final else scores(c + 1, bufs[(j + 1) % 2])
            m = absorb(c, bufs[j % 2], chunk_max, m)
            chunk_max = next_max
        return m, chunk_max

    carry = (jnp.full((1, KV_GROUP * tq), NEG_BIG, F32), scores(0, bufs[0]))
    carry = lax.fori_loop(0, n_chunks // unroll - 1, lambda i, cr: group(i * unroll, cr, False), carry)
    group(n_chunks - unroll, carry, True)

    acc = acc_ref[...]
    out_t = acc[:HEAD_DIM] / acc[HEAD_DIM:HEAD_DIM + 1]
    heads = [out_t[:, sl].T for sl in cols]
    o_ref[0] = jnp.concatenate(heads, axis=1).astype(BF16)


def _attn(qt, k, vt):
    b, _, s = qt.shape
    n_chunks = s // TM
    unroll = 4 if n_chunks % 4 == 0 and n_chunks > 8 else 2
    assert n_chunks % unroll == 0
    tq = min(TQ, s)
    mq = KV_GROUP * tq
    return pl.pallas_call(
        functools.partial(_attn_body, n_chunks=n_chunks, unroll=unroll),
        grid=(b, N_KV_HEADS, s // tq),
        in_specs=[
            pl.BlockSpec((1, KV_GROUP * HEAD_DIM, tq), lambda i, h, j: (i, h, j)),
            pl.BlockSpec((1, s, KV_W), lambda i, h, j: (i, 0, 0)),
            pl.BlockSpec((1, 1, n_chunks, V_ROWS, TM), lambda i, h, j: (i, h, 0, 0, 0)),
        ],
        out_specs=pl.BlockSpec((1, tq, KV_GROUP * HEAD_DIM), lambda i, h, j: (i, j, h)),
        out_shape=jax.ShapeDtypeStruct((b, s, ATT_WIDTH), BF16),
        scratch_shapes=[
            pltpu.VMEM((KV_W, mq), BF16),
            pltpu.VMEM((V_ROWS, mq), F32),
            pltpu.VMEM((TM, mq), F32),
            pltpu.VMEM((TM, mq), F32),
        ],
        compiler_params=_cparams(("parallel", "parallel", "arbitrary")),
        name="attn",
    )(qt, k, vt)


def _gla_direction(q_ref, k_ref, v_ref, la_ref, o_ref, state_ref, order, cum, keep, eye, ref_idx, last_idx):
    heads = range(GLA_HEADS)
    ksl = [slice(h * GLA_DK, (h + 1) * GLA_DK) for h in heads]
    vsl = [slice(h * GLA_DV, (h + 1) * GLA_DV) for h in heads]
    rows = [slice(ci * CHUNK, (ci + 1) * CHUNK) for ci in order]
    n = range(len(rows))

    b = []
    for sl in rows:
        la = la_ref[0, sl, :]
        la_hi = la.astype(BF16)
        la_lo = (la - la_hi.astype(F32)).astype(BF16)
        b.append(_bdot(cum, la_hi) + _bdot(cum, la_lo))

    qr, kr, kd, qb, dec, v = [], [], [], [], [], []
    for sl, bc in zip(rows, b):
        b_ref = bc[ref_idx:ref_idx + 1]
        b_last = bc[last_idx:last_idx + 1]
        qs = q_ref[0, sl, :] * (GLA_DK ** -0.5)
        k = k_ref[0, sl, :]
        qr.append((qs * jnp.exp(bc - b_ref)).astype(BF16))
        kr.append((k * jnp.exp(b_ref - bc)).astype(BF16))
        kd.append((k * jnp.exp(b_last - bc)).astype(BF16))
        qb.append((qs * jnp.exp(bc)).astype(BF16))
        dec.append(jnp.exp(b_last))
        v.append(v_ref[0, sl, :])

    a = [[jnp.where(keep, _nt_dot(qr[i][:, ksl[h]], kr[i][:, ksl[h]]), 0.0).astype(BF16) for h in heads] for i in n]
    v_t = [[_nt_dot(eye, v[i][:, vsl[h]]).astype(BF16) for h in heads] for i in n]
    o_intra = [[_bdot(a[i][h], v[i][:, vsl[h]]) for h in heads] for i in n]
    d_state = [[_bdot(v_t[i][h], kd[i][:, ksl[h]]) for h in heads] for i in n]

    state = [state_ref[h] for h in heads]
    entering = []
    for i in n:
        entering.append(state)
        state = [dec[i][:, ksl[h]] * state[h] + d_state[i][h] for h in heads]
    for h in heads:
        state_ref[h] = state[h]

    o_inter = [[_nt_dot(qb[i][:, ksl[h]], entering[i][h].astype(BF16)) for h in heads] for i in n]
    for i in n:
        o_ref[0, rows[i], :] = jnp.concatenate([o_intra[i][h] + o_inter[i][h] for h in heads], axis=1)


def _gla_body(qf_ref, kf_ref, vf_ref, laf_ref, qb_ref, kb_ref, vb_ref, lab_ref, of_ref, ob_ref, sf_ref, sb_ref,
              *, n_chunks):
    @pl.when(pl.program_id(1) == 0)
    def _():
        sf_ref[...] = jnp.zeros(sf_ref.shape, F32)
        sb_ref[...] = jnp.zeros(sb_ref.shape, F32)

    r = lax.broadcasted_iota(jnp.int32, (CHUNK, CHUNK), 0)
    c = lax.broadcasted_iota(jnp.int32, (CHUNK, CHUNK), 1)
    er = lax.broadcasted_iota(jnp.int32, (GLA_DV, GLA_DV), 0)
    ec = lax.broadcasted_iota(jnp.int32, (GLA_DV, GLA_DV), 1)
    eye = jnp.where(er == ec, 1.0, 0.0).astype(BF16)
    _gla_direction(qf_ref, kf_ref, vf_ref, laf_ref, of_ref, sf_ref, range(n_chunks),
                   jnp.where(c <= r, 1.0, 0.0).astype(BF16), c <= r, eye, CHUNK // 2, CHUNK - 1)
    _gla_direction(qb_ref, kb_ref, vb_ref, lab_ref, ob_ref, sb_ref, range(n_chunks - 1, -1, -1),
                   jnp.where(c >= r, 1.0, 0.0).astype(BF16), c > r, eye, CHUNK // 2 - 1, 0)


def _gla(ql, kl, vl, laf, lab):
    b, s, _ = ql.shape
    tb = min(GLA_BLOCK, s)
    nb = s // tb
    fwd = lambda w: pl.BlockSpec((1, tb, w), lambda i, j: (i, j, 0))
    bwd = lambda w: pl.BlockSpec((1, tb, w), lambda i, j: (i, nb - 1 - j, 0))
    return pl.pallas_call(
        functools.partial(_gla_body, n_chunks=tb // CHUNK),
        grid=(b, nb),
        in_specs=[fwd(GLA_KEY_WIDTH), fwd(GLA_KEY_WIDTH), fwd(GLA_WIDTH), fwd(GLA_KEY_WIDTH),
                  bwd(GLA_KEY_WIDTH), bwd(GLA_KEY_WIDTH), bwd(GLA_WIDTH), bwd(GLA_KEY_WIDTH)],
        out_specs=(fwd(GLA_WIDTH), bwd(GLA_WIDTH)),
        out_shape=(jax.ShapeDtypeStruct((b, s, GLA_WIDTH), F32), jax.ShapeDtypeStruct((b, s, GLA_WIDTH), F32)),
        scratch_shapes=[pltpu.VMEM((GLA_HEADS, GLA_DV, GLA_DK), F32), pltpu.VMEM((GLA_HEADS, GLA_DV, GLA_DK), F32)],
        compiler_params=_cparams(("parallel", "arbitrary")),
        name="gla",
    )(ql, kl, vl, laf, ql, kl, vl, lab)


def _outproj_body(h_ref, oa_ref, of_ref, ob_ref, r_ref, gg_ref, wa_ref, wg_ref, o_ref):
    o = of_ref[...] + ob_ref[...]
    gain = gg_ref[...]
    r = r_ref[...]
    parts = []
    for hh in range(GLA_HEADS):
        sl = slice(hh * GLA_DV, (hh + 1) * GLA_DV)
        rh = r[:, sl]
        parts.append((_rms_rows(o[:, sl], gain[:, sl]) * (rh * (1.0 / (1.0 + jnp.exp(-rh))))).astype(BF16))
    og = jnp.concatenate(parts, axis=1)
    o_ref[...] = h_ref[...] + _bdot(oa_ref[...], wa_ref[...]) + _bdot(og, wg_ref[...])


def _outproj(h, o_att, o_f, o_b, r, gla_gain, w_att, w_gla):
    t = h.shape[0]
    const = lambda i: (0, 0)
    tok = lambda w: pl.BlockSpec((TM, w), lambda i: (i, 0))
    return pl.pallas_call(
        _outproj_body,
        grid=(t // TM,),
        in_specs=[tok(D_MODEL), tok(ATT_WIDTH), tok(GLA_WIDTH), tok(GLA_WIDTH), tok(GLA_WIDTH),
                  pl.BlockSpec((1, GLA_WIDTH), const),
                  pl.BlockSpec((ATT_WIDTH, D_MODEL), const),
                  pl.BlockSpec((GLA_WIDTH, D_MODEL), const)],
        out_specs=tok(D_MODEL),
        out_shape=jax.ShapeDtypeStruct((t, D_MODEL), F32),
        compiler_params=_cparams(("parallel",)),
        name="outproj",
    )(h, o_att, o_f, o_b, r, gla_gain, w_att, w_gla)


def _rope_tables(seq_len):
    rows = seq_len // GRID_W
    row = jnp.repeat(jnp.arange(rows, dtype=F32), GRID_W)
    col = jnp.tile(jnp.arange(GRID_W, dtype=F32), rows)
    inv_freq = 1.0 / (ROPE_THETA ** (jnp.arange(0, AXIS_DIM, 2, dtype=F32) / AXIS_DIM))
    ang_r = row[:, None] * inv_freq[None, :]
    ang_c = col[:, None] * inv_freq[None, :]
    ang = jnp.concatenate([ang_r, ang_r, ang_c, ang_c], axis=-1)
    sign = jnp.where((jnp.arange(HEAD_DIM) % 32) < 16, -1.0, 1.0).astype(F32)
    cos, sin = jnp.cos(ang), jnp.sin(ang) * sign[None, :]
    return {
        "cos_t": cos.T, "sin_t": sin.T,
        "cos_n": jnp.tile(cos, (1, N_KV_HEADS)), "sin_n": jnp.tile(sin, (1, N_KV_HEADS)),
    }


def _prep_layer(l, norm_ffn1, w_ffn1_gu, w_ffn1_down, norm_mix, w_in, q_norm, k_norm, w_gate_f, b_gate_f,
                w_gate_b, b_gate_b, gla_norm, w_out, norm_ffn2, w_ffn2_gu, w_ffn2_down, norm_out):
    def gu_layout(w):
        g = w[:, :D_FF].reshape(D_MODEL, N_FF_CHUNKS, 1, FF_CHUNK)
        u = w[:, D_FF:].reshape(D_MODEL, N_FF_CHUNKS, 1, FF_CHUNK)
        return jnp.concatenate([g, u], axis=2).reshape(D_MODEL, 2 * D_FF).astype(BF16)

    w = w_in[l]
    o_q, o_k, o_v = 0, ATT_WIDTH, ATT_WIDTH + KV_W
    o_rest = ATT_WIDTH + 2 * KV_W
    n_rest = 2 * GLA_KEY_WIDTH + 2 * GLA_WIDTH
    gates = jnp.pad(w[:, o_rest + n_rest:], ((0, 0), (0, LANES - 2 * GATE_RANK)))
    w_rest = jnp.concatenate([w[:, o_k:o_v], w[:, o_rest:o_rest + n_rest], gates], axis=1).astype(BF16)
    w_qv_t = jnp.concatenate([w[:, o_q:o_k], w[:, o_v:o_rest]], axis=1).T.astype(BF16)
    w_gate = jnp.zeros((LANES, 2 * GLA_KEY_WIDTH), F32)
    w_gate = w_gate.at[:GATE_RANK, :GLA_KEY_WIDTH].set(w_gate_f[l])
    w_gate = w_gate.at[GATE_RANK:2 * GATE_RANK, GLA_KEY_WIDTH:].set(w_gate_b[l])
    hid = jnp.arange(KV_W) // HEAD_DIM
    grp_avg = jnp.where(hid[:, None] == hid[None, :], 1.0 / HEAD_DIM, 0.0).astype(BF16)
    row = lambda v: v.reshape(1, -1).astype(F32)
    return {
        "norm_ffn1": row(norm_ffn1[l]), "w_ffn1_gu": gu_layout(w_ffn1_gu[l]), "w_ffn1_down": w_ffn1_down[l].astype(BF16),
        "norm_mix": row(norm_mix[l]), "w_qv_t": w_qv_t, "w_rest": w_rest,
        "q_gain_col": q_norm[l].reshape(HEAD_DIM, 1).astype(F32),
        "k_gain_row": row(jnp.tile(k_norm[l], N_KV_HEADS)),
        "grp_avg": grp_avg, "w_gate": w_gate.astype(BF16),
        "b_gate": row(jnp.concatenate([b_gate_f[l], b_gate_b[l]])),
        "gla_gain": row(jnp.tile(gla_norm[l], GLA_HEADS)),
        "w_out_att": w_out[l][:ATT_WIDTH].astype(BF16), "w_out_gla": w_out[l][ATT_WIDTH:].astype(BF16),
        "norm_ffn2": row(norm_ffn2[l]), "w_ffn2_gu": gu_layout(w_ffn2_gu[l]), "w_ffn2_down": w_ffn2_down[l].astype(BF16),
        "norm_out": row(norm_out[l]),
    }


def _layer(x, p, tabs):
    b, s, d = x.shape
    t = b * s
    h = _ffn(x.reshape(t, d), p["norm_ffn1"], p["w_ffn1_gu"], p["w_ffn1_down"], p["norm_out"], False)
    qt, k, vt, ql, kl, vl, rl, laf, lab = _inproj(h.reshape(b, s, d), p, tabs)
    o_att = _attn(qt, k, vt)
    o_f, o_b = _gla(ql, kl, vl, laf, lab)
    h = _outproj(h, o_att.reshape(t, ATT_WIDTH), o_f.reshape(t, GLA_WIDTH), o_b.reshape(t, GLA_WIDTH),
                 rl.reshape(t, GLA_WIDTH), p["gla_gain"], p["w_out_att"], p["w_out_gla"])
    y = _ffn(h, p["norm_ffn2"], p["w_ffn2_gu"], p["w_ffn2_down"], p["norm_out"], True)
    return y.reshape(b, s, d)


def _trunk(x, layers):
    tabs = _rope_tables(x.shape[1])
    h = x
    for p in layers:
        h = _layer(h, p, tabs)
    return h


def kernel(x_prompt, x_sample, norm_ffn1, w_ffn1_gu, w_ffn1_down, norm_mix, w_in, q_norm, k_norm, w_gate_f, b_gate_f,
           w_gate_b, b_gate_b, gla_norm, w_out, norm_ffn2, w_ffn2_gu, w_ffn2_down, norm_out):
    params = (norm_ffn1, w_ffn1_gu, w_ffn1_down, norm_mix, w_in, q_norm, k_norm, w_gate_f, b_gate_f,
              w_gate_b, b_gate_b, gla_norm, w_out, norm_ffn2, w_ffn2_gu, w_ffn2_down, norm_out)
    layers = [_prep_layer(l, *params) for l in range(norm_ffn1.shape[0])]
    return (_trunk(x_prompt, layers), _trunk(x_sample, layers))
```

```python
import functools

import jax
import jax.numpy as jnp
from jax import lax
from jax.experimental import pallas as pl
from jax.experimental.pallas import tpu as pltpu

F32 = jnp.float32
BF16 = jnp.bfloat16

D_MODEL = 1024
GRID_W = 64
ATT_WIDTH = 512
HEAD_DIM = 64
N_HEADS = 8
N_KV_HEADS = 2
KV_GROUP = 4
AXIS_DIM = 32
ROPE_THETA = 10000.0
GLA_WIDTH = 512
GLA_HEADS = 4
GLA_DV = 128
GLA_DK = 64
GLA_KEY_WIDTH = 256
GATE_RANK = 16
GATE_TAU = 16.0
CHUNK = 64
D_FF = 2816
EPS = 1e-6

LANES = 128
KV_W = N_KV_HEADS * HEAD_DIM
QV_ROWS = ATT_WIDTH + KV_W
REST_W = KV_W + 2 * GLA_KEY_WIDTH + 2 * GLA_WIDTH + LANES
GATE_OFF = REST_W - LANES

TM = 512
TQ = 256
SCORE_LEAD = 1
FF_CHUNK = 256
N_FF_CHUNKS = D_FF // FF_CHUNK
GLA_BLOCK = 512
NEG_BIG = -1e30
LOG2E = 1.4426950408889634
V_ROWS = HEAD_DIM + 16
VMEM_LIMIT = 56 * 1024 * 1024


def _cparams(sem):
    return pltpu.CompilerParams(dimension_semantics=sem, vmem_limit_bytes=VMEM_LIMIT)


def _rms_rows(x, gain_row):
    ms = jnp.mean(x * x, axis=-1, keepdims=True)
    return x * lax.rsqrt(ms + EPS) * gain_row


def _bdot(a, b):
    return jnp.dot(a, b, preferred_element_type=F32)


def _nt_dot(a, b):
    return lax.dot_general(a, b, (((1,), (1,)), ((), ())), preferred_element_type=F32)


def _ffn_body(x_ref, g_ref, wgu_ref, wd_ref, og_ref, o_ref, *, final_norm):
    x = x_ref[...]
    xn = _rms_rows(x, g_ref[...]).astype(BF16)
    acc = None
    for c in range(N_FF_CHUNKS):
        g = _bdot(xn, wgu_ref[:, c * FF_CHUNK:(c + 1) * FF_CHUNK])
        u = _bdot(xn, wgu_ref[:, D_FF + c * FF_CHUNK:D_FF + (c + 1) * FF_CHUNK])
        a =(g * (1.0 / (1.0 + jnp.exp(-g))) * u).astype(BF16)
        d = _bdot(a, wd_ref[c * FF_CHUNK:(c + 1) * FF_CHUNK, :])
        acc = d if acc is None else acc + d
    h = x + 0.5 * acc
    if final_norm:
        h = _rms_rows(h, og_ref[...])
    o_ref[...] = h


def _ffn(x, gain, wgu, wd, out_gain, final_norm):
    t = x.shape[0]
    const = lambda i: (0, 0)
    return pl.pallas_call(
        functools.partial(_ffn_body, final_norm=final_norm),
        grid=(t // TM,),
        in_specs=[
            pl.BlockSpec((TM, D_MODEL), lambda i: (i, 0)),
            pl.BlockSpec((1, D_MODEL), const),
            pl.BlockSpec((D_MODEL, 2 * D_FF), const, pipeline_mode=pl.Buffered(1)),
            pl.BlockSpec((D_FF, D_MODEL), const, pipeline_mode=pl.Buffered(1)),
            pl.BlockSpec((1, D_MODEL), const),
        ],
        out_specs=pl.BlockSpec((TM, D_MODEL), lambda i: (i, 0)),
        out_shape=jax.ShapeDtypeStruct((t, D_MODEL), F32),
        compiler_params=_cparams(("parallel",)),
        name="ffn_final" if final_norm else "ffn",
    )(x, gain, wgu, wd, out_gain)


def _swap16(t, axis):
    n = t.shape[axis] // 16
    parts = [lax.slice_in_dim(t, (i ^ 1) * 16, (i ^ 1) * 16 + 16, axis=axis) for i in range(n)]
    return jnp.concatenate(parts, axis=axis)


def _inproj_body(h_ref, g_ref, wqv_ref, wrest_ref, qg_ref, kg_ref, cost_ref, sint_ref, cosn_ref, sinn_ref,
                 grp_ref, wgate_ref, bgate_ref,
                 qt_ref, k_ref, vt_ref, ql_ref, kl_ref, vl_ref, rl_ref, laf_ref, lab_ref):
    xn = _rms_rows(h_ref[0], g_ref[...]).astype(BF16)
    tm = xn.shape[0]

    ut = _nt_dot(wqv_ref[...], xn)
    q = ut[:ATT_WIDTH].reshape(N_HEADS, HEAD_DIM, tm)
    q = q * lax.rsqrt(jnp.mean(q * q, axis=1, keepdims=True) + EPS) * qg_ref[...][None]
    q = q * cost_ref[...][None] + _swap16(q, 1) * sint_ref[...][None]
    qt_ref[0] = (q * (HEAD_DIM ** -0.5 * LOG2E)).reshape(ATT_WIDTH, tm).astype(BF16)

    row = lax.broadcasted_iota(jnp.int32, (V_ROWS - HEAD_DIM, tm), 0)
    ones_rows = jnp.where(row == 0, 1.0, 0.0).astype(BF16)
    for kvh in range(N_KV_HEADS):
        v = ut[ATT_WIDTH + kvh * HEAD_DIM:ATT_WIDTH + (kvh + 1) * HEAD_DIM].astype(BF16)
        vt_ref[0, kvh, 0] = jnp.concatenate([v, ones_rows], axis=0)

    rest = _bdot(xn, wrest_ref[...])

    k = rest[:, :KV_W]
    ss = k * k
    ss_hi = ss.astype(BF16)
    ss_lo = (ss - ss_hi.astype(F32)).astype(BF16)
    ms = _bdot(ss_hi, grp_ref[...]) + _bdot(ss_lo, grp_ref[...])
    k = k * lax.rsqrt(ms + EPS) * kg_ref[...]
    lane = lax.broadcasted_iota(jnp.int32, k.shape, 1)
    k_sw = jnp.where((lane % 32) < 16, pltpu.roll(k, LANES - 16, 1), pltpu.roll(k, 16, 1))
    k_ref[0] = (k * cosn_ref[...] + k_sw * sinn_ref[...]).astype(BF16)

    o = KV_W
    ql_ref[0] = rest[:, o:o + GLA_KEY_WIDTH]
    o += GLA_KEY_WIDTH
    kl_ref[0] = rest[:, o:o + GLA_KEY_WIDTH]
    o += GLA_KEY_WIDTH
    vl_ref[0] = rest[:, o:o + GLA_WIDTH].astype(BF16)
    o += GLA_WIDTH
    rl_ref[0] = rest[:, o:o + GLA_WIDTH]

    pre = _bdot(rest[:, GATE_OFF:].astype(BF16), wgate_ref[...]) + bgate_ref[...]
    logsig = jnp.minimum(pre, 0.0) - jnp.log(1.0 + jnp.exp(-jnp.abs(pre)))
    la = logsig * (1.0 / GATE_TAU)
    laf_ref[0] = la[:, :GLA_KEY_WIDTH]
    lab_ref[0] = la[:, GLA_KEY_WIDTH:]


def _inproj(h, p, tabs):
    b, s, _ = h.shape
    nt = s // TM
    c2 = lambda i, j: (0, 0)
    tok = lambda w: pl.BlockSpec((1, TM, w), lambda i, j: (i, j, 0))
    out_shape = (
        jax.ShapeDtypeStruct((b, ATT_WIDTH, s), BF16),
        jax.ShapeDtypeStruct((b, s, KV_W), BF16),
        jax.ShapeDtypeStruct((b, N_KV_HEADS, nt, V_ROWS, TM), BF16),
        jax.ShapeDtypeStruct((b, s, GLA_KEY_WIDTH), F32),
        jax.ShapeDtypeStruct((b, s, GLA_KEY_WIDTH), F32),
        jax.ShapeDtypeStruct((b, s, GLA_WIDTH), BF16),
        jax.ShapeDtypeStruct((b, s, GLA_WIDTH), F32),
        jax.ShapeDtypeStruct((b, s, GLA_KEY_WIDTH), F32),
        jax.ShapeDtypeStruct((b, s, GLA_KEY_WIDTH), F32),
    )
    out_specs = (
        pl.BlockSpec((1, ATT_WIDTH, TM), lambda i, j: (i, 0, j)),
        tok(KV_W),
        pl.BlockSpec((1, N_KV_HEADS, 1, V_ROWS, TM), lambda i, j: (i, 0, j, 0, 0)),
        tok(GLA_KEY_WIDTH), tok(GLA_KEY_WIDTH), tok(GLA_WIDTH), tok(GLA_WIDTH), tok(GLA_KEY_WIDTH),
        tok(GLA_KEY_WIDTH),
    )
    in_specs = [
        tok(D_MODEL),
        pl.BlockSpec((1, D_MODEL), c2),
        pl.BlockSpec((QV_ROWS, D_MODEL), c2),
        pl.BlockSpec((D_MODEL, REST_W), c2),
        pl.BlockSpec((HEAD_DIM, 1), c2),
        pl.BlockSpec((1, KV_W), c2),
        pl.BlockSpec((HEAD_DIM, TM), lambda i, j: (0, j)),
        pl.BlockSpec((HEAD_DIM, TM), lambda i, j: (0, j)),
        pl.BlockSpec((TM, KV_W), lambda i, j: (j, 0)),
        pl.BlockSpec((TM, KV_W), lambda i, j: (j, 0)),
        pl.BlockSpec((KV_W, KV_W), c2),
        pl.BlockSpec((LANES, 2 * GLA_KEY_WIDTH), c2),
        pl.BlockSpec((1, 2 * GLA_KEY_WIDTH), c2),
    ]
    return pl.pallas_call(
        _inproj_body,
        grid=(b, nt),
        in_specs=in_specs,
        out_specs=out_specs,
        out_shape=out_shape,
        compiler_params=_cparams(("parallel", "parallel")),
        name="inproj",
    )(h, p["norm_mix"], p["w_qv_t"], p["w_rest"], p["q_gain_col"], p["k_gain_row"],
      tabs["cos_t"], tabs["sin_t"], tabs["cos_n"], tabs["sin_n"], p["grp_avg"], p["w_gate"], p["b_gate"])


def _attn_body(qt_ref, k_ref, vt_ref, o_ref, qcat_ref, acc_ref, *bufs, n_chunks):
    kvh = pl.program_id(1)
    tq = qt_ref.shape[2]
    cols = [slice(g * tq, (g + 1) * tq) for g in range(KV_GROUP)]

    qblk = qt_ref[0]
    top = jnp.concatenate([qblk[g * HEAD_DIM:(g + 1) * HEAD_DIM] for g in range(KV_GROUP)], axis=1)
    zero = jnp.zeros_like(top)
    qcat_ref[...] = jnp.where(kvh == 0, jnp.concatenate([top, zero], axis=0), jnp.concatenate([zero, top], axis=0))
    acc_ref[...] = jnp.zeros(acc_ref.shape, F32)

    def scores(c, dst_ref):
        kc = k_ref[0, pl.ds(pl.multiple_of(c * TM, TM), TM), :]
        maxes = []
        for sl in cols:
            s = _bdot(kc, qcat_ref[:, sl])
            dst_ref[:, sl] = s
            maxes.append(jnp.max(s, axis=0, keepdims=True))
        return jnp.concatenate(maxes, axis=1)

    def absorb(c, src_ref, chunk_max, m_old):
        m_new = jnp.maximum(m_old, chunk_max)
        alpha = jnp.exp2(m_old - m_new)
        vt = vt_ref[0, 0, c]
        acc_ref[...] = alpha * acc_ref[...]
        for sl in cols:
            p = jnp.exp2(src_ref[:, sl] - m_new[:, sl]).astype(BF16)
            acc_ref[:, sl] += _bdot(vt, p)
        return m_new

    n_buf = len(bufs)

    def group(first, carry, last_group):
        m, pending = carry[0], list(carry[1:])
        for j in range(n_buf):
            c = first + j
            if not (last_group and j + SCORE_LEAD >= n_buf):
                pending.append(scores(c + SCORE_LEAD, bufs[(j + SCORE_LEAD) % n_buf]))
            m = absorb(c, bufs[j], pending.pop(0), m)
        return (m, *pending)

    carry = (jnp.full((1, KV_GROUP * tq), NEG_BIG, F32), *[scores(c, bufs[c]) for c in range(SCORE_LEAD)])
    carry = lax.fori_loop(0, n_chunks // n_buf - 1, lambda i, cr: group(i * n_buf, cr, False), carry)
    group(n_chunks - n_buf, carry, True)

    acc = acc_ref[...]
    out_t = acc[:HEAD_DIM] / acc[HEAD_DIM:HEAD_DIM + 1]
    heads = [out_t[:, sl].T for sl in cols]
    o_ref[0] = jnp.concatenate(heads, axis=1).astype(BF16)


def _attn(qt, k, vt):
    b, _, s = qt.shape
    n_chunks = s // TM
    n_buf = 4 if n_chunks % 4 == 0 and n_chunks >= 16 else 2
    assert n_chunks % n_buf == 0
    tq = min(TQ, s)
    mq = KV_GROUP * tq
    return pl.pallas_call(
        functools.partial(_attn_body, n_chunks=n_chunks),
        grid=(b, N_KV_HEADS, s // tq),
        in_specs=[
            pl.BlockSpec((1, KV_GROUP * HEAD_DIM, tq), lambda i, h, j: (i, h, j)),
            pl.BlockSpec((1, s, KV_W), lambda i, h, j: (i, 0, 0)),
            pl.BlockSpec((1, 1, n_chunks, V_ROWS, TM), lambda i, h, j: (i, h, 0, 0, 0)),
        ],
        out_specs=pl.BlockSpec((1, tq, KV_GROUP * HEAD_DIM), lambda i, h, j: (i, j, h)),
        out_shape=jax.ShapeDtypeStruct((b, s, ATT_WIDTH), BF16),
        scratch_shapes=[
            pltpu.VMEM((KV_W, mq), BF16),
            pltpu.VMEM((V_ROWS, mq), F32),
        ] + [pltpu.VMEM((TM, mq), F32)] * n_buf,
        compiler_params=_cparams(("parallel", "parallel", "arbitrary")),
        name="attn",
    )(qt, k, vt)


def _gla_direction(q_ref, k_ref, v_ref, la_ref, o_ref, state_ref, order, cum, keep, ref_idx, last_idx):
    heads = range(GLA_HEADS)
    psl = [slice(p * LANES, (p + 1) * LANES) for p in range(GLA_HEADS * GLA_DK // LANES)]
    vsl =[slice(h * GLA_DV, (h + 1) * GLA_DV) for h in heads]
    rows = [slice(ci * CHUNK, (ci + 1) * CHUNK) for ci in order]
    n = range(len(rows))

    b = []
    for sl in rows:
        la = la_ref[0, sl, :]
        la_hi = la.astype(BF16)
        la_lo = (la - la_hi.astype(F32)).astype(BF16)
        b.append(_bdot(cum, la_hi) + _bdot(cum, la_lo))

    qr, kr, kd, qb, dec, v = [], [], [], [], [], []
    for sl, bc in zip(rows, b):
        b_ref = bc[ref_idx:ref_idx + 1]
        b_last = bc[last_idx:last_idx + 1]
        qs = q_ref[0, sl, :] * (GLA_DK ** -0.5)
        k = k_ref[0, sl, :]
        qr.append((qs * jnp.exp(bc - b_ref)).astype(BF16))
        kr.append((k * jnp.exp(b_ref - bc)).astype(BF16))
        kd.append((k * jnp.exp(b_last - bc)).astype(BF16))
        qb.append((qs * jnp.exp(bc)).astype(BF16))
        dec.append(jnp.exp(b_last))
        v.append(v_ref[0, sl, :])

    lane = lax.broadcasted_iota(jnp.int32, (1, LANES), 1)
    own = [lane < GLA_DK, lane >= GLA_DK]
    keep2 = jnp.concatenate([keep, keep], axis=0)
    a = [[None] * GLA_HEADS for _ in n]
    for i in n:
        for p, sl in enumerate(psl):
            qr_p = qr[i][:, sl]
            stacked = jnp.concatenate([jnp.where(own[0], qr_p, 0), jnp.where(own[1], qr_p, 0)], axis=0)
            a2 = jnp.where(keep2, _nt_dot(stacked, kr[i][:, sl]), 0.0).astype(BF16)
            a[i][2 * p] = a2[:CHUNK]
            a[i][2 * p + 1] = a2[CHUNK:]
    v_t = [[v[i][:, vsl[h]].astype(F32).T.astype(BF16) for h in heads] for i in n]
    o_intra = [[_bdot(a[i][h], v[i][:, vsl[h]]) for h in heads] for i in n]
    d_state = [[_bdot(v_t[i][h], jnp.where(own[h % 2], kd[i][:, psl[h // 2]], 0)) for h in heads] for i in n]

    state = [state_ref[h] for h in heads]
    entering = []
    for i in n:
        entering.append(state)
        state = [dec[i][:, psl[h // 2]] * state[h] + d_state[i][h] for h in heads]
    for h in heads:
        state_ref[h] = state[h]

    o_inter = [[_nt_dot(qb[i][:, psl[h // 2]], entering[i][h].astype(BF16)) for h in heads] for i in n]
    for i in n:
        o_ref[0, rows[i], :] = jnp.concatenate([o_intra[i][h] + o_inter[i][h] for h in heads], axis=1)


def _gla_body(qf_ref, kf_ref, vf_ref, laf_ref, qb_ref, kb_ref, vb_ref, lab_ref, of_ref, ob_ref, sf_ref, sb_ref,
              *, n_chunks):
    @pl.when(pl.program_id(1) == 0)
    def _():
        sf_ref[...] = jnp.zeros(sf_ref.shape, F32)
        sb_ref[...] = jnp.zeros(sb_ref.shape, F32)

    r = lax.broadcasted_iota(jnp.int32, (CHUNK, CHUNK), 0)
    c = lax.broadcasted_iota(jnp.int32, (CHUNK, CHUNK), 1)
    _gla_direction(qf_ref, kf_ref, vf_ref, laf_ref, of_ref, sf_ref, range(n_chunks),
                   jnp.where(c <= r, 1.0, 0.0).astype(BF16), c <= r, CHUNK // 2, CHUNK - 1)
    _gla_direction(qb_ref, kb_ref, vb_ref, lab_ref, ob_ref, sb_ref, range(n_chunks - 1, -1, -1),
                   jnp.where(c >= r, 1.0, 0.0).astype(BF16), c > r, CHUNK // 2 - 1, 0)


def _gla(ql, kl, vl, laf, lab):
    b, s, _ = ql.shape
    tb = min(GLA_BLOCK, s)
    nb = s // tb
    fwd = lambda w: pl.BlockSpec((1, tb, w), lambda i, j: (i, j, 0))
    bwd = lambda w: pl.BlockSpec((1, tb, w), lambda i, j: (i, nb - 1 - j, 0))
    return pl.pallas_call(
        functools.partial(_gla_body, n_chunks=tb // CHUNK),
        grid=(b, nb),
        in_specs=[fwd(GLA_KEY_WIDTH), fwd(GLA_KEY_WIDTH), fwd(GLA_WIDTH), fwd(GLA_KEY_WIDTH),
                  bwd(GLA_KEY_WIDTH), bwd(GLA_KEY_WIDTH), bwd(GLA_WIDTH), bwd(GLA_KEY_WIDTH)],
        out_specs=(fwd(GLA_WIDTH), bwd(GLA_WIDTH)),
        out_shape=(jax.ShapeDtypeStruct((b, s, GLA_WIDTH), F32), jax.ShapeDtypeStruct((b, s, GLA_WIDTH), F32)),
        scratch_shapes=[pltpu.VMEM((GLA_HEADS, GLA_DV, LANES), F32), pltpu.VMEM((GLA_HEADS, GLA_DV, LANES), F32)],
        compiler_params=_cparams(("parallel", "arbitrary")),
        name="gla",
    )(ql, kl, vl, laf, ql, kl, vl, lab)


def _outproj_body(h_ref, oa_ref, of_ref, ob_ref, r_ref, gg_ref, wa_ref, wg_ref, o_ref):
    o = of_ref[...] + ob_ref[...]
    gain = gg_ref[...]
    r = r_ref[...]
    parts = []
    for hh in range(GLA_HEADS):
        sl = slice(hh * GLA_DV, (hh + 1) * GLA_DV)
        rh = r[:, sl]
        parts.append((_rms_rows(o[:, sl], gain[:, sl]) * (rh * (1.0 / (1.0 + jnp.exp(-rh))))).astype(BF16))
    og = jnp.concatenate(parts, axis=1)
    o_ref[...] = h_ref[...] + _bdot(oa_ref[...], wa_ref[...]) + _bdot(og, wg_ref[...])


def _outproj(h, o_att, o_f, o_b, r, gla_gain, w_att, w_gla):
    t = h.shape[0]
    const = lambda i: (0, 0)
    tok = lambda w: pl.BlockSpec((TM, w), lambda i: (i, 0))
    return pl.pallas_call(
        _outproj_body,
        grid=(t // TM,),
        in_specs=[tok(D_MODEL), tok(ATT_WIDTH), tok(GLA_WIDTH), tok(GLA_WIDTH), tok(GLA_WIDTH),
                  pl.BlockSpec((1, GLA_WIDTH), const),
                  pl.BlockSpec((ATT_WIDTH, D_MODEL), const),
                  pl.BlockSpec((GLA_WIDTH, D_MODEL), const)],
        out_specs=tok(D_MODEL),
        out_shape=jax.ShapeDtypeStruct((t, D_MODEL), F32),
        compiler_params=_cparams(("parallel",)),
        name="outproj",
    )(h, o_att, o_f, o_b, r, gla_gain, w_att, w_gla)


def _rope_tables(seq_len):
    rows = seq_len // GRID_W
    row = jnp.repeat(jnp.arange(rows, dtype=F32), GRID_W)
    col = jnp.tile(jnp.arange(GRID_W, dtype=F32), rows)
    inv_freq = 1.0 / (ROPE_THETA ** (jnp.arange(0, AXIS_DIM, 2, dtype=F32) / AXIS_DIM))
    ang_r = row[:, None] * inv_freq[None, :]
    ang_c = col[:, None] * inv_freq[None, :]
    ang = jnp.concatenate([ang_r, ang_r, ang_c, ang_c], axis=-1)
    sign = jnp.where((jnp.arange(HEAD_DIM) % 32) < 16, -1.0, 1.0).astype(F32)
    cos, sin = jnp.cos(ang), jnp.sin(ang) * sign[None, :]
    return {
        "cos_t": cos.T, "sin_t": sin.T,
        "cos_n": jnp.tile(cos, (1, N_KV_HEADS)), "sin_n": jnp.tile(sin, (1, N_KV_HEADS)),
    }


def _prep_layer(l, norm_ffn1, w_ffn1_gu, w_ffn1_down, norm_mix, w_in, q_norm, k_norm, w_gate_f, b_gate_f,
                w_gate_b, b_gate_b, gla_norm, w_out, norm_ffn2, w_ffn2_gu, w_ffn2_down, norm_out):
    w = w_in[l]
    o_q, o_k, o_v = 0, ATT_WIDTH, ATT_WIDTH + KV_W
    o_rest = ATT_WIDTH + 2 * KV_W
    n_rest = 2 * GLA_KEY_WIDTH + 2 * GLA_WIDTH
    gates = jnp.pad(w[:, o_rest + n_rest:], ((0, 0), (0, LANES - 2 * GATE_RANK)))
    w_rest = jnp.concatenate([w[:, o_k:o_v], w[:, o_rest:o_rest + n_rest], gates], axis=1).astype(BF16)
    w_qv_t = jnp.concatenate([w[:, o_q:o_k], w[:, o_v:o_rest]], axis=1).T.astype(BF16)
    w_gate = jnp.zeros((LANES, 2 * GLA_KEY_WIDTH), F32)
    w_gate = w_gate.at[:GATE_RANK, :GLA_KEY_WIDTH].set(w_gate_f[l])
    w_gate = w_gate.at[GATE_RANK:2 * GATE_RANK, GLA_KEY_WIDTH:].set(w_gate_b[l])
    hid = jnp.arange(KV_W) // HEAD_DIM
    grp_avg = jnp.where(hid[:, None] == hid[None, :], 1.0 / HEAD_DIM, 0.0).astype(BF16)
    row = lambda v: v.reshape(1, -1).astype(F32)
    return {
        "norm_ffn1": row(norm_ffn1[l]), "w_ffn1_gu": w_ffn1_gu[l].astype(BF16), "w_ffn1_down": w_ffn1_down[l].astype(BF16),
        "norm_mix": row(norm_mix[l]), "w_qv_t": w_qv_t, "w_rest": w_rest,
        "q_gain_col": q_norm[l].reshape(HEAD_DIM, 1).astype(F32),
        "k_gain_row": row(jnp.tile(k_norm[l], N_KV_HEADS)),
        "grp_avg": grp_avg, "w_gate": w_gate.astype(BF16),
        "b_gate": row(jnp.concatenate([b_gate_f[l], b_gate_b[l]])),
        "gla_gain": row(jnp.tile(gla_norm[l], GLA_HEADS)),
        "w_out_att": w_out[l][:ATT_WIDTH].astype(BF16), "w_out_gla": w_out[l][ATT_WIDTH:].astype(BF16),
        "norm_ffn2": row(norm_ffn2[l]), "w_ffn2_gu": w_ffn2_gu[l].astype(BF16), "w_ffn2_down": w_ffn2_down[l].astype(BF16),
        "norm_out": row(norm_out[l]),
    }


def _layer(x, p, tabs):
    b, s, d = x.shape
    t = b * s
    h = _ffn(x.reshape(t, d), p["norm_ffn1"], p["w_ffn1_gu"], p["w_ffn1_down"], p["norm_out"], False)
    qt, k, vt, ql, kl, vl, rl, laf, lab = _inproj(h.reshape(b, s, d), p, tabs)
    o_att = _attn(qt, k, vt)
    o_f, o_b = _gla(ql, kl, vl, laf, lab)
    h = _outproj(h, o_att.reshape(t, ATT_WIDTH), o_f.reshape(t, GLA_WIDTH), o_b.reshape(t, GLA_WIDTH),
                 rl.reshape(t, GLA_WIDTH), p["gla_gain"], p["w_out_att"], p["w_out_gla"])
    y = _ffn(h, p["norm_ffn2"], p["w_ffn2_gu"], p["w_ffn2_down"], p["norm_out"], True)
    return y.reshape(b, s, d)


def _trunk(x, layers):
    tabs = _rope_tables(x.shape[1])
    h = x
    for p in layers:
        h = _layer(h, p, tabs)
    return h


def kernel(x_prompt, x_sample, norm_ffn1, w_ffn1_gu, w_ffn1_down, norm_mix, w_in, q_norm, k_norm, w_gate_f, b_gate_f,
           w_gate_b, b_gate_b, gla_norm, w_out, norm_ffn2, w_ffn2_gu, w_ffn2_down, norm_out):
    params = (norm_ffn1, w_ffn1_gu, w_ffn1_down, norm_mix, w_in, q_norm, k_norm, w_gate_f, b_gate_f,
              w_gate_b, b_gate_b, gla_norm, w_out, norm_ffn2, w_ffn2_gu, w_ffn2_down, norm_out)
    layers = [_prep_layer(l, *params) for l in range(norm_ffn1.shape[0])]
    return (_trunk(x_prompt, layers), _trunk(x_sample, layers))
```

```python
import functools

import jax
import jax.numpy as jnp
from jax import lax
from jax.experimental import pallas as pl
from jax.experimental.pallas import tpu as pltpu

F32 = jnp.float32
BF16 = jnp.bfloat16

D_MODEL = 1024
GRID_W = 64
ATT_WIDTH = 512
HEAD_DIM = 64
N_HEADS = 8
N_KV_HEADS = 2
KV_GROUP = 4
AXIS_DIM = 32
ROPE_THETA = 10000.0
GLA_WIDTH = 512
GLA_HEADS = 4
GLA_DV = 128
GLA_DK = 64
GLA_KEY_WIDTH = 256
GATE_RANK = 16
GATE_TAU = 16.0
CHUNK = 64
D_FF = 2816
EPS = 1e-6

LANES = 128
KV_W = N_KV_HEADS * HEAD_DIM
QV_ROWS = ATT_WIDTH + KV_W
REST_W = KV_W + 2 * GLA_KEY_WIDTH + 2 * GLA_WIDTH + LANES
GATE_OFF = REST_W - LANES

TM = 512
TQ = 256
SCORE_LEAD = 1
FF_CHUNK = 256
N_FF_CHUNKS = D_FF // FF_CHUNK
GLA_BLOCK = 512
NEG_BIG = -1e30
LOG2E = 1.4426950408889634
V_ROWS = HEAD_DIM + 16
VMEM_LIMIT = 56 * 1024 * 1024


def _cparams(sem):
    return pltpu.CompilerParams(dimension_semantics=sem, vmem_limit_bytes=VMEM_LIMIT)


def _rms_rows(x, gain_row):
    ms = jnp.mean(x * x, axis=-1, keepdims=True)
    return x * lax.rsqrt(ms + EPS) * gain_row


def _bdot(a, b):
    return jnp.dot(a, b, preferred_element_type=F32)


def _nt_dot(a, b):
    return lax.dot_general(a, b, (((1,), (1,)), ((), ())), preferred_element_type=F32)


def _swiglu_half_step(x, gain_row, wgu_ref, wd_ref):
    xn = _rms_rows(x, gain_row).astype(BF16)
    acc = None
    for c in range(N_FF_CHUNKS):
        g = _bdot(xn, wgu_ref[:, c * FF_CHUNK:(c + 1) * FF_CHUNK])
        u = _bdot(xn, wgu_ref[:, D_FF + c * FF_CHUNK:D_FF + (c + 1) * FF_CHUNK])
        a = (g * (1.0 / (1.0 + jnp.exp(-g))) * u).astype(BF16)
        d = _bdot(a, wd_ref[c * FF_CHUNK:(c + 1) * FF_CHUNK, :])
        acc = d if acc is None else acc + d
    return x + 0.5 * acc


def _resident(shape):
    return pl.BlockSpec(shape, lambda *_: (0,) * len(shape), pipeline_mode=pl.Buffered(1))


def _swap16(t, axis):
    n = t.shape[axis] // 16
    parts = [lax.slice_in_dim(t, (i ^ 1) * 16, (i ^ 1) * 16 + 16, axis=axis) for i in range(n)]
    return jnp.concatenate(parts, axis=axis)


def _ffn_inproj_body(x_ref, g1_ref, wgu_ref, wd_ref, g_ref, wqv_ref, wrest_ref, qg_ref, kg_ref,
                     cost_ref, sint_ref, cosn_ref, sinn_ref, grp_ref, wgate_ref, bgate_ref,
                     h_ref, qt_ref, k_ref, vt_ref, ql_ref, kl_ref, vl_ref, rl_ref, laf_ref, lab_ref):
    h = _swiglu_half_step(x_ref[0], g1_ref[...], wgu_ref, wd_ref)
    h_ref[0] = h
    xn = _rms_rows(h, g_ref[...]).astype(BF16)
    tm = xn.shape[0]

    ut = _nt_dot(wqv_ref[...], xn)
    q = ut[:ATT_WIDTH].reshape(N_HEADS, HEAD_DIM, tm)
    q = q * lax.rsqrt(jnp.mean(q * q, axis=1, keepdims=True) + EPS) * qg_ref[...][None]
    q = q * cost_ref[...][None] + _swap16(q, 1) * sint_ref[...][None]
    qt_ref[0] = (q * (HEAD_DIM ** -0.5 * LOG2E)).reshape(ATT_WIDTH, tm).astype(BF16)

    row = lax.broadcasted_iota(jnp.int32, (V_ROWS - HEAD_DIM, tm), 0)
    ones_rows = jnp.where(row == 0, 1.0, 0.0).astype(BF16)
    for kvh in range(N_KV_HEADS):
        v = ut[ATT_WIDTH + kvh * HEAD_DIM:ATT_WIDTH + (kvh + 1) * HEAD_DIM].astype(BF16)
        vt_ref[0, kvh, 0] = jnp.concatenate([v, ones_rows], axis=0)

    rest = _bdot(xn, wrest_ref[...])

    k = rest[:, :KV_W]
    ss = k * k
    ss_hi = ss.astype(BF16)
    ss_lo = (ss - ss_hi.astype(F32)).astype(BF16)
    ms = _bdot(ss_hi, grp_ref[...]) + _bdot(ss_lo, grp_ref[...])
    k = k * lax.rsqrt(ms + EPS) * kg_ref[...]
    lane = lax.broadcasted_iota(jnp.int32, k.shape, 1)
    k_sw = jnp.where((lane % 32) < 16, pltpu.roll(k, LANES - 16, 1), pltpu.roll(k, 16, 1))
    k_ref[0] = (k * cosn_ref[...] + k_sw * sinn_ref[...]).astype(BF16)

    o = KV_W
    ql_ref[0] = rest[:, o:o + GLA_KEY_WIDTH]
    o += GLA_KEY_WIDTH
    kl_ref[0] = rest[:, o:o + GLA_KEY_WIDTH]
    o += GLA_KEY_WIDTH
    vl_ref[0] = rest[:, o:o + GLA_WIDTH].astype(BF16)
    o += GLA_WIDTH
    rl_ref[0] = rest[:, o:o + GLA_WIDTH]

    pre = _bdot(rest[:, GATE_OFF:].astype(BF16), wgate_ref[...]) + bgate_ref[...]
    logsig = jnp.minimum(pre, 0.0) - jnp.log(1.0 + jnp.exp(-jnp.abs(pre)))
    la = logsig * (1.0 / GATE_TAU)
    laf_ref[0] = la[:, :GLA_KEY_WIDTH]
    lab_ref[0] = la[:, GLA_KEY_WIDTH:]


def _ffn_inproj(x, p, tabs):
    b, s, _ = x.shape
    nt = s // TM
    c2 = lambda i, j: (0, 0)
    tok = lambda w: pl.BlockSpec((1, TM, w), lambda i, j: (i, j, 0))
    out_shape = (
        jax.ShapeDtypeStruct((b, s, D_MODEL), F32),
        jax.ShapeDtypeStruct((b, ATT_WIDTH, s), BF16),
        jax.ShapeDtypeStruct((b, s, KV_W), BF16),
        jax.ShapeDtypeStruct((b, N_KV_HEADS, nt, V_ROWS, TM), BF16),
        jax.ShapeDtypeStruct((b, s, GLA_KEY_WIDTH), F32),
        jax.ShapeDtypeStruct((b, s, GLA_KEY_WIDTH), F32),
        jax.ShapeDtypeStruct((b, s, GLA_WIDTH), BF16),
        jax.ShapeDtypeStruct((b, s, GLA_WIDTH), F32),
        jax.ShapeDtypeStruct((b, s, GLA_KEY_WIDTH), F32),
        jax.ShapeDtypeStruct((b, s, GLA_KEY_WIDTH), F32),
    )
    out_specs = (
        tok(D_MODEL),
        pl.BlockSpec((1, ATT_WIDTH, TM), lambda i, j: (i, 0, j)),
        tok(KV_W),
        pl.BlockSpec((1, N_KV_HEADS, 1, V_ROWS, TM), lambda i, j: (i, 0, j, 0, 0)),
        tok(GLA_KEY_WIDTH), tok(GLA_KEY_WIDTH), tok(GLA_WIDTH), tok(GLA_WIDTH), tok(GLA_KEY_WIDTH),
        tok(GLA_KEY_WIDTH),
    )
    in_specs = [
        tok(D_MODEL),
        pl.BlockSpec((1, D_MODEL), c2),
        _resident((D_MODEL, 2 * D_FF)),
        _resident((D_FF, D_MODEL)),
        pl.BlockSpec((1, D_MODEL), c2),
        _resident((QV_ROWS, D_MODEL)),
        _resident((D_MODEL, REST_W)),
        pl.BlockSpec((HEAD_DIM, 1), c2),
        pl.BlockSpec((1, KV_W), c2),
        pl.BlockSpec((HEAD_DIM, TM), lambda i, j: (0, j)),
        pl.BlockSpec((HEAD_DIM, TM), lambda i, j: (0, j)),
        pl.BlockSpec((TM, KV_W), lambda i, j: (j, 0)),
        pl.BlockSpec((TM, KV_W), lambda i, j: (j, 0)),
        pl.BlockSpec((KV_W, KV_W), c2),
        pl.BlockSpec((LANES, 2 * GLA_KEY_WIDTH), c2),
        pl.BlockSpec((1, 2 * GLA_KEY_WIDTH), c2),
    ]
    return pl.pallas_call(
        _ffn_inproj_body,
        grid=(b, nt),
        in_specs=in_specs,
        out_specs=out_specs,
        out_shape=out_shape,
        compiler_params=_cparams(("parallel", "parallel")),
        name="ffn_inproj",
    )(x, p["norm_ffn1"], p["w_ffn1_gu"], p["w_ffn1_down"],
      p["norm_mix"], p["w_qv_t"], p["w_rest"], p["q_gain_col"], p["k_gain_row"],
      tabs["cos_t"], tabs["sin_t"], tabs["cos_n"], tabs["sin_n"], p["grp_avg"], p["w_gate"], p["b_gate"])


def _attn_body(qt_ref, k_ref, vt_ref, o_ref, qcat_ref, acc_ref, *bufs, n_chunks):
    kvh = pl.program_id(1)
    tq = qt_ref.shape[2]
    cols = [slice(g * tq, (g + 1) * tq) for g in range(KV_GROUP)]

    qblk = qt_ref[0]
    top = jnp.concatenate([qblk[g * HEAD_DIM:(g + 1) * HEAD_DIM] for g in range(KV_GROUP)], axis=1)
    zero = jnp.zeros_like(top)
    qcat_ref[...] = jnp.where(kvh == 0, jnp.concatenate([top, zero], axis=0), jnp.concatenate([zero, top], axis=0))
    acc_ref[...] = jnp.zeros(acc_ref.shape, F32)

    def scores(c, dst_ref):
        kc = k_ref[0, pl.ds(pl.multiple_of(c * TM, TM), TM), :]
        maxes = []
        for sl in cols:
            s = _bdot(kc, qcat_ref[:, sl])
            dst_ref[:, sl] = s
            maxes.append(jnp.max(s, axis=0, keepdims=True))
        return jnp.concatenate(maxes, axis=1)

    def absorb(c, src_ref, chunk_max, m_old):
        m_new = jnp.maximum(m_old, chunk_max)
        alpha = jnp.exp2(m_old - m_new)
        vt = vt_ref[0, 0, c]
        acc_ref[...] = alpha * acc_ref[...]
        for sl in cols:
            p = jnp.exp2(src_ref[:, sl] - m_new[:, sl]).astype(BF16)
            acc_ref[:, sl] += _bdot(vt, p)
        return m_new

    n_buf = len(bufs)

    def group(first, carry, last_group):
        m, pending = carry[0], list(carry[1:])
        for j in range(n_buf):
            c = first + j
            if not (last_group and j + SCORE_LEAD >= n_buf):
                pending.append(scores(c + SCORE_LEAD, bufs[(j + SCORE_LEAD) % n_buf]))
            m = absorb(c, bufs[j], pending.pop(0), m)
        return (m, *pending)

    carry = (jnp.full((1, KV_GROUP * tq), NEG_BIG, F32), *[scores(c, bufs[c]) for c in range(SCORE_LEAD)])
    carry = lax.fori_loop(0, n_chunks // n_buf - 1, lambda i, cr: group(i * n_buf, cr, False), carry)
    group(n_chunks - n_buf, carry, True)

    acc = acc_ref[...]
    out_t = acc[:HEAD_DIM] / acc[HEAD_DIM:HEAD_DIM + 1]
    heads = [out_t[:, sl].T for sl in cols]
    o_ref[0] = jnp.concatenate(heads, axis=1).astype(BF16)


def _attn(qt, k, vt):
    b, _, s = qt.shape
    n_chunks = s // TM
    n_buf = 4 if n_chunks % 4 == 0 and n_chunks >= 16 else 2
    assert n_chunks % n_buf == 0
    tq = min(TQ, s)
    mq = KV_GROUP * tq
    return pl.pallas_call(
        functools.partial(_attn_body, n_chunks=n_chunks),
        grid=(b, N_KV_HEADS, s // tq),
        in_specs=[
            pl.BlockSpec((1, KV_GROUP * HEAD_DIM, tq), lambda i, h, j: (i, h, j)),
            pl.BlockSpec((1, s, KV_W), lambda i, h, j: (i, 0, 0)),
            pl.BlockSpec((1, 1, n_chunks, V_ROWS, TM), lambda i, h, j: (i, h, 0, 0, 0)),
        ],
        out_specs=pl.BlockSpec((1, tq, KV_GROUP * HEAD_DIM), lambda i, h, j: (i, j, h)),
        out_shape=jax.ShapeDtypeStruct((b, s, ATT_WIDTH), BF16),
        scratch_shapes=[
            pltpu.VMEM((KV_W, mq), BF16),
            pltpu.VMEM((V_ROWS, mq), F32),
        ] + [pltpu.VMEM((TM, mq + LANES), F32)] * n_buf,
        compiler_params=_cparams(("parallel", "parallel", "arbitrary")),
        name="attn",
    )(qt, k, vt)


def _gla_direction(q_ref, k_ref, v_ref, la_ref, o_ref, state_ref, order, cum, keep, ref_idx, last_idx):
    heads = range(GLA_HEADS)
    psl = [slice(p * LANES, (p + 1) * LANES) for p in range(GLA_HEADS * GLA_DK // LANES)]
    vsl =[slice(h * GLA_DV, (h + 1) * GLA_DV) for h in heads]
    rows = [slice(ci * CHUNK, (ci + 1) * CHUNK) for ci in order]
    n = range(len(rows))

    b = []
    for sl in rows:
        la = la_ref[0, sl, :]
        la_hi = la.astype(BF16)
        la_lo = (la - la_hi.astype(F32)).astype(BF16)
        b.append(_bdot(cum, la_hi) + _bdot(cum, la_lo))

    qr, kr, kd, qb, dec, v = [], [], [], [], [], []
    for sl, bc in zip(rows, b):
        b_ref = bc[ref_idx:ref_idx + 1]
        b_last = bc[last_idx:last_idx + 1]
        qs = q_ref[0, sl, :] * (GLA_DK ** -0.5)
        k = k_ref[0, sl, :]
        qr.append((qs * jnp.exp(bc - b_ref)).astype(BF16))
        kr.append((k * jnp.exp(b_ref - bc)).astype(BF16))
        kd.append((k * jnp.exp(b_last - bc)).astype(BF16))
        qb.append((qs * jnp.exp(bc)).astype(BF16))
        dec.append(jnp.exp(b_last))
        v.append(v_ref[0, sl, :])

    lane = lax.broadcasted_iota(jnp.int32, (1, LANES), 1)
    own = [lane < GLA_DK, lane >= GLA_DK]
    keep2 = jnp.concatenate([keep, keep], axis=0)
    a = [[None] * GLA_HEADS for _ in n]
    for i in n:
        for p, sl in enumerate(psl):
            qr_p = qr[i][:, sl]
            stacked = jnp.concatenate([jnp.where(own[0], qr_p, 0), jnp.where(own[1], qr_p, 0)], axis=0)
            a2 = jnp.where(keep2, _nt_dot(stacked, kr[i][:, sl]), 0.0).astype(BF16)
            a[i][2 * p] = a2[:CHUNK]
            a[i][2 * p + 1] = a2[CHUNK:]
    v_t = [[v[i][:, vsl[h]].astype(F32).T.astype(BF16) for h in heads] for i in n]
    o_intra = [[_bdot(a[i][h], v[i][:, vsl[h]]) for h in heads] for i in n]
    d_state = [[_bdot(v_t[i][h], jnp.where(own[h % 2], kd[i][:, psl[h // 2]], 0)) for h in heads] for i in n]

    state = [state_ref[h] for h in heads]
    entering = []
    for i in n:
        entering.append(state)
        state = [dec[i][:, psl[h // 2]] * state[h] + d_state[i][h] for h in heads]
    for h in heads:
        state_ref[h] = state[h]

    o_inter = [[_nt_dot(qb[i][:, psl[h // 2]], entering[i][h].astype(BF16)) for h in heads] for i in n]
    for i in n:
        o_ref[0, rows[i], :] = jnp.concatenate([o_intra[i][h] + o_inter[i][h] for h in heads], axis=1)


def _gla_body(qf_ref, kf_ref, vf_ref, laf_ref, qb_ref, kb_ref, vb_ref, lab_ref, of_ref, ob_ref, sf_ref, sb_ref,
              *, n_chunks):
    @pl.when(pl.program_id(1) == 0)
    def _():
        sf_ref[...] = jnp.zeros(sf_ref.shape, F32)
        sb_ref[...] = jnp.zeros(sb_ref.shape, F32)

    r = lax.broadcasted_iota(jnp.int32, (CHUNK, CHUNK), 0)
    c = lax.broadcasted_iota(jnp.int32, (CHUNK, CHUNK), 1)
    _gla_direction(qf_ref, kf_ref, vf_ref, laf_ref, of_ref, sf_ref, range(n_chunks),
                   jnp.where(c <= r, 1.0, 0.0).astype(BF16), c <= r, CHUNK // 2, CHUNK - 1)
    _gla_direction(qb_ref, kb_ref, vb_ref, lab_ref, ob_ref, sb_ref, range(n_chunks - 1, -1, -1),
                   jnp.where(c >= r, 1.0, 0.0).astype(BF16), c > r, CHUNK // 2 - 1, 0)


def _gla(ql, kl, vl, laf, lab):
    b, s, _ = ql.shape
    tb = min(GLA_BLOCK, s)
    nb = s // tb
    fwd = lambda w: pl.BlockSpec((1, tb, w), lambda i, j: (i, j, 0))
    bwd = lambda w: pl.BlockSpec((1, tb, w), lambda i, j: (i, nb - 1 - j, 0))
    return pl.pallas_call(
        functools.partial(_gla_body, n_chunks=tb // CHUNK),
        grid=(b, nb),
        in_specs=[fwd(GLA_KEY_WIDTH), fwd(GLA_KEY_WIDTH), fwd(GLA_WIDTH), fwd(GLA_KEY_WIDTH),
                  bwd(GLA_KEY_WIDTH), bwd(GLA_KEY_WIDTH), bwd(GLA_WIDTH), bwd(GLA_KEY_WIDTH)],
        out_specs=(fwd(GLA_WIDTH), bwd(GLA_WIDTH)),
        out_shape=(jax.ShapeDtypeStruct((b, s, GLA_WIDTH), F32), jax.ShapeDtypeStruct((b, s, GLA_WIDTH), F32)),
        scratch_shapes=[pltpu.VMEM((GLA_HEADS, GLA_DV, LANES), F32), pltpu.VMEM((GLA_HEADS, GLA_DV, LANES), F32)],
        compiler_params=_cparams(("parallel", "arbitrary")),
        name="gla",
    )(ql, kl, vl, laf, ql, kl, vl, lab)


def _outproj_ffn_body(h_ref, oa_ref, of_ref, ob_ref, r_ref, gg_ref, wa_ref, wg_ref, g2_ref, wgu_ref, wd_ref, og_ref,
                      o_ref):
    o = of_ref[...] + ob_ref[...]
    gain = gg_ref[...]
    r = r_ref[...]
    parts = []
    for hh in range(GLA_HEADS):
        sl = slice(hh * GLA_DV, (hh + 1) * GLA_DV)
        rh = r[:, sl]
        parts.append((_rms_rows(o[:, sl], gain[:, sl]) * (rh * (1.0 / (1.0 + jnp.exp(-rh))))).astype(BF16))
    og = jnp.concatenate(parts, axis=1)
    h = h_ref[...] + _bdot(oa_ref[...], wa_ref[...]) + _bdot(og, wg_ref[...])
    o_ref[...] = _rms_rows(_swiglu_half_step(h, g2_ref[...], wgu_ref, wd_ref), og_ref[...])


def _outproj_ffn(h, o_att, o_f, o_b, r, p):
    t = h.shape[0]
    const = lambda i: (0, 0)
    tok = lambda w: pl.BlockSpec((TM, w), lambda i: (i, 0))
    return pl.pallas_call(
        _outproj_ffn_body,
        grid=(t // TM,),
        in_specs=[tok(D_MODEL), tok(ATT_WIDTH), tok(GLA_WIDTH), tok(GLA_WIDTH), tok(GLA_WIDTH),
                  pl.BlockSpec((1, GLA_WIDTH), const),
                  _resident((ATT_WIDTH, D_MODEL)),
                  _resident((GLA_WIDTH, D_MODEL)),
                  pl.BlockSpec((1, D_MODEL), const),
                  _resident((D_MODEL, 2 * D_FF)),
                  _resident((D_FF, D_MODEL)),
                  pl.BlockSpec((1, D_MODEL), const)],
        out_specs=tok(D_MODEL),
        out_shape=jax.ShapeDtypeStruct((t, D_MODEL), F32),
        compiler_params=_cparams(("parallel",)),
        name="outproj_ffn",
    )(h, o_att, o_f, o_b, r, p["gla_gain"], p["w_out_att"], p["w_out_gla"],
      p["norm_ffn2"], p["w_ffn2_gu"], p["w_ffn2_down"], p["norm_out"])


def _rope_tables(seq_len):
    rows = seq_len // GRID_W
    row = jnp.repeat(jnp.arange(rows, dtype=F32), GRID_W)
    col = jnp.tile(jnp.arange(GRID_W, dtype=F32), rows)
    inv_freq = 1.0 / (ROPE_THETA ** (jnp.arange(0, AXIS_DIM, 2, dtype=F32) / AXIS_DIM))
    ang_r = row[:, None] * inv_freq[None, :]
    ang_c = col[:, None] * inv_freq[None, :]
    ang = jnp.concatenate([ang_r, ang_r, ang_c, ang_c], axis=-1)
    sign = jnp.where((jnp.arange(HEAD_DIM) % 32) < 16, -1.0, 1.0).astype(F32)
    cos, sin = jnp.cos(ang), jnp.sin(ang) * sign[None, :]
    return {
        "cos_t": cos.T, "sin_t": sin.T,
        "cos_n": jnp.tile(cos, (1, N_KV_HEADS)), "sin_n": jnp.tile(sin, (1, N_KV_HEADS)),
    }


def _prep_layer(l, norm_ffn1, w_ffn1_gu, w_ffn1_down, norm_mix, w_in, q_norm, k_norm, w_gate_f, b_gate_f,
                w_gate_b, b_gate_b, gla_norm, w_out, norm_ffn2, w_ffn2_gu, w_ffn2_down, norm_out):
    w = w_in[l]
    o_q, o_k, o_v = 0, ATT_WIDTH, ATT_WIDTH + KV_W
    o_rest = ATT_WIDTH + 2 * KV_W
    n_rest = 2 * GLA_KEY_WIDTH + 2 * GLA_WIDTH
    gates = jnp.pad(w[:, o_rest + n_rest:], ((0, 0), (0, LANES - 2 * GATE_RANK)))
    w_rest = jnp.concatenate([w[:, o_k:o_v], w[:, o_rest:o_rest + n_rest], gates], axis=1).astype(BF16)
    w_qv_t = jnp.concatenate([w[:, o_q:o_k], w[:, o_v:o_rest]], axis=1).T.astype(BF16)
    w_gate = jnp.zeros((LANES, 2 * GLA_KEY_WIDTH), F32)
    w_gate = w_gate.at[:GATE_RANK, :GLA_KEY_WIDTH].set(w_gate_f[l])
    w_gate = w_gate.at[GATE_RANK:2 * GATE_RANK, GLA_KEY_WIDTH:].set(w_gate_b[l])
    hid = jnp.arange(KV_W) // HEAD_DIM
    grp_avg = jnp.where(hid[:, None] == hid[None, :], 1.0 / HEAD_DIM, 0.0).astype(BF16)
    row = lambda v: v.reshape(1, -1).astype(F32)
    return {
        "norm_ffn1": row(norm_ffn1[l]), "w_ffn1_gu": w_ffn1_gu[l].astype(BF16), "w_ffn1_down": w_ffn1_down[l].astype(BF16),
        "norm_mix": row(norm_mix[l]), "w_qv_t": w_qv_t, "w_rest": w_rest,
        "q_gain_col": q_norm[l].reshape(HEAD_DIM, 1).astype(F32),
        "k_gain_row": row(jnp.tile(k_norm[l], N_KV_HEADS)),
        "grp_avg": grp_avg, "w_gate": w_gate.astype(BF16),
        "b_gate": row(jnp.concatenate([b_gate_f[l], b_gate_b[l]])),
        "gla_gain": row(jnp.tile(gla_norm[l], GLA_HEADS)),
        "w_out_att": w_out[l][:ATT_WIDTH].astype(BF16), "w_out_gla": w_out[l][ATT_WIDTH:].astype(BF16),
        "norm_ffn2": row(norm_ffn2[l]), "w_ffn2_gu": w_ffn2_gu[l].astype(BF16), "w_ffn2_down": w_ffn2_down[l].astype(BF16),
        "norm_out": row(norm_out[l]),
    }


def _layer(x, p, tabs):
    b, s, d = x.shape
    t = b * s
    h, qt, k, vt, ql, kl, vl, rl, laf, lab = _ffn_inproj(x, p, tabs)
    o_att = _attn(qt, k, vt)
    o_f, o_b = _gla(ql, kl, vl, laf, lab)
    y = _outproj_ffn(h.reshape(t, d), o_att.reshape(t, ATT_WIDTH), o_f.reshape(t, GLA_WIDTH),
                     o_b.reshape(t, GLA_WIDTH), rl.reshape(t, GLA_WIDTH), p)
    return y.reshape(b, s, d)


def _trunk(x, layers):
    tabs = _rope_tables(x.shape[1])
    h = x
    for p in layers:
        h = _layer(h, p, tabs)
    return h


def kernel(x_prompt, x_sample, norm_ffn1, w_ffn1_gu, w_ffn1_down, norm_mix, w_in, q_norm, k_norm, w_gate_f, b_gate_f,
           w_gate_b, b_gate_b, gla_norm, w_out, norm_ffn2, w_ffn2_gu, w_ffn2_down, norm_out):
    params = (norm_ffn1, w_ffn1_gu, w_ffn1_down, norm_mix, w_in, q_norm, k_norm, w_gate_f, b_gate_f,
              w_gate_b, b_gate_b, gla_norm, w_out, norm_ffn2, w_ffn2_gu, w_ffn2_down, norm_out)
    layers = [_prep_layer(l, *params) for l in range(norm_ffn1.shape[0])]
    return (_trunk(x_prompt, layers), _trunk(x_sample, layers))
```

```python
import functools

import jax
import jax.numpy as jnp
from jax import lax
from jax.experimental import pallas as pl
from jax.experimental.pallas import tpu as pltpu

F32 = jnp.float32
BF16 = jnp.bfloat16

D_MODEL = 1024
GRID_W = 64
ATT_WIDTH = 512
HEAD_DIM = 64
N_HEADS = 8
N_KV_HEADS = 2
KV_GROUP = 4
AXIS_DIM = 32
ROPE_THETA = 10000.0
GLA_WIDTH = 512
GLA_HEADS = 4
GLA_DV = 128
GLA_DK = 64
GLA_KEY_WIDTH = 256
GATE_RANK = 16
GATE_TAU = 16.0
CHUNK = 64
D_FF = 2816
EPS = 1e-6

LANES = 128
KV_W = N_KV_HEADS * HEAD_DIM
QV_ROWS = ATT_WIDTH + KV_W
REST_W = KV_W + 2 * GLA_KEY_WIDTH + 2 * GLA_WIDTH + LANES
GATE_OFF = REST_W - LANES

TM = 512
TQ = 256
SCORE_LEAD = 1
FF_CHUNK = 256
N_FF_CHUNKS = D_FF // FF_CHUNK
GLA_BLOCK = 512
NEG_BIG = -1e30
LOG2E = 1.4426950408889634
V_ROWS = 2 * HEAD_DIM
VMEM_LIMIT = 56 * 1024 * 1024


def _cparams(sem):
    return pltpu.CompilerParams(dimension_semantics=sem, vmem_limit_bytes=VMEM_LIMIT)


def _rms_rows(x, gain_row):
    ms = jnp.mean(x * x, axis=-1, keepdims=True)
    return x * lax.rsqrt(ms + EPS) * gain_row


def _bdot(a, b):
    return jnp.dot(a, b, preferred_element_type=F32)


def _nt_dot(a, b):
    return lax.dot_general(a, b, (((1,), (1,)), ((), ())), preferred_element_type=F32)


def _swiglu_half_step(x, gain_row, wgu_ref, wd_ref):
    xn = _rms_rows(x, gain_row).astype(BF16)
    acc = None
    for c in range(N_FF_CHUNKS):
        g = _bdot(xn, wgu_ref[:, c * FF_CHUNK:(c + 1) * FF_CHUNK])
        u = _bdot(xn, wgu_ref[:, D_FF + c * FF_CHUNK:D_FF + (c + 1) * FF_CHUNK])
        a = (g * (1.0 / (1.0 + jnp.exp(-g))) * u).astype(BF16)
        d = _bdot(a, wd_ref[c * FF_CHUNK:(c + 1) * FF_CHUNK, :])
        acc = d if acc is None else acc + d
    return x + 0.5 * acc


def _resident(shape):
    return pl.BlockSpec(shape, lambda *_: (0,) * len(shape), pipeline_mode=pl.Buffered(1))


def _swap16(t, axis):
    n = t.shape[axis] // 16
    parts = [lax.slice_in_dim(t, (i ^ 1) * 16, (i ^ 1) * 16 + 16, axis=axis) for i in range(n)]
    return jnp.concatenate(parts, axis=axis)


def _ffn_inproj_body(x_ref, g1_ref, wgu_ref, wd_ref, g_ref, wqv_ref, wrest_ref, qg_ref, kg_ref,
                     cost_ref, sint_ref, cosn_ref, sinn_ref, grp_ref, wgate_ref, bgate_ref,
                     h_ref, qt_ref, k_ref, vt_ref, ql_ref, kl_ref, vl_ref, rl_ref, laf_ref, lab_ref):
    h = _swiglu_half_step(x_ref[0], g1_ref[...], wgu_ref, wd_ref)
    h_ref[0] = h
    xn = _rms_rows(h, g_ref[...]).astype(BF16)
    tm = xn.shape[0]

    ut = _nt_dot(wqv_ref[...], xn)
    q = ut[:ATT_WIDTH].reshape(N_HEADS, HEAD_DIM, tm)
    q = q * lax.rsqrt(jnp.mean(q * q, axis=1, keepdims=True) + EPS) * qg_ref[...][None]
    q = q * cost_ref[...][None] + _swap16(q, 1) * sint_ref[...][None]
    qt_ref[0] = (q * (HEAD_DIM ** -0.5 * LOG2E)).reshape(ATT_WIDTH, tm).astype(BF16)

    row = lax.broadcasted_iota(jnp.int32, (V_ROWS - HEAD_DIM, tm), 0)
    ones_rows = jnp.where(row == 0, 1.0, 0.0).astype(BF16)
    for kvh in range(N_KV_HEADS):
        v = ut[ATT_WIDTH + kvh * HEAD_DIM:ATT_WIDTH + (kvh + 1) * HEAD_DIM].astype(BF16)
        vt_ref[0, kvh, 0] = jnp.concatenate([v, ones_rows], axis=0)

    rest = _bdot(xn, wrest_ref[...])

    k = rest[:, :KV_W]
    ss = k * k
    ss_hi = ss.astype(BF16)
    ss_lo = (ss - ss_hi.astype(F32)).astype(BF16)
    ms = _bdot(ss_hi, grp_ref[...]) + _bdot(ss_lo, grp_ref[...])
    k = k * lax.rsqrt(ms + EPS) * kg_ref[...]
    lane = lax.broadcasted_iota(jnp.int32, k.shape, 1)
    k_sw = jnp.where((lane % 32) < 16, pltpu.roll(k, LANES - 16, 1), pltpu.roll(k, 16, 1))
    k_ref[0] = (k * cosn_ref[...] + k_sw * sinn_ref[...]).astype(BF16)

    o = KV_W
    ql_ref[0] = rest[:, o:o + GLA_KEY_WIDTH]
    o += GLA_KEY_WIDTH
    kl_ref[0] = rest[:, o:o + GLA_KEY_WIDTH]
    o += GLA_KEY_WIDTH
    vl_ref[0] = rest[:, o:o + GLA_WIDTH].astype(BF16)
    o += GLA_WIDTH
    rl_ref[0] = rest[:, o:o + GLA_WIDTH]

    pre = _bdot(rest[:, GATE_OFF:].astype(BF16), wgate_ref[...]) + bgate_ref[...]
    logsig = jnp.minimum(pre, 0.0) - jnp.log(1.0 + jnp.exp(-jnp.abs(pre)))
    la = logsig * (1.0 / GATE_TAU)
    laf_ref[0] = la[:, :GLA_KEY_WIDTH]
    lab_ref[0] = la[:, GLA_KEY_WIDTH:]


def _ffn_inproj(x, p, tabs):
    b, s, _ = x.shape
    nt = s // TM
    c2 = lambda i, j: (0, 0)
    tok = lambda w: pl.BlockSpec((1, TM, w), lambda i, j: (i, j, 0))
    out_shape = (
        jax.ShapeDtypeStruct((b, s, D_MODEL), F32),
        jax.ShapeDtypeStruct((b, ATT_WIDTH, s), BF16),
        jax.ShapeDtypeStruct((b, s, KV_W), BF16),
        jax.ShapeDtypeStruct((b, N_KV_HEADS, nt, V_ROWS, TM), BF16),
        jax.ShapeDtypeStruct((b, s, GLA_KEY_WIDTH), F32),
        jax.ShapeDtypeStruct((b, s, GLA_KEY_WIDTH), F32),
        jax.ShapeDtypeStruct((b, s, GLA_WIDTH), BF16),
        jax.ShapeDtypeStruct((b, s, GLA_WIDTH), F32),
        jax.ShapeDtypeStruct((b, s, GLA_KEY_WIDTH), F32),
        jax.ShapeDtypeStruct((b, s, GLA_KEY_WIDTH), F32),
    )
    out_specs = (
        tok(D_MODEL),
        pl.BlockSpec((1, ATT_WIDTH, TM), lambda i, j: (i, 0, j)),
        tok(KV_W),
        pl.BlockSpec((1, N_KV_HEADS, 1, V_ROWS, TM), lambda i, j: (i, 0, j, 0, 0)),
        tok(GLA_KEY_WIDTH), tok(GLA_KEY_WIDTH), tok(GLA_WIDTH), tok(GLA_WIDTH), tok(GLA_KEY_WIDTH),
        tok(GLA_KEY_WIDTH),
    )
    in_specs = [
        tok(D_MODEL),
        pl.BlockSpec((1, D_MODEL), c2),
        _resident((D_MODEL, 2 * D_FF)),
        _resident((D_FF, D_MODEL)),
        pl.BlockSpec((1, D_MODEL), c2),
        _resident((QV_ROWS, D_MODEL)),
        _resident((D_MODEL, REST_W)),
        pl.BlockSpec((HEAD_DIM, 1), c2),
        pl.BlockSpec((1, KV_W), c2),
        pl.BlockSpec((HEAD_DIM, TM), lambda i, j: (0, j)),
        pl.BlockSpec((HEAD_DIM, TM), lambda i, j: (0, j)),
        pl.BlockSpec((TM, KV_W), lambda i, j: (j, 0)),
        pl.BlockSpec((TM, KV_W), lambda i, j: (j, 0)),
        pl.BlockSpec((KV_W, KV_W), c2),
        pl.BlockSpec((LANES, 2 * GLA_KEY_WIDTH), c2),
        pl.BlockSpec((1, 2 * GLA_KEY_WIDTH), c2),
    ]
    return pl.pallas_call(
        _ffn_inproj_body,
        grid=(b, nt),
        in_specs=in_specs,
        out_specs=out_specs,
        out_shape=out_shape,
        compiler_params=_cparams(("parallel", "parallel")),
        name="ffn_inproj",
    )(x, p["norm_ffn1"], p["w_ffn1_gu"], p["w_ffn1_down"],
      p["norm_mix"], p["w_qv_t"], p["w_rest"], p["q_gain_col"], p["k_gain_row"],
      tabs["cos_t"], tabs["sin_t"], tabs["cos_n"], tabs["sin_n"], p["grp_avg"], p["w_gate"], p["b_gate"])


def _attn_body(qt_ref, k_ref, vt_ref, o_ref, qcat_ref, acc_ref, *bufs, n_chunks):
    kvh = pl.program_id(1)
    tq = qt_ref.shape[2]
    cols = [slice(g * tq, (g + 1) * tq) for g in range(KV_GROUP)]

    qblk = qt_ref[0]
    top = jnp.concatenate([qblk[g * HEAD_DIM:(g + 1) * HEAD_DIM] for g in range(KV_GROUP)], axis=1)
    zero = jnp.zeros_like(top)
    qcat_ref[...] = jnp.where(kvh == 0, jnp.concatenate([top, zero], axis=0), jnp.concatenate([zero, top], axis=0))
    acc_ref[...] = jnp.zeros(acc_ref.shape, F32)

    def scores(c, dst_ref):
        kc = k_ref[0, pl.ds(pl.multiple_of(c * TM, TM), TM), :]
        maxes = []
        for sl in cols:
            s = _bdot(kc, qcat_ref[:, sl])
            dst_ref[:, sl] = s
            maxes.append(jnp.max(s, axis=0, keepdims=True))
        return jnp.concatenate(maxes, axis=1)

    def absorb(c, src_ref, chunk_max, m_old):
        m_new = jnp.maximum(m_old, chunk_max)
        alpha = jnp.exp2(m_old - m_new)
        vt = vt_ref[0, 0, c]
        acc_ref[...] = alpha * acc_ref[...]
        for sl in cols:
            p = jnp.exp2(src_ref[:, sl] - m_new[:, sl]).astype(BF16)
            acc_ref[:, sl] += _bdot(vt, p)
        return m_new

    n_buf = len(bufs)

    def group(first, carry, last_group):
        m, pending = carry[0], list(carry[1:])
        for j in range(n_buf):
            c = first + j
            if not (last_group and j + SCORE_LEAD >= n_buf):
                pending.append(scores(c + SCORE_LEAD, bufs[(j + SCORE_LEAD) % n_buf]))
            m = absorb(c, bufs[j], pending.pop(0), m)
        return (m, *pending)

    carry = (jnp.full((1, KV_GROUP * tq), NEG_BIG, F32), *[scores(c, bufs[c]) for c in range(SCORE_LEAD)])
    carry = lax.fori_loop(0, n_chunks // n_buf - 1, lambda i, cr: group(i * n_buf, cr, False), carry)
    group(n_chunks - n_buf, carry, True)

    acc = acc_ref[...]
    out_t = acc[:HEAD_DIM] / acc[HEAD_DIM:HEAD_DIM + 1]
    heads = [out_t[:, sl].T for sl in cols]
    o_ref[0] = jnp.concatenate(heads, axis=1).astype(BF16)


def _attn(qt, k, vt):
    b, _, s = qt.shape
    n_chunks = s // TM
    n_buf = 4 if n_chunks % 4 == 0 and n_chunks >= 16 else 2
    assert n_chunks % n_buf == 0
    tq = min(TQ, s)
    mq = KV_GROUP * tq
    return pl.pallas_call(
        functools.partial(_attn_body, n_chunks=n_chunks),
        grid=(b, N_KV_HEADS, s // tq),
        in_specs=[
            pl.BlockSpec((1, KV_GROUP * HEAD_DIM, tq), lambda i, h, j: (i, h, j)),
            pl.BlockSpec((1, s, KV_W), lambda i, h, j: (i, 0, 0)),
            pl.BlockSpec((1, 1, n_chunks, V_ROWS, TM), lambda i, h, j: (i, h, 0, 0, 0)),
        ],
        out_specs=pl.BlockSpec((1, tq, KV_GROUP * HEAD_DIM), lambda i, h, j: (i, j, h)),
        out_shape=jax.ShapeDtypeStruct((b, s, ATT_WIDTH), BF16),
        scratch_shapes=[
            pltpu.VMEM((KV_W, mq), BF16),
            pltpu.VMEM((V_ROWS, mq), F32),
        ] + [pltpu.VMEM((TM, mq + LANES), F32)] * n_buf,
        compiler_params=_cparams(("parallel", "parallel", "arbitrary")),
        name="attn",
    )(qt, k, vt)


def _gla_direction(q_ref, k_ref, v_ref, la_ref, o_ref, state_ref, order, cum, keep, ref_idx, last_idx):
    heads = range(GLA_HEADS)
    psl = [slice(p * LANES, (p + 1) * LANES) for p in range(GLA_HEADS * GLA_DK // LANES)]
    vsl =[slice(h * GLA_DV, (h + 1) * GLA_DV) for h in heads]
    rows = [slice(ci * CHUNK, (ci + 1) * CHUNK) for ci in order]
    n = range(len(rows))

    b = []
    for sl in rows:
        la = la_ref[0, sl, :]
        la_hi = la.astype(BF16)
        la_lo = (la - la_hi.astype(F32)).astype(BF16)
        b.append(_bdot(cum, la_hi) + _bdot(cum, la_lo))

    qr, kr, kd, qb, dec, v = [], [], [], [], [], []
    for sl, bc in zip(rows, b):
        b_ref = bc[ref_idx:ref_idx + 1]
        b_last = bc[last_idx:last_idx + 1]
        qs = q_ref[0, sl, :] * (GLA_DK ** -0.5)
        k = k_ref[0, sl, :]
        qr.append((qs * jnp.exp(bc - b_ref)).astype(BF16))
        kr.append((k * jnp.exp(b_ref - bc)).astype(BF16))
        kd.append((k * jnp.exp(b_last - bc)).astype(BF16))
        qb.append((qs * jnp.exp(bc)).astype(BF16))
        dec.append(jnp.exp(b_last))
        v.append(v_ref[0, sl, :])

    lane = lax.broadcasted_iota(jnp.int32, (1, LANES), 1)
    own = [lane < GLA_DK, lane >= GLA_DK]
    keep2 = jnp.concatenate([keep, keep], axis=0)
    a = [[None] * GLA_HEADS for _ in n]
    for i in n:
        for p, sl in enumerate(psl):
            qr_p = qr[i][:, sl]
            stacked = jnp.concatenate([jnp.where(own[0], qr_p, 0), jnp.where(own[1], qr_p, 0)], axis=0)
            a2 = jnp.where(keep2, _nt_dot(stacked, kr[i][:, sl]), 0.0).astype(BF16)
            a[i][2 * p] = a2[:CHUNK]
            a[i][2 * p + 1] = a2[CHUNK:]
    v_t = [[v[i][:, vsl[h]].astype(F32).T.astype(BF16) for h in heads] for i in n]
    o_intra = [[_bdot(a[i][h], v[i][:, vsl[h]]) for h in heads] for i in n]
    d_state = [[_bdot(v_t[i][h], jnp.where(own[h % 2], kd[i][:, psl[h // 2]], 0)) for h in heads] for i in n]

    state = [state_ref[h] for h in heads]
    entering = []
    for i in n:
        entering.append(state)
        state = [dec[i][:, psl[h // 2]] * state[h] + d_state[i][h] for h in heads]
    for h in heads:
        state_ref[h] = state[h]

    o_inter = [[_nt_dot(qb[i][:, psl[h // 2]], entering[i][h].astype(BF16)) for h in heads] for i in n]
    for i in n:
        o_ref[0, rows[i], :] = jnp.concatenate([o_intra[i][h] + o_inter[i][h] for h in heads], axis=1)


def _gla_body(qf_ref, kf_ref, vf_ref, laf_ref, qb_ref, kb_ref, vb_ref, lab_ref, of_ref, ob_ref, sf_ref, sb_ref,
              *, n_chunks):
    @pl.when(pl.program_id(1) == 0)
    def _():
        sf_ref[...] = jnp.zeros(sf_ref.shape, F32)
        sb_ref[...] = jnp.zeros(sb_ref.shape, F32)

    r = lax.broadcasted_iota(jnp.int32, (CHUNK, CHUNK), 0)
    c = lax.broadcasted_iota(jnp.int32, (CHUNK, CHUNK), 1)
    _gla_direction(qf_ref, kf_ref, vf_ref, laf_ref, of_ref, sf_ref, range(n_chunks),
                   jnp.where(c <= r, 1.0, 0.0).astype(BF16), c <= r, CHUNK // 2, CHUNK - 1)
    _gla_direction(qb_ref, kb_ref, vb_ref, lab_ref, ob_ref, sb_ref, range(n_chunks - 1, -1, -1),
                   jnp.where(c >= r, 1.0, 0.0).astype(BF16), c > r, CHUNK // 2 - 1, 0)


def _gla(ql, kl, vl, laf, lab):
    b, s, _ = ql.shape
    tb = min(GLA_BLOCK, s)
    nb = s // tb
    fwd = lambda w: pl.BlockSpec((1, tb, w), lambda i, j: (i, j, 0))
    bwd = lambda w: pl.BlockSpec((1, tb, w), lambda i, j: (i, nb - 1 - j, 0))
    return pl.pallas_call(
        functools.partial(_gla_body, n_chunks=tb // CHUNK),
        grid=(b, nb),
        in_specs=[fwd(GLA_KEY_WIDTH), fwd(GLA_KEY_WIDTH), fwd(GLA_WIDTH), fwd(GLA_KEY_WIDTH),
                  bwd(GLA_KEY_WIDTH), bwd(GLA_KEY_WIDTH), bwd(GLA_WIDTH), bwd(GLA_KEY_WIDTH)],
        out_specs=(fwd(GLA_WIDTH), bwd(GLA_WIDTH)),
        out_shape=(jax.ShapeDtypeStruct((b, s, GLA_WIDTH), F32), jax.ShapeDtypeStruct((b, s, GLA_WIDTH), F32)),
        scratch_shapes=[pltpu.VMEM((GLA_HEADS, GLA_DV, LANES), F32), pltpu.VMEM((GLA_HEADS, GLA_DV, LANES), F32)],
        compiler_params=_cparams(("parallel", "arbitrary")),
        name="gla",
    )(ql, kl, vl, laf, ql, kl, vl, lab)


def _outproj_ffn_body(h_ref, oa_ref, of_ref, ob_ref, r_ref, gg_ref, wa_ref, wg_ref, g2_ref, wgu_ref, wd_ref, og_ref,
                      o_ref):
    o = of_ref[...] + ob_ref[...]
    gain = gg_ref[...]
    r = r_ref[...]
    parts = []
    for hh in range(GLA_HEADS):
        sl = slice(hh * GLA_DV, (hh + 1) * GLA_DV)
        rh = r[:, sl]
        parts.append((_rms_rows(o[:, sl], gain[:, sl]) * (rh * (1.0 / (1.0 + jnp.exp(-rh))))).astype(BF16))
    og = jnp.concatenate(parts, axis=1)
    h = h_ref[...] + _bdot(oa_ref[...], wa_ref[...]) + _bdot(og, wg_ref[...])
    o_ref[...] = _rms_rows(_swiglu_half_step(h, g2_ref[...], wgu_ref, wd_ref), og_ref[...])


def _outproj_ffn(h, o_att, o_f, o_b, r, p):
    t = h.shape[0]
    const = lambda i: (0, 0)
    tok = lambda w: pl.BlockSpec((TM, w), lambda i: (i, 0))
    return pl.pallas_call(
        _outproj_ffn_body,
        grid=(t // TM,),
        in_specs=[tok(D_MODEL), tok(ATT_WIDTH), tok(GLA_WIDTH), tok(GLA_WIDTH), tok(GLA_WIDTH),
                  pl.BlockSpec((1, GLA_WIDTH), const),
                  _resident((ATT_WIDTH, D_MODEL)),
                  _resident((GLA_WIDTH, D_MODEL)),
                  pl.BlockSpec((1, D_MODEL), const),
                  _resident((D_MODEL, 2 * D_FF)),
                  _resident((D_FF, D_MODEL)),
                  pl.BlockSpec((1, D_MODEL), const)],
        out_specs=tok(D_MODEL),
        out_shape=jax.ShapeDtypeStruct((t, D_MODEL), F32),
        compiler_params=_cparams(("parallel",)),
        name="outproj_ffn",
    )(h, o_att, o_f, o_b, r, p["gla_gain"], p["w_out_att"], p["w_out_gla"],
      p["norm_ffn2"], p["w_ffn2_gu"], p["w_ffn2_down"], p["norm_out"])


def _rope_tables(seq_len):
    rows = seq_len // GRID_W
    row = jnp.repeat(jnp.arange(rows, dtype=F32), GRID_W)
    col = jnp.tile(jnp.arange(GRID_W, dtype=F32), rows)
    inv_freq = 1.0 / (ROPE_THETA ** (jnp.arange(0, AXIS_DIM, 2, dtype=F32) / AXIS_DIM))
    ang_r = row[:, None] * inv_freq[None, :]
    ang_c = col[:, None] * inv_freq[None, :]
    ang = jnp.concatenate([ang_r, ang_r, ang_c, ang_c], axis=-1)
    sign = jnp.where((jnp.arange(HEAD_DIM) % 32) < 16, -1.0, 1.0).astype(F32)
    cos, sin = jnp.cos(ang), jnp.sin(ang) * sign[None, :]
    return {
        "cos_t": cos.T, "sin_t": sin.T,
        "cos_n": jnp.tile(cos, (1, N_KV_HEADS)), "sin_n": jnp.tile(sin, (1, N_KV_HEADS)),
    }


def _prep_layer(l, norm_ffn1, w_ffn1_gu, w_ffn1_down, norm_mix, w_in, q_norm, k_norm, w_gate_f, b_gate_f,
                w_gate_b, b_gate_b, gla_norm, w_out, norm_ffn2, w_ffn2_gu, w_ffn2_down, norm_out):
    w = w_in[l]
    o_q, o_k, o_v = 0, ATT_WIDTH, ATT_WIDTH + KV_W
    o_rest = ATT_WIDTH + 2 * KV_W
    n_rest = 2 * GLA_KEY_WIDTH + 2 * GLA_WIDTH
    gates = jnp.pad(w[:, o_rest + n_rest:], ((0, 0), (0, LANES - 2 * GATE_RANK)))
    w_rest = jnp.concatenate([w[:, o_k:o_v], w[:, o_rest:o_rest + n_rest], gates], axis=1).astype(BF16)
    w_qv_t = jnp.concatenate([w[:, o_q:o_k], w[:, o_v:o_rest]], axis=1).T.astype(BF16)
    w_gate = jnp.zeros((LANES, 2 * GLA_KEY_WIDTH), F32)
    w_gate = w_gate.at[:GATE_RANK, :GLA_KEY_WIDTH].set(w_gate_f[l])
    w_gate = w_gate.at[GATE_RANK:2 * GATE_RANK, GLA_KEY_WIDTH:].set(w_gate_b[l])
    hid = jnp.arange(KV_W) // HEAD_DIM
    grp_avg = jnp.where(hid[:, None] == hid[None, :], 1.0 / HEAD_DIM, 0.0).astype(BF16)
    row = lambda v: v.reshape(1, -1).astype(F32)
    return {
        "norm_ffn1": row(norm_ffn1[l]), "w_ffn1_gu": w_ffn1_gu[l].astype(BF16), "w_ffn1_down": w_ffn1_down[l].astype(BF16),
        "norm_mix": row(norm_mix[l]), "w_qv_t": w_qv_t, "w_rest": w_rest,
        "q_gain_col": q_norm[l].reshape(HEAD_DIM, 1).astype(F32),
        "k_gain_row": row(jnp.tile(k_norm[l], N_KV_HEADS)),
        "grp_avg": grp_avg, "w_gate": w_gate.astype(BF16),
        "b_gate": row(jnp.concatenate([b_gate_f[l], b_gate_b[l]])),
        "gla_gain": row(jnp.tile(gla_norm[l], GLA_HEADS)),
        "w_out_att": w_out[l][:ATT_WIDTH].astype(BF16), "w_out_gla": w_out[l][ATT_WIDTH:].astype(BF16),
        "norm_ffn2": row(norm_ffn2[l]), "w_ffn2_gu": w_ffn2_gu[l].astype(BF16), "w_ffn2_down": w_ffn2_down[l].astype(BF16),
        "norm_out": row(norm_out[l]),
    }


def _layer(x, p, tabs):
    b, s, d = x.shape
    t = b * s
    h, qt, k, vt, ql, kl, vl, rl, laf, lab = _ffn_inproj(x, p, tabs)
    o_att = _attn(qt, k, vt)
    o_f, o_b = _gla(ql, kl, vl, laf, lab)
    y = _outproj_ffn(h.reshape(t, d), o_att.reshape(t, ATT_WIDTH), o_f.reshape(t, GLA_WIDTH),
                     o_b.reshape(t, GLA_WIDTH), rl.reshape(t, GLA_WIDTH), p)
    return y.reshape(b, s, d)


def _trunk(x, layers):
    tabs = _rope_tables(x.shape[1])
    h = x
    for p in layers:
        h = _layer(h, p, tabs)
    return h


def kernel(x_prompt, x_sample, norm_ffn1, w_ffn1_gu, w_ffn1_down, norm_mix, w_in, q_norm, k_norm, w_gate_f, b_gate_f,
           w_gate_b, b_gate_b, gla_norm, w_out, norm_ffn2, w_ffn2_gu, w_ffn2_down, norm_out):
    params = (norm_ffn1, w_ffn1_gu, w_ffn1_down, norm_mix, w_in, q_norm, k_norm, w_gate_f, b_gate_f,
              w_gate_b, b_gate_b, gla_norm, w_out, norm_ffn2, w_ffn2_gu, w_ffn2_down, norm_out)
    layers = [_prep_layer(l, *params) for l in range(norm_ffn1.shape[0])]
    return (_trunk(x_prompt, layers), _trunk(x_sample, layers))
```

```python
import functools

import jax
import jax.numpy as jnp
from jax import lax
from jax.experimental import pallas as pl
from jax.experimental.pallas import tpu as pltpu

F32 = jnp.float32
BF16 = jnp.bfloat16

D_MODEL = 1024
GRID_W = 64
ATT_WIDTH = 512
HEAD_DIM = 64
N_HEADS = 8
N_KV_HEADS = 2
KV_GROUP = 4
AXIS_DIM = 32
ROPE_THETA = 10000.0
GLA_WIDTH = 512
GLA_HEADS = 4
GLA_DV = 128
GLA_DK = 64
GLA_KEY_WIDTH = 256
GATE_RANK = 16
GATE_TAU = 16.0
CHUNK = 64
D_FF = 2816
EPS = 1e-6

LANES = 128
KV_W = N_KV_HEADS * HEAD_DIM
QV_ROWS = ATT_WIDTH + KV_W
REST_W = KV_W + 2 * GLA_KEY_WIDTH + 2 * GLA_WIDTH + LANES
GATE_OFF = REST_W - LANES

TM = 512
TQ = 256
SCORE_LEAD = 1
SCORE_BUFFERS = 4
FF_CHUNK = 256
N_FF_CHUNKS = D_FF // FF_CHUNK
GLA_BLOCK = 512
NEG_BIG = -1e30
LOG2E = 1.4426950408889634
V_ROWS = 2 * HEAD_DIM
VMEM_LIMIT = 56 * 1024 * 1024


def _cparams(sem):
    return pltpu.CompilerParams(dimension_semantics=sem, vmem_limit_bytes=VMEM_LIMIT)


def _rms_rows(x, gain_row):
    ms = jnp.mean(x * x, axis=-1, keepdims=True)
    return x * lax.rsqrt(ms + EPS) * gain_row


def _bdot(a, b):
    return jnp.dot(a, b, preferred_element_type=F32)


def _nt_dot(a, b):
    return lax.dot_general(a, b, (((1,), (1,)), ((), ())), preferred_element_type=F32)


def _swiglu_half_step(x, gain_row, wgu_ref, wd_ref):
    xn = _rms_rows(x, gain_row).astype(BF16)
    acc = None
    for c in range(N_FF_CHUNKS):
        g = _bdot(xn, wgu_ref[:, c * FF_CHUNK:(c + 1) * FF_CHUNK])
        u = _bdot(xn, wgu_ref[:, D_FF + c * FF_CHUNK:D_FF + (c + 1) * FF_CHUNK])
        a = (g * (1.0 / (1.0 + jnp.exp(-g))) * u).astype(BF16)
        d = _bdot(a, wd_ref[c * FF_CHUNK:(c + 1) * FF_CHUNK, :])
        acc = d if acc is None else acc + d
    return x + 0.5 * acc


def _resident(shape):
    return pl.BlockSpec(shape, lambda *_: (0,) * len(shape), pipeline_mode=pl.Buffered(1))


def _swap16(t, axis):
    n = t.shape[axis] // 16
    parts = [lax.slice_in_dim(t, (i ^ 1) * 16, (i ^ 1) * 16 + 16, axis=axis) for i in range(n)]
    return jnp.concatenate(parts, axis=axis)


def _ffn_inproj_body(x_ref, g1_ref, wgu_ref, wd_ref, g_ref, wqv_ref, wrest_ref, qg_ref, kg_ref,
                     cost_ref, sint_ref, cosn_ref, sinn_ref, grp_ref, wgate_ref, bgate_ref,
                     h_ref, qt_ref, k_ref, vt_ref, ql_ref, kl_ref, vl_ref, rl_ref, laf_ref, lab_ref):
    h = _swiglu_half_step(x_ref[0], g1_ref[...], wgu_ref, wd_ref)
    h_ref[0] = h
    xn = _rms_rows(h, g_ref[...]).astype(BF16)
    tm = xn.shape[0]

    ut = _nt_dot(wqv_ref[...], xn)
    q = ut[:ATT_WIDTH].reshape(N_HEADS, HEAD_DIM, tm)
    q = q * lax.rsqrt(jnp.mean(q * q, axis=1, keepdims=True) + EPS) * qg_ref[...][None]
    q = q * cost_ref[...][None] + _swap16(q, 1) * sint_ref[...][None]
    qt_ref[0] = (q * (HEAD_DIM ** -0.5 * LOG2E)).reshape(ATT_WIDTH, tm).astype(BF16)

    row = lax.broadcasted_iota(jnp.int32, (V_ROWS - HEAD_DIM, tm), 0)
    ones_rows = jnp.where(row == 0, 1.0, 0.0).astype(BF16)
    for kvh in range(N_KV_HEADS):
        v = ut[ATT_WIDTH + kvh * HEAD_DIM:ATT_WIDTH + (kvh + 1) * HEAD_DIM].astype(BF16)
        vt_ref[0, kvh, 0] = jnp.concatenate([v, ones_rows], axis=0)

    rest = _bdot(xn, wrest_ref[...])

    k = rest[:, :KV_W]
    ss = k * k
    ss_hi = ss.astype(BF16)
    ss_lo = (ss - ss_hi.astype(F32)).astype(BF16)
    ms = _bdot(ss_hi, grp_ref[...]) + _bdot(ss_lo, grp_ref[...])
    k = k * lax.rsqrt(ms + EPS) * kg_ref[...]
    lane = lax.broadcasted_iota(jnp.int32, k.shape, 1)
    k_sw = jnp.where((lane % 32) < 16, pltpu.roll(k, LANES - 16, 1), pltpu.roll(k, 16, 1))
    k_ref[0] = (k * cosn_ref[...] + k_sw * sinn_ref[...]).astype(BF16)

    o = KV_W
    ql_ref[0] = rest[:, o:o + GLA_KEY_WIDTH]
    o += GLA_KEY_WIDTH
    kl_ref[0] = rest[:, o:o + GLA_KEY_WIDTH]
    o += GLA_KEY_WIDTH
    vl_ref[0] = rest[:, o:o + GLA_WIDTH].astype(BF16)
    o += GLA_WIDTH
    rl_ref[0] = rest[:, o:o + GLA_WIDTH]

    pre = _bdot(rest[:, GATE_OFF:].astype(BF16), wgate_ref[...]) + bgate_ref[...]
    logsig = jnp.minimum(pre, 0.0) - jnp.log(1.0 + jnp.exp(-jnp.abs(pre)))
    la = logsig * (1.0 / GATE_TAU)
    laf_ref[0] = la[:, :GLA_KEY_WIDTH]
    lab_ref[0] = la[:, GLA_KEY_WIDTH:]


def _ffn_inproj(x, p, tabs):
    b, s, _ = x.shape
    nt = s // TM
    c2 = lambda i, j: (0, 0)
    tok = lambda w: pl.BlockSpec((1, TM, w), lambda i, j: (i, j, 0))
    out_shape = (
        jax.ShapeDtypeStruct((b, s, D_MODEL), F32),
        jax.ShapeDtypeStruct((b, ATT_WIDTH, s), BF16),
        jax.ShapeDtypeStruct((b, s, KV_W), BF16),
        jax.ShapeDtypeStruct((b, N_KV_HEADS, nt, V_ROWS, TM), BF16),
        jax.ShapeDtypeStruct((b, s, GLA_KEY_WIDTH), F32),
        jax.ShapeDtypeStruct((b, s, GLA_KEY_WIDTH), F32),
        jax.ShapeDtypeStruct((b, s, GLA_WIDTH), BF16),
        jax.ShapeDtypeStruct((b, s, GLA_WIDTH), F32),
        jax.ShapeDtypeStruct((b, s, GLA_KEY_WIDTH), F32),
        jax.ShapeDtypeStruct((b, s, GLA_KEY_WIDTH), F32),
    )
    out_specs = (
        tok(D_MODEL),
        pl.BlockSpec((1, ATT_WIDTH, TM), lambda i, j: (i, 0, j)),
        tok(KV_W),
        pl.BlockSpec((1, N_KV_HEADS, 1, V_ROWS, TM), lambda i, j: (i, 0, j, 0, 0)),
        tok(GLA_KEY_WIDTH), tok(GLA_KEY_WIDTH), tok(GLA_WIDTH), tok(GLA_WIDTH), tok(GLA_KEY_WIDTH),
        tok(GLA_KEY_WIDTH),
    )
    in_specs = [
        tok(D_MODEL),
        pl.BlockSpec((1, D_MODEL), c2),
        _resident((D_MODEL, 2 * D_FF)),
        _resident((D_FF, D_MODEL)),
        pl.BlockSpec((1, D_MODEL), c2),
        _resident((QV_ROWS, D_MODEL)),
        _resident((D_MODEL, REST_W)),
        pl.BlockSpec((HEAD_DIM, 1), c2),
        pl.BlockSpec((1, KV_W), c2),
        pl.BlockSpec((HEAD_DIM, TM), lambda i, j: (0, j)),
        pl.BlockSpec((HEAD_DIM, TM), lambda i, j: (0, j)),
        pl.BlockSpec((TM, KV_W), lambda i, j: (j, 0)),
        pl.BlockSpec((TM, KV_W), lambda i, j: (j, 0)),
        pl.BlockSpec((KV_W, KV_W), c2),
        pl.BlockSpec((LANES, 2 * GLA_KEY_WIDTH), c2),
        pl.BlockSpec((1, 2 * GLA_KEY_WIDTH), c2),
    ]
    return pl.pallas_call(
        _ffn_inproj_body,
        grid=(b, nt),
        in_specs=in_specs,
        out_specs=out_specs,
        out_shape=out_shape,
        compiler_params=_cparams(("parallel", "parallel")),
        name="ffn_inproj",
    )(x, p["norm_ffn1"], p["w_ffn1_gu"], p["w_ffn1_down"],
      p["norm_mix"], p["w_qv_t"], p["w_rest"], p["q_gain_col"], p["k_gain_row"],
      tabs["cos_t"], tabs["sin_t"], tabs["cos_n"], tabs["sin_n"], p["grp_avg"], p["w_gate"], p["b_gate"])


def _attn_body(trips_ref, qt_ref, k_ref, vt_ref, o_ref, qcat_ref, acc_ref, *bufs, n_chunks):
    kvh = pl.program_id(1)
    tq = qt_ref.shape[2]
    cols = [slice(g * tq, (g + 1) * tq) for g in range(KV_GROUP)]

    qblk = qt_ref[0]
    top = jnp.concatenate([qblk[g * HEAD_DIM:(g + 1) * HEAD_DIM] for g in range(KV_GROUP)], axis=1)
    zero = jnp.zeros_like(top)
    qcat_ref[...] = jnp.where(kvh == 0, jnp.concatenate([top, zero], axis=0), jnp.concatenate([zero, top], axis=0))
    acc_ref[...] = jnp.zeros(acc_ref.shape, F32)

    def scores(c, dst_ref):
        kc = k_ref[0, pl.ds(pl.multiple_of(c * TM, TM), TM), :]
        maxes = []
        for sl in cols:
            s = _bdot(kc, qcat_ref[:, sl])
            dst_ref[:, sl] = s
            maxes.append(jnp.max(s, axis=0, keepdims=True))
        return jnp.concatenate(maxes, axis=1)

    def absorb(c, src_ref, chunk_max, m_old):
        m_new = jnp.maximum(m_old, chunk_max)
        alpha = jnp.exp2(m_old - m_new)
        vt = vt_ref[0, 0, c]
        acc_ref[...] = alpha * acc_ref[...]
        for sl in cols:
            p = jnp.exp2(src_ref[:, sl] - m_new[:, sl]).astype(BF16)
            acc_ref[:, sl] += _bdot(vt, p)
        return m_new

    n_buf = len(bufs)

    def group(first, carry, last_group):
        m, pending = carry[0], list(carry[1:])
        for j in range(n_buf):
            c = first + j
            if not (last_group and j + SCORE_LEAD >= n_buf):
                pending.append(scores(c + SCORE_LEAD, bufs[(j + SCORE_LEAD) % n_buf]))
            m = absorb(c, bufs[j], pending.pop(0), m)
        return (m, *pending)

    carry = (jnp.full((1, KV_GROUP * tq), NEG_BIG, F32), *[scores(c, bufs[c]) for c in range(SCORE_LEAD)])
    carry = lax.fori_loop(0, trips_ref[0], lambda i, cr: group(i * n_buf, cr, False), carry)
    group(n_chunks - n_buf, carry, True)

    acc = acc_ref[...]
    out_t = acc[:HEAD_DIM] * (1.0 / acc[HEAD_DIM:HEAD_DIM + 1])
    heads = [out_t[:, sl].T for sl in cols]
    o_ref[0] = jnp.concatenate(heads, axis=1).astype(BF16)


def _attn(qt, k, vt):
    b, _, s = qt.shape
    n_chunks = s // TM
    assert n_chunks % SCORE_BUFFERS == 0
    tq = min(TQ, s)
    mq = KV_GROUP * tq
    trips = jnp.full((1,), n_chunks // SCORE_BUFFERS - 1, jnp.int32)
    return pl.pallas_call(
        functools.partial(_attn_body, n_chunks=n_chunks),
        grid_spec=pltpu.PrefetchScalarGridSpec(
            num_scalar_prefetch=1,
            grid=(b, N_KV_HEADS, s // tq),
            in_specs=[
                pl.BlockSpec((1, KV_GROUP * HEAD_DIM, tq), lambda i, h, j, n: (i, h, j)),
                pl.BlockSpec((1, s, KV_W), lambda i, h, j, n: (i, 0, 0)),
                pl.BlockSpec((1, 1, n_chunks, V_ROWS, TM), lambda i, h, j, n: (i, h, 0, 0, 0)),
            ],
            out_specs=pl.BlockSpec((1, tq, KV_GROUP * HEAD_DIM), lambda i, h, j, n: (i, j, h)),
            scratch_shapes=[
                pltpu.VMEM((KV_W, mq), BF16),
                pltpu.VMEM((V_ROWS, mq), F32),
            ] + [pltpu.VMEM((TM, mq + LANES), F32)] * SCORE_BUFFERS,
        ),
        out_shape=jax.ShapeDtypeStruct((b, s, ATT_WIDTH), BF16),
        compiler_params=_cparams(("parallel", "parallel", "arbitrary")),
        name="attn",
    )(trips, qt, k, vt)


def _gla_direction(q_ref, k_ref, v_ref, la_ref, o_ref, state_ref, order, cum, keep, ref_idx, last_idx):
    heads = range(GLA_HEADS)
    psl = [slice(p * LANES, (p + 1) * LANES) for p in range(GLA_HEADS * GLA_DK // LANES)]
    vsl = [slice(h * GLA_DV, (h + 1) * GLA_DV) for h in heads]
    rows = [slice(ci * CHUNK, (ci + 1) * CHUNK) for ci in order]
    n = range(len(rows))

    b = []
    for sl in rows:
        la = la_ref[0, sl, :]
        la_hi = la.astype(BF16)
        la_lo = (la - la_hi.astype(F32)).astype(BF16)
        b.append(_bdot(cum, la_hi) + _bdot(cum, la_lo))

    qr, kr, kd, qb, dec, v = [], [], [], [], [], []
    for sl, bc in zip(rows, b):
        b_ref = bc[ref_idx:ref_idx + 1]
        b_last = bc[last_idx:last_idx + 1]
        qs = q_ref[0, sl, :] * (GLA_DK ** -0.5)
        k = k_ref[0, sl, :]
        qr.append((qs * jnp.exp(bc - b_ref)).astype(BF16))
        kr.append((k * jnp.exp(b_ref - bc)).astype(BF16))
        kd.append((k * jnp.exp(b_last - bc)).astype(BF16))
        qb.append((qs * jnp.exp(bc)).astype(BF16))
        dec.append(jnp.exp(b_last))
        v.append(v_ref[0, sl, :])

    lane = lax.broadcasted_iota(jnp.int32, (1, LANES), 1)
    own = [lane < GLA_DK, lane >= GLA_DK]
    keep2 = jnp.concatenate([keep, keep], axis=0)
    a = [[None] * GLA_HEADS for _ in n]
    for i in n:
        for p, sl in enumerate(psl):
            qr_p = qr[i][:, sl]
            stacked = jnp.concatenate([jnp.where(own[0], qr_p, 0), jnp.where(own[1], qr_p, 0)], axis=0)
            a2 = jnp.where(keep2, _nt_dot(stacked, kr[i][:, sl]), 0.0).astype(BF16)
            a[i][2 * p] = a2[:CHUNK]
            a[i][2 * p + 1] = a2[CHUNK:]
    v_t = [[v[i][:, vsl[h]].astype(F32).T.astype(BF16) for h in heads] for i in n]
    o_intra = [[_bdot(a[i][h], v[i][:, vsl[h]]) for h in heads] for i in n]
    d_state = [[_bdot(v_t[i][h], jnp.where(own[h % 2], kd[i][:, psl[h // 2]], 0)) for h in heads] for i in n]

    state = [state_ref[h] for h in heads]
    entering = []
    for i in n:
        entering.append(state)
        state = [dec[i][:, psl[h // 2]] * state[h] + d_state[i][h] for h in heads]
    for h in heads:
        state_ref[h] = state[h]

    o_inter = [[_nt_dot(qb[i][:, psl[h // 2]], entering[i][h].astype(BF16)) for h in heads] for i in n]
    for i in n:
        o_ref[0, rows[i], :] = jnp.concatenate([o_intra[i][h] + o_inter[i][h] for h in heads], axis=1)


def _gla_body(qf_ref, kf_ref, vf_ref, laf_ref, qb_ref, kb_ref, vb_ref, lab_ref, of_ref, ob_ref, sf_ref, sb_ref,
              *, n_chunks):
    @pl.when(pl.program_id(1) == 0)
    def _():
        sf_ref[...] = jnp.zeros(sf_ref.shape, F32)
        sb_ref[...] = jnp.zeros(sb_ref.shape, F32)

    r = lax.broadcasted_iota(jnp.int32, (CHUNK, CHUNK), 0)
    c = lax.broadcasted_iota(jnp.int32, (CHUNK, CHUNK), 1)
    _gla_direction(qf_ref, kf_ref, vf_ref, laf_ref, of_ref, sf_ref, range(n_chunks),
                   jnp.where(c <= r, 1.0, 0.0).astype(BF16), c <= r, CHUNK // 2, CHUNK - 1)
    _gla_direction(qb_ref, kb_ref, vb_ref, lab_ref, ob_ref, sb_ref, range(n_chunks - 1, -1, -1),
                   jnp.where(c >= r, 1.0, 0.0).astype(BF16), c > r, CHUNK // 2 - 1, 0)


def _gla(ql, kl, vl, laf, lab):
    b, s, _ = ql.shape
    tb = min(GLA_BLOCK, s)
    nb = s // tb
    fwd = lambda w: pl.BlockSpec((1, tb, w), lambda i, j: (i, j, 0))
    bwd = lambda w: pl.BlockSpec((1, tb, w), lambda i, j: (i, nb - 1 - j, 0))
    return pl.pallas_call(
        functools.partial(_gla_body, n_chunks=tb // CHUNK),
        grid=(b, nb),
        in_specs=[fwd(GLA_KEY_WIDTH), fwd(GLA_KEY_WIDTH), fwd(GLA_WIDTH), fwd(GLA_KEY_WIDTH),
                  bwd(GLA_KEY_WIDTH), bwd(GLA_KEY_WIDTH), bwd(GLA_WIDTH), bwd(GLA_KEY_WIDTH)],
        out_specs=(fwd(GLA_WIDTH), bwd(GLA_WIDTH)),
        out_shape=(jax.ShapeDtypeStruct((b, s, GLA_WIDTH), F32), jax.ShapeDtypeStruct((b, s, GLA_WIDTH), F32)),
        scratch_shapes=[pltpu.VMEM((GLA_HEADS, GLA_DV, LANES), F32), pltpu.VMEM((GLA_HEADS, GLA_DV, LANES), F32)],
        compiler_params=_cparams(("parallel", "arbitrary")),
        name="gla",
    )(ql, kl, vl, laf, ql, kl, vl, lab)


def _outproj_ffn_body(h_ref, oa_ref, of_ref, ob_ref, r_ref, gg_ref, wa_ref, wg_ref, g2_ref, wgu_ref, wd_ref, og_ref,
                      o_ref):
    o = of_ref[...] + ob_ref[...]
    gain = gg_ref[...]
    r = r_ref[...]
    parts = []
    for hh in range(GLA_HEADS):
        sl = slice(hh * GLA_DV, (hh + 1) * GLA_DV)
        rh = r[:, sl]
        parts.append((_rms_rows(o[:, sl], gain[:, sl]) * (rh * (1.0 / (1.0 + jnp.exp(-rh))))).astype(BF16))
    og = jnp.concatenate(parts, axis=1)
    h = h_ref[...] + _bdot(oa_ref[...], wa_ref[...]) + _bdot(og, wg_ref[...])
    o_ref[...] = _rms_rows(_swiglu_half_step(h, g2_ref[...], wgu_ref, wd_ref), og_ref[...])


def _outproj_ffn(h, o_att, o_f, o_b, r, p):
    t = h.shape[0]
    const = lambda i: (0, 0)
    tok = lambda w: pl.BlockSpec((TM, w), lambda i: (i, 0))
    return pl.pallas_call(
        _outproj_ffn_body,
        grid=(t // TM,),
        in_specs=[tok(D_MODEL), tok(ATT_WIDTH), tok(GLA_WIDTH), tok(GLA_WIDTH), tok(GLA_WIDTH),
                  pl.BlockSpec((1, GLA_WIDTH), const),
                  _resident((ATT_WIDTH, D_MODEL)),
                  _resident((GLA_WIDTH, D_MODEL)),
                  pl.BlockSpec((1, D_MODEL), const),
                  _resident((D_MODEL, 2 * D_FF)),
                  _resident((D_FF, D_MODEL)),
                  pl.BlockSpec((1, D_MODEL), const)],
        out_specs=tok(D_MODEL),
        out_shape=jax.ShapeDtypeStruct((t, D_MODEL), F32),
        compiler_params=_cparams(("parallel",)),
        name="outproj_ffn",
    )(h, o_att, o_f, o_b, r, p["gla_gain"], p["w_out_att"], p["w_out_gla"],
      p["norm_ffn2"], p["w_ffn2_gu"], p["w_ffn2_down"], p["norm_out"])


def _rope_tables(seq_len):
    rows = seq_len // GRID_W
    row = jnp.repeat(jnp.arange(rows, dtype=F32), GRID_W)
    col = jnp.tile(jnp.arange(GRID_W, dtype=F32), rows)
    inv_freq = 1.0 / (ROPE_THETA ** (jnp.arange(0, AXIS_DIM, 2, dtype=F32) / AXIS_DIM))
    ang_r = row[:, None] * inv_freq[None, :]
    ang_c = col[:, None] * inv_freq[None, :]
    ang = jnp.concatenate([ang_r, ang_r, ang_c, ang_c], axis=-1)
    sign = jnp.where((jnp.arange(HEAD_DIM) % 32) < 16, -1.0, 1.0).astype(F32)
    cos, sin = jnp.cos(ang), jnp.sin(ang) * sign[None, :]
    return {
        "cos_t": cos.T, "sin_t": sin.T,
        "cos_n": jnp.tile(cos, (1, N_KV_HEADS)), "sin_n": jnp.tile(sin, (1, N_KV_HEADS)),
    }


def _prep_layer(l, norm_ffn1, w_ffn1_gu, w_ffn1_down, norm_mix, w_in, q_norm, k_norm, w_gate_f, b_gate_f,
                w_gate_b, b_gate_b, gla_norm, w_out, norm_ffn2, w_ffn2_gu, w_ffn2_down, norm_out):
    w = w_in[l]
    o_q, o_k, o_v = 0, ATT_WIDTH, ATT_WIDTH + KV_W
    o_rest = ATT_WIDTH + 2 * KV_W
    n_rest = 2 * GLA_KEY_WIDTH + 2 * GLA_WIDTH
    gates = jnp.pad(w[:, o_rest + n_rest:], ((0, 0), (0, LANES - 2 * GATE_RANK)))
    w_rest = jnp.concatenate([w[:, o_k:o_v], w[:, o_rest:o_rest + n_rest], gates], axis=1).astype(BF16)
    w_qv_t = jnp.concatenate([w[:, o_q:o_k], w[:, o_v:o_rest]], axis=1).T.astype(BF16)
    w_gate = jnp.zeros((LANES, 2 * GLA_KEY_WIDTH), F32)
    w_gate = w_gate.at[:GATE_RANK, :GLA_KEY_WIDTH].set(w_gate_f[l])
    w_gate = w_gate.at[GATE_RANK:2 * GATE_RANK, GLA_KEY_WIDTH:].set(w_gate_b[l])
    hid = jnp.arange(KV_W) // HEAD_DIM
    grp_avg = jnp.where(hid[:, None] == hid[None, :], 1.0 / HEAD_DIM, 0.0).astype(BF16)
    row = lambda v: v.reshape(1, -1).astype(F32)
    return {
        "norm_ffn1": row(norm_ffn1[l]), "w_ffn1_gu": w_ffn1_gu[l].astype(BF16), "w_ffn1_down": w_ffn1_down[l].astype(BF16),
        "norm_mix": row(norm_mix[l]), "w_qv_t": w_qv_t, "w_rest": w_rest,
        "q_gain_col": q_norm[l].reshape(HEAD_DIM, 1).astype(F32),
        "k_gain_row": row(jnp.tile(k_norm[l], N_KV_HEADS)),
        "grp_avg": grp_avg, "w_gate": w_gate.astype(BF16),
        "b_gate": row(jnp.concatenate([b_gate_f[l], b_gate_b[l]])),
        "gla_gain": row(jnp.tile(gla_norm[l], GLA_HEADS)),
        "w_out_att": w_out[l][:ATT_WIDTH].astype(BF16), "w_out_gla": w_out[l][ATT_WIDTH:].astype(BF16),
        "norm_ffn2": row(norm_ffn2[l]), "w_ffn2_gu": w_ffn2_gu[l].astype(BF16), "w_ffn2_down": w_ffn2_down[l].astype(BF16),
        "norm_out": row(norm_out[l]),
    }


def _layer(x, p, tabs):
    b, s, d = x.shape
    t = b * s
    h, qt, k, vt, ql, kl, vl, rl, laf, lab = _ffn_inproj(x, p, tabs)
    o_att = _attn(qt, k, vt)
    o_f, o_b = _gla(ql, kl, vl, laf, lab)
    y = _outproj_ffn(h.reshape(t, d), o_att.reshape(t, ATT_WIDTH), o_f.reshape(t, GLA_WIDTH),
                     o_b.reshape(t, GLA_WIDTH), rl.reshape(t, GLA_WIDTH), p)
    return y.reshape(b, s, d)


def _trunk(x, layers):
    tabs = _rope_tables(x.shape[1])
    h = x
    for p in layers:
        h = _layer(h, p, tabs)
    return h


def kernel(x_prompt, x_sample, norm_ffn1, w_ffn1_gu, w_ffn1_down, norm_mix, w_in, q_norm, k_norm, w_gate_f, b_gate_f,
           w_gate_b, b_gate_b, gla_norm, w_out, norm_ffn2, w_ffn2_gu, w_ffn2_down, norm_out):
    params = (norm_ffn1, w_ffn1_gu, w_ffn1_down, norm_mix, w_in, q_norm, k_norm, w_gate_f, b_gate_f,
              w_gate_b, b_gate_b, gla_norm, w_out, norm_ffn2, w_ffn2_gu, w_ffn2_down, norm_out)
    layers = [_prep_layer(l, *params) for l in range(norm_ffn1.shape[0])]
    return (_trunk(x_prompt, layers), _trunk(x_sample, layers))
```

```python
import functools

import jax
import jax.numpy as jnp
from jax import lax
from jax.experimental import pallas as pl
from jax.experimental.pallas import tpu as pltpu

F32 = jnp.float32
BF16 = jnp.bfloat16

D_MODEL = 1024
GRID_W = 64
ATT_WIDTH = 512
HEAD_DIM = 64
N_HEADS = 8
N_KV_HEADS = 2
KV_GROUP = 4
AXIS_DIM = 32
ROPE_THETA = 10000.0
GLA_WIDTH = 512
GLA_HEADS = 4
GLA_DV = 128
GLA_DK = 64
GLA_KEY_WIDTH = 256
GATE_RANK = 16
GATE_TAU = 16.0
CHUNK = 64
D_FF = 2816
EPS = 1e-6

LANES = 128
KV_W = N_KV_HEADS * HEAD_DIM
QV_ROWS = ATT_WIDTH + KV_W
REST_W = KV_W + 2 * GLA_KEY_WIDTH + 2 * GLA_WIDTH + LANES
GATE_OFF = REST_W - LANES

TM = 512
TQ = 256
ATTN_LEAD = 2
MAX_FIXED_SHIFT = 48.0
SCORE_BUFFERS = 4
FF_CHUNK = 256
N_FF_CHUNKS = D_FF // FF_CHUNK
GLA_BLOCK = 512
NEG_BIG = -1e30
LOG2E = 1.4426950408889634
V_ROWS = 2 * HEAD_DIM
VMEM_LIMIT = 56 * 1024 * 1024


def _cparams(sem):
    return pltpu.CompilerParams(dimension_semantics=sem, vmem_limit_bytes=VMEM_LIMIT)


def _rms_rows(x, gain_row):
    ms = jnp.mean(x * x, axis=-1, keepdims=True)
    return x * lax.rsqrt(ms + EPS) * gain_row


def _bdot(a, b):
    return jnp.dot(a, b, preferred_element_type=F32)


def _nt_dot(a, b):
    return lax.dot_general(a, b, (((1,), (1,)), ((), ())), preferred_element_type=F32)


def _swiglu_half_step(x, gain_row, wgu_ref, wd_ref):
    xn = _rms_rows(x, gain_row).astype(BF16)
    acc = None
    for c in range(N_FF_CHUNKS):
        g = _bdot(xn, wgu_ref[:, c * FF_CHUNK:(c + 1) * FF_CHUNK])
        u = _bdot(xn, wgu_ref[:, D_FF + c * FF_CHUNK:D_FF + (c + 1) * FF_CHUNK])
        a = (g * (1.0 / (1.0 + jnp.exp(-g))) * u).astype(BF16)
        d = _bdot(a, wd_ref[c * FF_CHUNK:(c + 1) * FF_CHUNK, :])
        acc = d if acc is None else acc + d
    return x + 0.5 * acc


def _resident(shape):
    return pl.BlockSpec(shape, lambda *_: (0,) * len(shape), pipeline_mode=pl.Buffered(1))


def _swap16(t, axis):
    n = t.shape[axis] // 16
    parts = [lax.slice_in_dim(t, (i ^ 1) * 16, (i ^ 1) * 16 + 16, axis=axis) for i in range(n)]
    return jnp.concatenate(parts, axis=axis)


def _ffn_inproj_body(x_ref, g1_ref, wgu_ref, wd_ref, g_ref, wqv_ref, wrest_ref, qg_ref, kg_ref,
                     cost_ref, sint_ref, cosn_ref, sinn_ref, grp_ref, wgate_ref, bgate_ref,
                     h_ref, qt_ref, k_ref, vt_ref, ql_ref, kl_ref, vl_ref, rl_ref, laf_ref, lab_ref):
    h = _swiglu_half_step(x_ref[0], g1_ref[...], wgu_ref, wd_ref)
    h_ref[0] = h
    xn = _rms_rows(h, g_ref[...]).astype(BF16)
    tm = xn.shape[0]

    ut = _nt_dot(wqv_ref[...], xn)
    q = ut[:ATT_WIDTH].reshape(N_HEADS, HEAD_DIM, tm)
    q = q * lax.rsqrt(jnp.mean(q * q, axis=1, keepdims=True) + EPS) * qg_ref[...][None]
    q = q * cost_ref[...][None] + _swap16(q, 1) * sint_ref[...][None]
    qt_ref[0] = (q * (HEAD_DIM ** -0.5 * LOG2E)).reshape(ATT_WIDTH, tm).astype(BF16)

    row = lax.broadcasted_iota(jnp.int32, (V_ROWS - HEAD_DIM, tm), 0)
    ones_rows = jnp.where(row == 0, 1.0, 0.0).astype(BF16)
    for kvh in range(N_KV_HEADS):
        v = ut[ATT_WIDTH + kvh * HEAD_DIM:ATT_WIDTH + (kvh + 1) * HEAD_DIM].astype(BF16)
        vt_ref[0, kvh, 0] = jnp.concatenate([v, ones_rows], axis=0)

    rest = _bdot(xn, wrest_ref[...])

    k = rest[:, :KV_W]
    ss = k * k
    ss_hi = ss.astype(BF16)
    ss_lo = (ss - ss_hi.astype(F32)).astype(BF16)
    ms = _bdot(ss_hi, grp_ref[...]) + _bdot(ss_lo, grp_ref[...])
    k = k * lax.rsqrt(ms + EPS) * kg_ref[...]
    lane = lax.broadcasted_iota(jnp.int32, k.shape, 1)
    k_sw = jnp.where((lane % 32) < 16, pltpu.roll(k, LANES - 16, 1), pltpu.roll(k, 16, 1))
    k = k * cosn_ref[...] + k_sw * sinn_ref[...]
    tail = jnp.where(lane == HEAD_DIM, 1.0, 0.0)
    k_ref[0, 0] = jnp.where(lane < HEAD_DIM, k, tail).astype(BF16)
    k_ref[0, 1] = jnp.where(lane < HEAD_DIM, pltpu.roll(k, HEAD_DIM, 1), tail).astype(BF16)

    o = KV_W
    ql_ref[0] = rest[:, o:o + GLA_KEY_WIDTH]
    o += GLA_KEY_WIDTH
    kl_ref[0] = rest[:, o:o + GLA_KEY_WIDTH]
    o += GLA_KEY_WIDTH
    vl_ref[0] = rest[:, o:o + GLA_WIDTH].astype(BF16)
    o += GLA_WIDTH
    rl_ref[0] = rest[:, o:o + GLA_WIDTH]

    pre = _bdot(rest[:, GATE_OFF:].astype(BF16), wgate_ref[...]) + bgate_ref[...]
    logsig = jnp.minimum(pre, 0.0) - jnp.log(1.0 + jnp.exp(-jnp.abs(pre)))
    la = logsig * (1.0 / GATE_TAU)
    laf_ref[0] = la[:, :GLA_KEY_WIDTH]
    lab_ref[0] = la[:, GLA_KEY_WIDTH:]


def _ffn_inproj(x, p, tabs):
    b, s, _ = x.shape
    nt = s // TM
    c2 = lambda i, j: (0, 0)
    tok = lambda w: pl.BlockSpec((1, TM, w), lambda i, j: (i, j, 0))
    out_shape = (
        jax.ShapeDtypeStruct((b, s, D_MODEL), F32),
        jax.ShapeDtypeStruct((b, ATT_WIDTH, s), BF16),
        jax.ShapeDtypeStruct((b, N_KV_HEADS, s, LANES), BF16),
        jax.ShapeDtypeStruct((b, N_KV_HEADS, nt, V_ROWS, TM), BF16),
        jax.ShapeDtypeStruct((b, s, GLA_KEY_WIDTH), F32),
        jax.ShapeDtypeStruct((b, s, GLA_KEY_WIDTH), F32),
        jax.ShapeDtypeStruct((b, s, GLA_WIDTH), BF16),
        jax.ShapeDtypeStruct((b, s, GLA_WIDTH), F32),
        jax.ShapeDtypeStruct((b, s, GLA_KEY_WIDTH), F32),
        jax.ShapeDtypeStruct((b, s, GLA_KEY_WIDTH), F32),
    )
    out_specs = (
        tok(D_MODEL),
        pl.BlockSpec((1, ATT_WIDTH, TM), lambda i, j: (i, 0, j)),
        pl.BlockSpec((1, N_KV_HEADS, TM, LANES), lambda i, j: (i, 0, j, 0)),
        pl.BlockSpec((1, N_KV_HEADS, 1, V_ROWS, TM), lambda i, j: (i, 0, j, 0, 0)),
        tok(GLA_KEY_WIDTH), tok(GLA_KEY_WIDTH), tok(GLA_WIDTH), tok(GLA_WIDTH), tok(GLA_KEY_WIDTH),
        tok(GLA_KEY_WIDTH),
    )
    in_specs = [
        tok(D_MODEL),
        pl.BlockSpec((1, D_MODEL), c2),
        _resident((D_MODEL, 2 * D_FF)),
        _resident((D_FF, D_MODEL)),
        pl.BlockSpec((1, D_MODEL), c2),
        _resident((QV_ROWS, D_MODEL)),
        _resident((D_MODEL, REST_W)),
        pl.BlockSpec((HEAD_DIM, 1), c2),
        pl.BlockSpec((1, KV_W), c2),
        pl.BlockSpec((HEAD_DIM, TM), lambda i, j: (0, j)),
        pl.BlockSpec((HEAD_DIM, TM), lambda i, j: (0, j)),
        pl.BlockSpec((TM, KV_W), lambda i, j: (j, 0)),
        pl.BlockSpec((TM, KV_W), lambda i, j: (j, 0)),
        pl.BlockSpec((KV_W, KV_W), c2),
        pl.BlockSpec((LANES, 2 * GLA_KEY_WIDTH), c2),
        pl.BlockSpec((1, 2 * GLA_KEY_WIDTH), c2),
    ]
    return pl.pallas_call(
        _ffn_inproj_body,
        grid=(b, nt),
        in_specs=in_specs,
        out_specs=out_specs,
        out_shape=out_shape,
        compiler_params=_cparams(("parallel", "parallel")),
        name="ffn_inproj",
    )(x, p["norm_ffn1"], p["w_ffn1_gu"], p["w_ffn1_down"],
      p["norm_mix"], p["w_qv_t"], p["w_rest"], p["q_gain_col"], p["k_gain_row"],
      tabs["cos_t"], tabs["sin_t"], tabs["cos_n"], tabs["sin_n"], p["grp_avg"], p["w_gate"], p["b_gate"])


def _query_matrix(qt_ref, extra_row):
    qblk = qt_ref[0]
    top = jnp.concatenate([qblk[g * HEAD_DIM:(g + 1) * HEAD_DIM] for g in range(KV_GROUP)], axis=1)
    row = lax.broadcasted_iota(jnp.int32, top.shape, 0)
    bottom = jnp.where(row == 0, extra_row, 0.0).astype(BF16)
    return jnp.concatenate([top, bottom], axis=0)


def _write_heads(o_ref, acc, cols):
    out_t = acc[:HEAD_DIM] * (1.0 / acc[HEAD_DIM:HEAD_DIM + 1])
    heads = [out_t[:, sl].T for sl in cols]
    o_ref[0] = jnp.concatenate(heads, axis=1).astype(BF16)


def _attn_shifted_body(trips_ref, shift_ref, qt_ref, k_ref, vt_ref, o_ref, qcat_ref, acc_ref, *, n_chunks):
    tq = qt_ref.shape[2]
    cols = [slice(g * tq, (g + 1) * tq) for g in range(KV_GROUP)]
    qcat_ref[...] = _query_matrix(qt_ref, jnp.full((1, KV_GROUP * tq), -shift_ref[0], F32))
    acc_ref[...] = jnp.zeros(acc_ref.shape, F32)

    def scores(c, g):
        kc = k_ref[0, 0, pl.ds(pl.multiple_of(c * TM, TM), TM), :]
        return _bdot(kc, qcat_ref[:, cols[g]])

    def absorb(c, g, s):
        acc_ref[:, cols[g]] += _bdot(vt_ref[0, 0, c], jnp.exp2(s).astype(BF16))

    def group(first):
        units = [(first + j, g) for j in range(SCORE_BUFFERS) for g in range(KV_GROUP)]
        pending = [scores(*u) for u in units[:ATTN_LEAD]]
        for idx, (c, g) in enumerate(units):
            if idx + ATTN_LEAD < len(units):
                pending.append(scores(*units[idx + ATTN_LEAD]))
            absorb(c, g, pending.pop(0))

    def trip(i, carry):
        group(i * SCORE_BUFFERS)
        return carry

    lax.fori_loop(0, trips_ref[0] + 1, trip, 0)
    _write_heads(o_ref, acc_ref[...], cols)


def _attn_online_body(trips_ref, shift_ref, qt_ref, k_ref, vt_ref, o_ref, qcat_ref, acc_ref, *bufs, n_chunks):
    tq = qt_ref.shape[2]
    cols = [slice(g * tq, (g + 1) * tq) for g in range(KV_GROUP)]
    qcat_ref[...] = _query_matrix(qt_ref, jnp.zeros((1, KV_GROUP * tq), F32))
    acc_ref[...] = jnp.zeros(acc_ref.shape, F32)

    def scores(c, dst_ref):
        kc = k_ref[0, 0, pl.ds(pl.multiple_of(c * TM, TM), TM), :]
        maxes = []
        for sl in cols:
            s = _bdot(kc, qcat_ref[:, sl])
            dst_ref[:, sl] = s
            maxes.append(jnp.max(s, axis=0, keepdims=True))
        return jnp.concatenate(maxes, axis=1)

    def absorb(c, src_ref, chunk_max, m_old):
        m_new = jnp.maximum(m_old, chunk_max)
        alpha = jnp.exp2(m_old - m_new)
        vt = vt_ref[0, 0, c]
        acc_ref[...] = alpha * acc_ref[...]
        for sl in cols:
            p = jnp.exp2(src_ref[:, sl] - m_new[:, sl]).astype(BF16)
            acc_ref[:, sl] += _bdot(vt, p)
        return m_new

    n_buf = len(bufs)

    def group(first, carry, last_group):
        m, chunk_max = carry
        for j in range(n_buf):
            c = first + j
            next_max = None if last_group and j + 1 == n_buf else scores(c + 1, bufs[(j + 1) % n_buf])
            m = absorb(c, bufs[j], chunk_max, m)
            chunk_max = next_max
        return m, chunk_max

    carry = (jnp.full((1, KV_GROUP * tq), NEG_BIG, F32), scores(0, bufs[0]))
    carry = lax.fori_loop(0, trips_ref[0], lambda i, cr: group(i * n_buf, cr, False), carry)
    group(n_chunks - n_buf, carry, True)
    _write_heads(o_ref, acc_ref[...], cols)


def _attn_call(body, n_score_buffers, score_bound, qt, k, vt):
    b, _, s = qt.shape
    n_chunks = s // TM
    assert n_chunks % SCORE_BUFFERS == 0
    tq = min(TQ, s)
    mq = KV_GROUP * tq
    trips = jnp.full((1,), n_chunks // SCORE_BUFFERS - 1, jnp.int32)
    return pl.pallas_call(
        functools.partial(body, n_chunks=n_chunks),
        grid_spec=pltpu.PrefetchScalarGridSpec(
            num_scalar_prefetch=2,
            grid=(b, N_KV_HEADS, s // tq),
            in_specs=[
                pl.BlockSpec((1, KV_GROUP * HEAD_DIM, tq), lambda i, h, j, *_: (i, h, j)),
                pl.BlockSpec((1, 1, s, LANES), lambda i, h, j, *_: (i, h, 0, 0)),
                pl.BlockSpec((1, 1, n_chunks, V_ROWS, TM), lambda i, h, j, *_: (i, h, 0, 0, 0)),
            ],
            out_specs=pl.BlockSpec((1, tq, KV_GROUP * HEAD_DIM), lambda i, h, j, *_: (i, j, h)),
            scratch_shapes=[
                pltpu.VMEM((LANES, mq), BF16),
                pltpu.VMEM((V_ROWS, mq), F32),
            ] + [pltpu.VMEM((TM, mq + LANES), F32)] * n_score_buffers,
        ),
        out_shape=jax.ShapeDtypeStruct((b, s, ATT_WIDTH), BF16),
        compiler_params=_cparams(("parallel", "parallel", "arbitrary")),
        name="attn",
    )(trips, score_bound, qt, k, vt)


def _attn(qt, k, vt, score_bound):
    return lax.cond(score_bound[0] <= MAX_FIXED_SHIFT,
                    functools.partial(_attn_call, _attn_shifted_body, 0),
                    functools.partial(_attn_call, _attn_online_body, SCORE_BUFFERS),
                    score_bound, qt, k, vt)


def _gla_direction(q_ref, k_ref, v_ref, la_ref, o_ref, state_ref, order, cum, keep, ref_idx, last_idx):
    heads = range(GLA_HEADS)
    psl = [slice(p * LANES, (p + 1) * LANES) for p in range(GLA_HEADS * GLA_DK // LANES)]
    vsl = [slice(h * GLA_DV, (h + 1) * GLA_DV) for h in heads]
    rows = [slice(ci * CHUNK, (ci + 1) * CHUNK) for ci in order]
    n = range(len(rows))

    b = []
    for sl in rows:
        la = la_ref[0, sl, :]
        la_hi = la.astype(BF16)
        la_lo = (la - la_hi.astype(F32)).astype(BF16)
        b.append(_bdot(cum, la_hi) + _bdot(cum, la_lo))

    qr, kr, kd, qb, dec, v = [], [], [], [], [], []
    for sl, bc in zip(rows, b):
        b_ref = bc[ref_idx:ref_idx + 1]
        b_last = bc[last_idx:last_idx + 1]
        qs = q_ref[0, sl, :] * (GLA_DK ** -0.5)
        k = k_ref[0, sl, :]
        qr.append((qs * jnp.exp(bc - b_ref)).astype(BF16))
        kr.append((k * jnp.exp(b_ref - bc)).astype(BF16))
        kd.append((k * jnp.exp(b_last - bc)).astype(BF16))
        qb.append((qs * jnp.exp(bc)).astype(BF16))
        dec.append(jnp.exp(b_last))
        v.append(v_ref[0, sl, :])

    lane = lax.broadcasted_iota(jnp.int32, (1, LANES), 1)
    own = [lane < GLA_DK, lane >= GLA_DK]
    keep2 = jnp.concatenate([keep, keep], axis=0)
    a = [[None] * GLA_HEADS for _ in n]
    for i in n:
        for p, sl in enumerate(psl):
            qr_p = qr[i][:, sl]
            stacked = jnp.concatenate([jnp.where(own[0], qr_p, 0), jnp.where(own[1], qr_p, 0)], axis=0)
            a2 = jnp.where(keep2, _nt_dot(stacked, kr[i][:, sl]), 0.0).astype(BF16)
            a[i][2 * p] = a2[:CHUNK]
            a[i][2 * p + 1] = a2[CHUNK:]
    v_t = [[v[i][:, vsl[h]].astype(F32).T.astype(BF16) for h in heads] for i in n]
    o_intra = [[_bdot(a[i][h], v[i][:, vsl[h]]) for h in heads] for i in n]
    d_state = [[_bdot(v_t[i][h], jnp.where(own[h % 2], kd[i][:, psl[h // 2]], 0)) for h in heads] for i in n]

    state = [state_ref[h] for h in heads]
    entering = []
    for i in n:
        entering.append(state)
        state = [dec[i][:, psl[h // 2]] * state[h] + d_state[i][h] for h in heads]
    for h in heads:
        state_ref[h] = state[h]

    o_inter = [[_nt_dot(qb[i][:, psl[h // 2]], entering[i][h].astype(BF16)) for h in heads] for i in n]
    for i in n:
        o_ref[0, rows[i], :] = jnp.concatenate([o_intra[i][h] + o_inter[i][h] for h in heads], axis=1)


def _gla_body(qf_ref, kf_ref, vf_ref, laf_ref, qb_ref, kb_ref, vb_ref, lab_ref, of_ref, ob_ref, sf_ref, sb_ref,
              *, n_chunks):
    @pl.when(pl.program_id(1) == 0)
    def _():
        sf_ref[...] = jnp.zeros(sf_ref.shape, F32)
        sb_ref[...] = jnp.zeros(sb_ref.shape, F32)

    r = lax.broadcasted_iota(jnp.int32, (CHUNK, CHUNK), 0)
    c = lax.broadcasted_iota(jnp.int32, (CHUNK, CHUNK), 1)
    _gla_direction(qf_ref, kf_ref, vf_ref, laf_ref, of_ref, sf_ref, range(n_chunks),
                   jnp.where(c <= r, 1.0, 0.0).astype(BF16), c <= r, CHUNK // 2, CHUNK - 1)
    _gla_direction(qb_ref, kb_ref, vb_ref, lab_ref, ob_ref, sb_ref, range(n_chunks - 1, -1, -1),
                   jnp.where(c >= r, 1.0, 0.0).astype(BF16), c > r, CHUNK // 2 - 1, 0)


def _gla(ql, kl, vl, laf, lab):
    b, s, _ = ql.shape
    tb = min(GLA_BLOCK, s)
    nb = s // tb
    fwd = lambda w: pl.BlockSpec((1, tb, w), lambda i, j: (i, j, 0))
    bwd = lambda w: pl.BlockSpec((1, tb, w), lambda i, j: (i, nb - 1 - j, 0))
    return pl.pallas_call(
        functools.partial(_gla_body, n_chunks=tb // CHUNK),
        grid=(b, nb),
        in_specs=[fwd(GLA_KEY_WIDTH), fwd(GLA_KEY_WIDTH), fwd(GLA_WIDTH), fwd(GLA_KEY_WIDTH),
                  bwd(GLA_KEY_WIDTH), bwd(GLA_KEY_WIDTH), bwd(GLA_WIDTH), bwd(GLA_KEY_WIDTH)],
        out_specs=(fwd(GLA_WIDTH), bwd(GLA_WIDTH)),
        out_shape=(jax.ShapeDtypeStruct((b, s, GLA_WIDTH), F32), jax.ShapeDtypeStruct((b, s, GLA_WIDTH), F32)),
        scratch_shapes=[pltpu.VMEM((GLA_HEADS, GLA_DV, LANES), F32), pltpu.VMEM((GLA_HEADS, GLA_DV, LANES), F32)],
        compiler_params=_cparams(("parallel", "arbitrary")),
        name="gla",
    )(ql, kl, vl, laf, ql, kl, vl, lab)


def _outproj_ffn_body(h_ref, oa_ref, of_ref, ob_ref, r_ref, gg_ref, wa_ref, wg_ref, g2_ref, wgu_ref, wd_ref, og_ref,
                      o_ref):
    o = of_ref[...] + ob_ref[...]
    gain = gg_ref[...]
    r = r_ref[...]
    parts = []
    for hh in range(GLA_HEADS):
        sl = slice(hh * GLA_DV, (hh + 1) * GLA_DV)
        rh = r[:, sl]
        parts.append((_rms_rows(o[:, sl], gain[:, sl]) * (rh * (1.0 / (1.0 + jnp.exp(-rh))))).astype(BF16))
    og = jnp.concatenate(parts, axis=1)
    h = h_ref[...] + _bdot(oa_ref[...], wa_ref[...]) + _bdot(og, wg_ref[...])
    o_ref[...] = _rms_rows(_swiglu_half_step(h, g2_ref[...], wgu_ref, wd_ref), og_ref[...])


def _outproj_ffn(h, o_att, o_f, o_b, r, p):
    t = h.shape[0]
    const = lambda i: (0, 0)
    tok = lambda w: pl.BlockSpec((TM, w), lambda i: (i, 0))
    return pl.pallas_call(
        _outproj_ffn_body,
        grid=(t // TM,),
        in_specs=[tok(D_MODEL), tok(ATT_WIDTH), tok(GLA_WIDTH), tok(GLA_WIDTH), tok(GLA_WIDTH),
                  pl.BlockSpec((1, GLA_WIDTH), const),
                  _resident((ATT_WIDTH, D_MODEL)),
                  _resident((GLA_WIDTH, D_MODEL)),
                  pl.BlockSpec((1, D_MODEL), const),
                  _resident((D_MODEL, 2 * D_FF)),
                  _resident((D_FF, D_MODEL)),
                  pl.BlockSpec((1, D_MODEL), const)],
        out_specs=tok(D_MODEL),
        out_shape=jax.ShapeDtypeStruct((t, D_MODEL), F32),
        compiler_params=_cparams(("parallel",)),
        name="outproj_ffn",
    )(h, o_att, o_f, o_b, r, p["gla_gain"], p["w_out_att"], p["w_out_gla"],
      p["norm_ffn2"], p["w_ffn2_gu"], p["w_ffn2_down"], p["norm_out"])


def _rope_tables(seq_len):
    rows = seq_len // GRID_W
    row = jnp.repeat(jnp.arange(rows, dtype=F32), GRID_W)
    col = jnp.tile(jnp.arange(GRID_W, dtype=F32), rows)
    inv_freq = 1.0 / (ROPE_THETA ** (jnp.arange(0, AXIS_DIM, 2, dtype=F32) / AXIS_DIM))
    ang_r = row[:, None] * inv_freq[None, :]
    ang_c = col[:, None] * inv_freq[None, :]
    ang = jnp.concatenate([ang_r, ang_r, ang_c, ang_c], axis=-1)
    sign = jnp.where((jnp.arange(HEAD_DIM) % 32) < 16, -1.0, 1.0).astype(F32)
    cos, sin = jnp.cos(ang), jnp.sin(ang) * sign[None, :]
    return {
        "cos_t": cos.T, "sin_t": sin.T,
        "cos_n": jnp.tile(cos, (1, N_KV_HEADS)), "sin_n": jnp.tile(sin, (1, N_KV_HEADS)),
    }


def _prep_layer(l, norm_ffn1, w_ffn1_gu, w_ffn1_down, norm_mix, w_in, q_norm, k_norm, w_gate_f, b_gate_f,
                w_gate_b, b_gate_b, gla_norm, w_out, norm_ffn2, w_ffn2_gu, w_ffn2_down, norm_out):
    w = w_in[l]
    o_q, o_k, o_v = 0, ATT_WIDTH, ATT_WIDTH + KV_W
    o_rest = ATT_WIDTH + 2 * KV_W
    n_rest = 2 * GLA_KEY_WIDTH + 2 * GLA_WIDTH
    gates = jnp.pad(w[:, o_rest + n_rest:], ((0, 0), (0, LANES - 2 * GATE_RANK)))
    w_rest = jnp.concatenate([w[:, o_k:o_v], w[:, o_rest:o_rest + n_rest], gates], axis=1).astype(BF16)
    w_qv_t = jnp.concatenate([w[:, o_q:o_k], w[:, o_v:o_rest]], axis=1).T.astype(BF16)
    w_gate = jnp.zeros((LANES, 2 * GLA_KEY_WIDTH), F32)
    w_gate = w_gate.at[:GATE_RANK, :GLA_KEY_WIDTH].set(w_gate_f[l])
    w_gate = w_gate.at[GATE_RANK:2 * GATE_RANK, GLA_KEY_WIDTH:].set(w_gate_b[l])
    hid = jnp.arange(KV_W) // HEAD_DIM
    grp_avg = jnp.where(hid[:, None] == hid[None, :], 1.0 / HEAD_DIM, 0.0).astype(BF16)
    row = lambda v: v.reshape(1, -1).astype(F32)
    return {
        "norm_ffn1": row(norm_ffn1[l]), "w_ffn1_gu": w_ffn1_gu[l].astype(BF16), "w_ffn1_down": w_ffn1_down[l].astype(BF16),
        "norm_mix": row(norm_mix[l]), "w_qv_t": w_qv_t, "w_rest": w_rest,
        "q_gain_col": q_norm[l].reshape(HEAD_DIM, 1).astype(F32),
        "k_gain_row": row(jnp.tile(k_norm[l], N_KV_HEADS)),
        "grp_avg": grp_avg, "w_gate": w_gate.astype(BF16),
        "score_bound": (HEAD_DIM ** 0.5 * LOG2E * jnp.max(jnp.abs(q_norm[l])) * jnp.max(jnp.abs(k_norm[l]))
                        ).reshape(1).astype(F32),
        "b_gate": row(jnp.concatenate([b_gate_f[l], b_gate_b[l]])),
        "gla_gain": row(jnp.tile(gla_norm[l], GLA_HEADS)),
        "w_out_att": w_out[l][:ATT_WIDTH].astype(BF16), "w_out_gla": w_out[l][ATT_WIDTH:].astype(BF16),
        "norm_ffn2": row(norm_ffn2[l]), "w_ffn2_gu": w_ffn2_gu[l].astype(BF16), "w_ffn2_down": w_ffn2_down[l].astype(BF16),
        "norm_out": row(norm_out[l]),
    }


def _layer(x, p, tabs):
    b, s, d = x.shape
    t = b * s
    h, qt, k, vt, ql, kl, vl, rl, laf, lab = _ffn_inproj(x, p, tabs)
    o_att = _attn(qt, k, vt, p["score_bound"])
    o_f, o_b = _gla(ql, kl, vl, laf, lab)
    y = _outproj_ffn(h.reshape(t, d), o_att.reshape(t, ATT_WIDTH), o_f.reshape(t, GLA_WIDTH),
                     o_b.reshape(t, GLA_WIDTH), rl.reshape(t, GLA_WIDTH), p)
    return y.reshape(b, s, d)


def _trunk(x, layers):
    tabs = _rope_tables(x.shape[1])
    h = x
    for p in layers:
        h = _layer(h, p, tabs)
    return h


def kernel(x_prompt, x_sample, norm_ffn1, w_ffn1_gu, w_ffn1_down, norm_mix, w_in, q_norm, k_norm, w_gate_f, b_gate_f,
           w_gate_b, b_gate_b, gla_norm, w_out, norm_ffn2, w_ffn2_gu, w_ffn2_down, norm_out):
    params = (norm_ffn1, w_ffn1_gu, w_ffn1_down, norm_mix, w_in, q_norm, k_norm, w_gate_f, b_gate_f,
              w_gate_b, b_gate_b, gla_norm, w_out, norm_ffn2, w_ffn2_gu, w_ffn2_down, norm_out)
    layers = [_prep_layer(l, *params) for l in range(norm_ffn1.shape[0])]
    return (_trunk(x_prompt, layers), _trunk(x_sample, layers))
```

```python
import functools

import jax
import jax.numpy as jnp
from jax import lax
from jax.experimental import pallas as pl
from jax.experimental.pallas import tpu as pltpu

F32 = jnp.float32
BF16 = jnp.bfloat16

D_MODEL = 1024
GRID_W = 64
ATT_WIDTH = 512
HEAD_DIM = 64
N_HEADS = 8
N_KV_HEADS = 2
KV_GROUP = 4
AXIS_DIM = 32
ROPE_THETA = 10000.0
GLA_WIDTH = 512
GLA_HEADS = 4
GLA_DV = 128
GLA_DK = 64
GLA_KEY_WIDTH = 256
GATE_RANK = 16
GATE_TAU = 16.0
CHUNK = 64
D_FF = 2816
EPS = 1e-6

LANES = 128
KV_W = N_KV_HEADS * HEAD_DIM
QV_ROWS = ATT_WIDTH + KV_W
REST_W = KV_W + 2 * GLA_KEY_WIDTH + 2 * GLA_WIDTH + LANES
GATE_OFF = REST_W - LANES

TM = 512
TQ = 256
ATTN_LEAD = 2
MAX_FIXED_SHIFT = 48.0
SCORE_BUFFERS = 4
FF_CHUNK = 256
N_FF_CHUNKS = D_FF // FF_CHUNK
GLA_BLOCK = 512
NEG_BIG = -1e30
LOG2E = 1.4426950408889634
V_ROWS = 2 * HEAD_DIM
VMEM_LIMIT = 56 * 1024 * 1024


def _cparams(sem):
    return pltpu.CompilerParams(dimension_semantics=sem, vmem_limit_bytes=VMEM_LIMIT)


def _rms_rows(x, gain_row):
    ms = jnp.mean(x * x, axis=-1, keepdims=True)
    return x * lax.rsqrt(ms + EPS) * gain_row


def _bdot(a, b):
    return jnp.dot(a, b, preferred_element_type=F32)


def _nt_dot(a, b):
    return lax.dot_general(a, b, (((1,), (1,)), ((), ())), preferred_element_type=F32)


def _swiglu_half_step(x, gain_row, wgu_ref, wd_ref):
    xn = _rms_rows(x, gain_row).astype(BF16)
    acc = None
    for c in range(N_FF_CHUNKS):
        g = _bdot(xn, wgu_ref[:, c * FF_CHUNK:(c + 1) * FF_CHUNK])
        u = _bdot(xn, wgu_ref[:, D_FF + c * FF_CHUNK:D_FF + (c + 1) * FF_CHUNK])
        a = (g * (1.0 / (1.0 + jnp.exp(-g))) * u).astype(BF16)
        d = _bdot(a, wd_ref[c * FF_CHUNK:(c + 1) * FF_CHUNK, :])
        acc = d if acc is None else acc + d
    return x + 0.5 * acc


def _resident(shape):
    return pl.BlockSpec(shape, lambda *_: (0,) * len(shape), pipeline_mode=pl.Buffered(1))


def _swap16(t, axis):
    n = t.shape[axis] // 16
    parts = [lax.slice_in_dim(t, (i ^ 1) * 16, (i ^ 1) * 16 + 16, axis=axis) for i in range(n)]
    return jnp.concatenate(parts, axis=axis)


def _ffn_inproj_body(x_ref, g1_ref, wgu_ref, wd_ref, g_ref, wqv_ref, wrest_ref, qg_ref, kg_ref,
                     cost_ref, sint_ref, cosn_ref, sinn_ref, grp_ref, wgate_ref, bgate_ref,
                     h_ref, qt_ref, k_ref, vt_ref, ql_ref, kl_ref, vl_ref, rl_ref, laf_ref, lab_ref):
    h = _swiglu_half_step(x_ref[0], g1_ref[...], wgu_ref, wd_ref)
    h_ref[0] = h
    xn = _rms_rows(h, g_ref[...]).astype(BF16)
    tm = xn.shape[0]

    ut = _nt_dot(wqv_ref[...], xn)
    q = ut[:ATT_WIDTH].reshape(N_HEADS, HEAD_DIM, tm)
    q = q * lax.rsqrt(jnp.mean(q * q, axis=1, keepdims=True) + EPS) * qg_ref[...][None]
    q = q * cost_ref[...][None] + _swap16(q, 1) * sint_ref[...][None]
    qt_ref[0] = (q * (HEAD_DIM ** -0.5 * LOG2E)).reshape(ATT_WIDTH, tm).astype(BF16)

    row = lax.broadcasted_iota(jnp.int32, (V_ROWS - HEAD_DIM, tm), 0)
    ones_rows = jnp.where(row == 0, 1.0, 0.0).astype(BF16)
    for kvh in range(N_KV_HEADS):
        v = ut[ATT_WIDTH + kvh * HEAD_DIM:ATT_WIDTH + (kvh + 1) * HEAD_DIM].astype(BF16)
        vt_ref[0, kvh, 0] = jnp.concatenate([v, ones_rows], axis=0)

    rest = _bdot(xn, wrest_ref[...])

    k = rest[:, :KV_W]
    ss = k * k
    ss_hi = ss.astype(BF16)
    ss_lo = (ss - ss_hi.astype(F32)).astype(BF16)
    ms = _bdot(ss_hi, grp_ref[...]) + _bdot(ss_lo, grp_ref[...])
    k = k * lax.rsqrt(ms + EPS) * kg_ref[...]
    lane = lax.broadcasted_iota(jnp.int32, k.shape, 1)
    k_sw = jnp.where((lane % 32) < 16, pltpu.roll(k, LANES - 16, 1), pltpu.roll(k, 16, 1))
    k = k * cosn_ref[...] + k_sw * sinn_ref[...]
    tail = jnp.where(lane == HEAD_DIM, 1.0, 0.0)
    k_ref[0, 0] = jnp.where(lane < HEAD_DIM, k, tail).astype(BF16)
    k_ref[0, 1] = jnp.where(lane < HEAD_DIM, pltpu.roll(k, HEAD_DIM, 1), tail).astype(BF16)

    o = KV_W
    ql_ref[0] = rest[:, o:o + GLA_KEY_WIDTH]
    o += GLA_KEY_WIDTH
    kl_ref[0] = rest[:, o:o + GLA_KEY_WIDTH]
    o += GLA_KEY_WIDTH
    vl_ref[0] = rest[:, o:o + GLA_WIDTH].astype(BF16)
    o += GLA_WIDTH
    rl_ref[0] = rest[:, o:o + GLA_WIDTH]

    pre = _bdot(rest[:, GATE_OFF:].astype(BF16), wgate_ref[...]) + bgate_ref[...]
    logsig = jnp.minimum(pre, 0.0) - jnp.log(1.0 + jnp.exp(-jnp.abs(pre)))
    la = logsig * (1.0 / GATE_TAU)
    laf_ref[0] = la[:, :GLA_KEY_WIDTH]
    lab_ref[0] = la[:, GLA_KEY_WIDTH:]


def _ffn_inproj(x, p, tabs):
    b, s, _ = x.shape
    nt = s // TM
    c2 = lambda i, j: (0, 0)
    tok = lambda w: pl.BlockSpec((1, TM, w), lambda i, j: (i, j, 0))
    out_shape = (
        jax.ShapeDtypeStruct((b, s, D_MODEL), F32),
        jax.ShapeDtypeStruct((b, ATT_WIDTH, s), BF16),
        jax.ShapeDtypeStruct((b, N_KV_HEADS, s, LANES), BF16),
        jax.ShapeDtypeStruct((b, N_KV_HEADS, nt, V_ROWS, TM), BF16),
        jax.ShapeDtypeStruct((b, s, GLA_KEY_WIDTH), F32),
        jax.ShapeDtypeStruct((b, s, GLA_KEY_WIDTH), F32),
        jax.ShapeDtypeStruct((b, s, GLA_WIDTH), BF16),
        jax.ShapeDtypeStruct((b, s, GLA_WIDTH), F32),
        jax.ShapeDtypeStruct((b, s, GLA_KEY_WIDTH), F32),
        jax.ShapeDtypeStruct((b, s, GLA_KEY_WIDTH), F32),
    )
    out_specs = (
        tok(D_MODEL),
        pl.BlockSpec((1, ATT_WIDTH, TM), lambda i, j: (i, 0, j)),
        pl.BlockSpec((1, N_KV_HEADS, TM, LANES), lambda i, j: (i, 0, j, 0)),
        pl.BlockSpec((1, N_KV_HEADS, 1, V_ROWS, TM), lambda i, j: (i, 0, j, 0, 0)),
        tok(GLA_KEY_WIDTH), tok(GLA_KEY_WIDTH), tok(GLA_WIDTH), tok(GLA_WIDTH), tok(GLA_KEY_WIDTH),
        tok(GLA_KEY_WIDTH),
    )
    in_specs = [
        tok(D_MODEL),
        pl.BlockSpec((1, D_MODEL), c2),
        _resident((D_MODEL, 2 * D_FF)),
        _resident((D_FF, D_MODEL)),
        pl.BlockSpec((1, D_MODEL), c2),
        _resident((QV_ROWS, D_MODEL)),
        _resident((D_MODEL, REST_W)),
        pl.BlockSpec((HEAD_DIM, 1), c2),
        pl.BlockSpec((1, KV_W), c2),
        pl.BlockSpec((HEAD_DIM, TM), lambda i, j: (0, j)),
        pl.BlockSpec((HEAD_DIM, TM), lambda i, j: (0, j)),
        pl.BlockSpec((TM, KV_W), lambda i, j: (j, 0)),
        pl.BlockSpec((TM, KV_W), lambda i, j: (j, 0)),
        pl.BlockSpec((KV_W, KV_W), c2),
        pl.BlockSpec((LANES, 2 * GLA_KEY_WIDTH), c2),
        pl.BlockSpec((1, 2 * GLA_KEY_WIDTH), c2),
    ]
    return pl.pallas_call(
        _ffn_inproj_body,
        grid=(b, nt),
        in_specs=in_specs,
        out_specs=out_specs,
        out_shape=out_shape,
        compiler_params=_cparams(("parallel", "parallel")),
        name="ffn_inproj",
    )(x, p["norm_ffn1"], p["w_ffn1_gu"], p["w_ffn1_down"],
      p["norm_mix"], p["w_qv_t"], p["w_rest"], p["q_gain_col"], p["k_gain_row"],
      tabs["cos_t"], tabs["sin_t"], tabs["cos_n"], tabs["sin_n"], p["grp_avg"], p["w_gate"], p["b_gate"])


def _query_matrix(qt_ref, extra_row):
    qblk = qt_ref[0]
    top = jnp.concatenate([qblk[g * HEAD_DIM:(g + 1) * HEAD_DIM] for g in range(KV_GROUP)], axis=1)
    row = lax.broadcasted_iota(jnp.int32, top.shape, 0)
    bottom = jnp.where(row == 0, extra_row, 0.0).astype(BF16)
    return jnp.concatenate([top, bottom], axis=0)


def _write_heads(o_ref, acc, cols):
    out_t = acc[:HEAD_DIM] * (1.0 / acc[HEAD_DIM:HEAD_DIM + 1])
    heads = [out_t[:, sl].T for sl in cols]
    o_ref[0] = jnp.concatenate(heads, axis=1).astype(BF16)


def _attn_shifted_body(trips_ref, shift_ref, qt_ref, k_ref, vt_ref, o_ref, qcat_ref, acc_ref, *, n_chunks):
    tq = qt_ref.shape[2]
    cols = [slice(g * tq, (g + 1) * tq) for g in range(KV_GROUP)]
    qcat_ref[...] = _query_matrix(qt_ref, jnp.full((1, KV_GROUP * tq), -shift_ref[0], F32))
    acc_ref[...] = jnp.zeros(acc_ref.shape, F32)

    def scores(c, g):
        return _bdot(k_ref[0, 0, c * TM:(c + 1) * TM, :], qcat_ref[:, cols[g]])

    def absorb(c, g, s):
        acc_ref[:, cols[g]] += _bdot(vt_ref[0, 0, c], jnp.exp2(s).astype(BF16))

    units = [(c, g) for c in range(n_chunks) for g in range(KV_GROUP)]
    pending = [scores(*u) for u in units[:ATTN_LEAD]]
    for idx, (c, g) in enumerate(units):
        if idx + ATTN_LEAD < len(units):
            pending.append(scores(*units[idx + ATTN_LEAD]))
        absorb(c, g, pending.pop(0))
    _write_heads(o_ref, acc_ref[...], cols)


def _attn_online_body(trips_ref, shift_ref, qt_ref, k_ref, vt_ref, o_ref, qcat_ref, acc_ref, *bufs, n_chunks):
    tq = qt_ref.shape[2]
    cols = [slice(g * tq, (g + 1) * tq) for g in range(KV_GROUP)]
    qcat_ref[...] = _query_matrix(qt_ref, jnp.zeros((1, KV_GROUP * tq), F32))
    acc_ref[...] = jnp.zeros(acc_ref.shape, F32)

    def scores(c, dst_ref):
        kc = k_ref[0, 0, pl.ds(pl.multiple_of(c * TM, TM), TM), :]
        maxes = []
        for sl in cols:
            s = _bdot(kc, qcat_ref[:, sl])
            dst_ref[:, sl] = s
            maxes.append(jnp.max(s, axis=0, keepdims=True))
        return jnp.concatenate(maxes, axis=1)

    def absorb(c, src_ref, chunk_max, m_old):
        m_new = jnp.maximum(m_old, chunk_max)
        alpha = jnp.exp2(m_old - m_new)
        vt = vt_ref[0, 0, c]
        acc_ref[...] = alpha * acc_ref[...]
        for sl in cols:
            p = jnp.exp2(src_ref[:, sl] - m_new[:, sl]).astype(BF16)
            acc_ref[:, sl] += _bdot(vt, p)
        return m_new

    n_buf = len(bufs)

    def group(first, carry, last_group):
        m, chunk_max = carry
        for j in range(n_buf):
            c = first + j
            next_max = None if last_group and j + 1 == n_buf else scores(c + 1, bufs[(j + 1) % n_buf])
            m = absorb(c, bufs[j], chunk_max, m)
            chunk_max = next_max
        return m, chunk_max

    carry = (jnp.full((1, KV_GROUP * tq), NEG_BIG, F32), scores(0, bufs[0]))
    carry = lax.fori_loop(0, trips_ref[0], lambda i, cr: group(i * n_buf, cr, False), carry)
    group(n_chunks - n_buf, carry, True)
    _write_heads(o_ref, acc_ref[...], cols)


def _attn_call(body, n_score_buffers, score_bound, qt, k, vt):
    b, _, s = qt.shape
    n_chunks = s // TM
    assert n_chunks % SCORE_BUFFERS == 0
    tq = min(TQ, s)
    mq = KV_GROUP * tq
    trips = jnp.full((1,), n_chunks // SCORE_BUFFERS - 1, jnp.int32)
    return pl.pallas_call(
        functools.partial(body, n_chunks=n_chunks),
        grid_spec=pltpu.PrefetchScalarGridSpec(
            num_scalar_prefetch=2,
            grid=(b, N_KV_HEADS, s // tq),
            in_specs=[
                pl.BlockSpec((1, KV_GROUP * HEAD_DIM, tq), lambda i, h, j, *_: (i, h, j)),
                pl.BlockSpec((1, 1, s, LANES), lambda i, h, j, *_: (i, h, 0, 0)),
                pl.BlockSpec((1, 1, n_chunks, V_ROWS, TM), lambda i, h, j, *_: (i, h, 0, 0, 0)),
            ],
            out_specs=pl.BlockSpec((1, tq, KV_GROUP * HEAD_DIM), lambda i, h, j, *_: (i, j, h)),
            scratch_shapes=[
                pltpu.VMEM((LANES, mq), BF16),
                pltpu.VMEM((V_ROWS, mq), F32),
            ] + [pltpu.VMEM((TM, mq + LANES), F32)] * n_score_buffers,
        ),
        out_shape=jax.ShapeDtypeStruct((b, s, ATT_WIDTH), BF16),
        compiler_params=_cparams(("parallel", "parallel", "arbitrary")),
        name="attn",
    )(trips, score_bound, qt, k, vt)


def _attn(qt, k, vt, score_bound):
    return lax.cond(score_bound[0] <= MAX_FIXED_SHIFT,
                    functools.partial(_attn_call, _attn_shifted_body, 0),
                    functools.partial(_attn_call, _attn_online_body, SCORE_BUFFERS),
                    score_bound, qt, k, vt)


def _gla_direction(q_ref, k_ref, v_ref, la_ref, o_ref, state_ref, order, cum, keep, ref_idx, last_idx):
    heads = range(GLA_HEADS)
    psl = [slice(p * LANES, (p + 1) * LANES) for p in range(GLA_HEADS * GLA_DK // LANES)]
    vsl = [slice(h * GLA_DV, (h + 1) * GLA_DV) for h in heads]
    rows = [slice(ci * CHUNK, (ci + 1) * CHUNK) for ci in order]
    n = range(len(rows))

    b = []
    for sl in rows:
        la = la_ref[0, sl, :]
        la_hi = la.astype(BF16)
        la_lo = (la - la_hi.astype(F32)).astype(BF16)
        b.append(_bdot(cum, la_hi) + _bdot(cum, la_lo))

    qr, kr, kd, qb, dec, v = [], [], [], [], [], []
    for sl, bc in zip(rows, b):
        b_ref = bc[ref_idx:ref_idx + 1]
        b_last = bc[last_idx:last_idx + 1]
        qs = q_ref[0, sl, :] * (GLA_DK ** -0.5)
        k = k_ref[0, sl, :]
        qr.append((qs * jnp.exp(bc - b_ref)).astype(BF16))
        kr.append((k * jnp.exp(b_ref - bc)).astype(BF16))
        kd.append((k * jnp.exp(b_last - bc)).astype(BF16))
        qb.append((qs * jnp.exp(bc)).astype(BF16))
        dec.append(jnp.exp(b_last))
        v.append(v_ref[0, sl, :])

    lane = lax.broadcasted_iota(jnp.int32, (1, LANES), 1)
    own = [lane < GLA_DK, lane >= GLA_DK]
    keep2 = jnp.concatenate([keep, keep], axis=0)
    a = [[None] * GLA_HEADS for _ in n]
    for i in n:
        for p, sl in enumerate(psl):
            qr_p = qr[i][:, sl]
            stacked = jnp.concatenate([jnp.where(own[0], qr_p, 0), jnp.where(own[1], qr_p, 0)], axis=0)
            a2 = jnp.where(keep2, _nt_dot(stacked, kr[i][:, sl]), 0.0).astype(BF16)
            a[i][2 * p] = a2[:CHUNK]
            a[i][2 * p + 1] = a2[CHUNK:]
    v_t = [[v[i][:, vsl[h]].astype(F32).T.astype(BF16) for h in heads] for i in n]
    o_intra = [[_bdot(a[i][h], v[i][:, vsl[h]]) for h in heads] for i in n]
    d_state = [[_bdot(v_t[i][h], jnp.where(own[h % 2], kd[i][:, psl[h // 2]], 0)) for h in heads] for i in n]

    state = [state_ref[h] for h in heads]
    entering = []
    for i in n:
        entering.append(state)
        state = [dec[i][:, psl[h // 2]] * state[h] + d_state[i][h] for h in heads]
    for h in heads:
        state_ref[h] = state[h]

    o_inter = [[_nt_dot(qb[i][:, psl[h // 2]], entering[i][h].astype(BF16)) for h in heads] for i in n]
    for i in n:
        o_ref[0, rows[i], :] = jnp.concatenate([o_intra[i][h] + o_inter[i][h] for h in heads], axis=1)


def _gla_body(qf_ref, kf_ref, vf_ref, laf_ref, qb_ref, kb_ref, vb_ref, lab_ref, of_ref, ob_ref, sf_ref, sb_ref,
              *, n_chunks):
    @pl.when(pl.program_id(1) == 0)
    def _():
        sf_ref[...] = jnp.zeros(sf_ref.shape, F32)
        sb_ref[...] = jnp.zeros(sb_ref.shape, F32)

    r = lax.broadcasted_iota(jnp.int32, (CHUNK, CHUNK), 0)
    c = lax.broadcasted_iota(jnp.int32, (CHUNK, CHUNK), 1)
    _gla_direction(qf_ref, kf_ref, vf_ref, laf_ref, of_ref, sf_ref, range(n_chunks),
                   jnp.where(c <= r, 1.0, 0.0).astype(BF16), c <= r, CHUNK // 2, CHUNK - 1)
    _gla_direction(qb_ref, kb_ref, vb_ref, lab_ref, ob_ref, sb_ref, range(n_chunks - 1, -1, -1),
                   jnp.where(c >= r, 1.0, 0.0).astype(BF16), c > r, CHUNK // 2 - 1, 0)


def _gla(ql, kl, vl, laf, lab):
    b, s, _ = ql.shape
    tb = min(GLA_BLOCK, s)
    nb = s // tb
    fwd = lambda w: pl.BlockSpec((1, tb, w), lambda i, j: (i, j, 0))
    bwd = lambda w: pl.BlockSpec((1, tb, w), lambda i, j: (i, nb - 1 - j, 0))
    return pl.pallas_call(
        functools.partial(_gla_body, n_chunks=tb // CHUNK),
        grid=(b, nb),
        in_specs=[fwd(GLA_KEY_WIDTH), fwd(GLA_KEY_WIDTH), fwd(GLA_WIDTH), fwd(GLA_KEY_WIDTH),
                  bwd(GLA_KEY_WIDTH), bwd(GLA_KEY_WIDTH), bwd(GLA_WIDTH), bwd(GLA_KEY_WIDTH)],
        out_specs=(fwd(GLA_WIDTH), bwd(GLA_WIDTH)),
        out_shape=(jax.ShapeDtypeStruct((b, s, GLA_WIDTH), F32), jax.ShapeDtypeStruct((b, s, GLA_WIDTH), F32)),
        scratch_shapes=[pltpu.VMEM((GLA_HEADS, GLA_DV, LANES), F32), pltpu.VMEM((GLA_HEADS, GLA_DV, LANES), F32)],
        compiler_params=_cparams(("parallel", "arbitrary")),
        name="gla",
    )(ql, kl, vl, laf, ql, kl, vl, lab)


def _outproj_ffn_body(h_ref, oa_ref, of_ref, ob_ref, r_ref, gg_ref, wa_ref, wg_ref, g2_ref, wgu_ref, wd_ref, og_ref,
                      o_ref):
    o = of_ref[...] + ob_ref[...]
    gain = gg_ref[...]
    r = r_ref[...]
    parts = []
    for hh in range(GLA_HEADS):
        sl = slice(hh * GLA_DV, (hh + 1) * GLA_DV)
        rh = r[:, sl]
        parts.append((_rms_rows(o[:, sl], gain[:, sl]) * (rh * (1.0 / (1.0 + jnp.exp(-rh))))).astype(BF16))
    og = jnp.concatenate(parts, axis=1)
    h = h_ref[...] + _bdot(oa_ref[...], wa_ref[...]) + _bdot(og, wg_ref[...])
    o_ref[...] = _rms_rows(_swiglu_half_step(h, g2_ref[...], wgu_ref, wd_ref), og_ref[...])


def _outproj_ffn(h, o_att, o_f, o_b, r, p):
    t = h.shape[0]
    const = lambda i: (0, 0)
    tok = lambda w: pl.BlockSpec((TM, w), lambda i: (i, 0))
    return pl.pallas_call(
        _outproj_ffn_body,
        grid=(t // TM,),
        in_specs=[tok(D_MODEL), tok(ATT_WIDTH), tok(GLA_WIDTH), tok(GLA_WIDTH), tok(GLA_WIDTH),
                  pl.BlockSpec((1, GLA_WIDTH), const),
                  _resident((ATT_WIDTH, D_MODEL)),
                  _resident((GLA_WIDTH, D_MODEL)),
                  pl.BlockSpec((1, D_MODEL), const),
                  _resident((D_MODEL, 2 * D_FF)),
                  _resident((D_FF, D_MODEL)),
                  pl.BlockSpec((1, D_MODEL), const)],
        out_specs=tok(D_MODEL),
        out_shape=jax.ShapeDtypeStruct((t, D_MODEL), F32),
        compiler_params=_cparams(("parallel",)),
        name="outproj_ffn",
    )(h, o_att, o_f, o_b, r, p["gla_gain"], p["w_out_att"], p["w_out_gla"],
      p["norm_ffn2"], p["w_ffn2_gu"], p["w_ffn2_down"], p["norm_out"])


def _rope_tables(seq_len):
    rows = seq_len // GRID_W
    row = jnp.repeat(jnp.arange(rows, dtype=F32), GRID_W)
    col = jnp.tile(jnp.arange(GRID_W, dtype=F32), rows)
    inv_freq = 1.0 / (ROPE_THETA ** (jnp.arange(0, AXIS_DIM, 2, dtype=F32) / AXIS_DIM))
    ang_r = row[:, None] * inv_freq[None, :]
    ang_c = col[:, None] * inv_freq[None, :]
    ang = jnp.concatenate([ang_r, ang_r, ang_c, ang_c], axis=-1)
    sign = jnp.where((jnp.arange(HEAD_DIM) % 32) < 16, -1.0, 1.0).astype(F32)
    cos, sin = jnp.cos(ang), jnp.sin(ang) * sign[None, :]
    return {
        "cos_t": cos.T, "sin_t": sin.T,
        "cos_n": jnp.tile(cos, (1, N_KV_HEADS)), "sin_n": jnp.tile(sin, (1, N_KV_HEADS)),
    }


def _prep_layer(l, norm_ffn1, w_ffn1_gu, w_ffn1_down, norm_mix, w_in, q_norm, k_norm, w_gate_f, b_gate_f,
                w_gate_b, b_gate_b, gla_norm, w_out, norm_ffn2, w_ffn2_gu, w_ffn2_down, norm_out):
    w = w_in[l]
    o_q, o_k, o_v = 0, ATT_WIDTH, ATT_WIDTH + KV_W
    o_rest = ATT_WIDTH + 2 * KV_W
    n_rest = 2 * GLA_KEY_WIDTH + 2 * GLA_WIDTH
    gates = jnp.pad(w[:, o_rest + n_rest:], ((0, 0), (0, LANES - 2 * GATE_RANK)))
    w_rest = jnp.concatenate([w[:, o_k:o_v], w[:, o_rest:o_rest + n_rest], gates], axis=1).astype(BF16)
    w_qv_t = jnp.concatenate([w[:, o_q:o_k], w[:, o_v:o_rest]], axis=1).T.astype(BF16)
    w_gate = jnp.zeros((LANES, 2 * GLA_KEY_WIDTH), F32)
    w_gate = w_gate.at[:GATE_RANK, :GLA_KEY_WIDTH].set(w_gate_f[l])
    w_gate = w_gate.at[GATE_RANK:2 * GATE_RANK, GLA_KEY_WIDTH:].set(w_gate_b[l])
    hid = jnp.arange(KV_W) // HEAD_DIM
    grp_avg = jnp.where(hid[:, None] == hid[None, :], 1.0 / HEAD_DIM, 0.0).astype(BF16)
    row = lambda v: v.reshape(1, -1).astype(F32)
    return {
        "norm_ffn1": row(norm_ffn1[l]), "w_ffn1_gu": w_ffn1_gu[l].astype(BF16), "w_ffn1_down": w_ffn1_down[l].astype(BF16),
        "norm_mix": row(norm_mix[l]), "w_qv_t": w_qv_t, "w_rest": w_rest,
        "q_gain_col": q_norm[l].reshape(HEAD_DIM, 1).astype(F32),
        "k_gain_row": row(jnp.tile(k_norm[l], N_KV_HEADS)),
        "grp_avg": grp_avg, "w_gate": w_gate.astype(BF16),
        "score_bound": (HEAD_DIM ** 0.5 * LOG2E * jnp.max(jnp.abs(q_norm[l])) * jnp.max(jnp.abs(k_norm[l]))
                        ).reshape(1).astype(F32),
        "b_gate": row(jnp.concatenate([b_gate_f[l], b_gate_b[l]])),
        "gla_gain": row(jnp.tile(gla_norm[l], GLA_HEADS)),
        "w_out_att": w_out[l][:ATT_WIDTH].astype(BF16), "w_out_gla": w_out[l][ATT_WIDTH:].astype(BF16),
        "norm_ffn2": row(norm_ffn2[l]), "w_ffn2_gu": w_ffn2_gu[l].astype(BF16), "w_ffn2_down": w_ffn2_down[l].astype(BF16),
        "norm_out": row(norm_out[l]),
    }


def _layer(x, p, tabs):
    b, s, d = x.shape
    t = b * s
    h, qt, k, vt, ql, kl, vl, rl, laf, lab = _ffn_inproj(x, p, tabs)
    o_att = _attn(qt, k, vt, p["score_bound"])
    o_f, o_b = _gla(ql, kl, vl, laf, lab)
    y = _outproj_ffn(h.reshape(t, d), o_att.reshape(t, ATT_WIDTH), o_f.reshape(t, GLA_WIDTH),
                     o_b.reshape(t, GLA_WIDTH), rl.reshape(t, GLA_WIDTH), p)
    return y.reshape(b, s, d)


def _trunk(x, layers):
    tabs = _rope_tables(x.shape[1])
    h = x
    for p in layers:
        h = _layer(h, p, tabs)
    return h


def kernel(x_prompt, x_sample, norm_ffn1, w_ffn1_gu, w_ffn1_down, norm_mix, w_in, q_norm, k_norm, w_gate_f, b_gate_f,
           w_gate_b, b_gate_b, gla_norm, w_out, norm_ffn2, w_ffn2_gu, w_ffn2_down, norm_out):
    params = (norm_ffn1, w_ffn1_gu, w_ffn1_down, norm_mix, w_in, q_norm, k_norm, w_gate_f, b_gate_f,
              w_gate_b, b_gate_b, gla_norm, w_out, norm_ffn2, w_ffn2_gu, w_ffn2_down, norm_out)
    layers = [_prep_layer(l, *params) for l in range(norm_ffn1.shape[0])]
    return (_trunk(x_prompt, layers), _trunk(x_sample, layers))
```

```python
import functools

import jax
import jax.numpy as jnp
from jax import lax
from jax.experimental import pallas as pl
from jax.experimental.pallas import tpu as pltpu

F32 = jnp.float32
BF16 = jnp.bfloat16

D_MODEL = 1024
GRID_W = 64
ATT_WIDTH = 512
HEAD_DIM = 64
N_HEADS = 8
N_KV_HEADS = 2
KV_GROUP = 4
AXIS_DIM = 32
ROPE_THETA = 10000.0
GLA_WIDTH = 512
GLA_HEADS = 4
GLA_DV = 128
GLA_DK = 64
GLA_KEY_WIDTH = 256
GATE_RANK = 16
GATE_TAU = 16.0
CHUNK = 64
D_FF = 2816
EPS = 1e-6

LANES = 128
KV_W = N_KV_HEADS * HEAD_DIM
QV_ROWS = ATT_WIDTH + KV_W
REST_W = KV_W + 2 * GLA_KEY_WIDTH + 2 * GLA_WIDTH + LANES
GATE_OFF = REST_W - LANES

TM = 512
TQ = 256
TQ_SHIFTED = 512
ATTN_LEAD = 2
MAX_FIXED_SHIFT = 48.0
SCORE_BUFFERS = 4
FF_CHUNK = 256
N_FF_CHUNKS = D_FF // FF_CHUNK
GLA_BLOCK = 512
NEG_BIG = -1e30
LOG2E = 1.4426950408889634
V_ROWS = 2 * HEAD_DIM
VMEM_LIMIT = 56 * 1024 * 1024


def _cparams(sem):
    return pltpu.CompilerParams(dimension_semantics=sem, vmem_limit_bytes=VMEM_LIMIT)


def _rms_rows(x, gain_row):
    ms = jnp.mean(x * x, axis=-1, keepdims=True)
    return x * lax.rsqrt(ms + EPS) * gain_row


def _bdot(a, b):
    return jnp.dot(a, b, preferred_element_type=F32)


def _nt_dot(a, b):
    return lax.dot_general(a, b, (((1,), (1,)), ((), ())), preferred_element_type=F32)


def _swiglu_half_step(x, gain_row, wgu_ref, wd_ref):
    xn = _rms_rows(x, gain_row).astype(BF16)
    acc = None
    for c in range(N_FF_CHUNKS):
        g = _bdot(xn, wgu_ref[:, c * FF_CHUNK:(c + 1) * FF_CHUNK])
        u = _bdot(xn, wgu_ref[:, D_FF + c * FF_CHUNK:D_FF + (c + 1) * FF_CHUNK])
        a = (g * (1.0 / (1.0 + jnp.exp(-g))) * u).astype(BF16)
        d = _bdot(a, wd_ref[c * FF_CHUNK:(c + 1) * FF_CHUNK, :])
        acc = d if acc is None else acc + d
    return x + 0.5 * acc


def _resident(shape):
    return pl.BlockSpec(shape, lambda *_: (0,) * len(shape), pipeline_mode=pl.Buffered(1))


def _swap16(t, axis):
    n = t.shape[axis] // 16
    parts = [lax.slice_in_dim(t, (i ^ 1) * 16, (i ^ 1) * 16 + 16, axis=axis) for i in range(n)]
    return jnp.concatenate(parts, axis=axis)


def _ffn_inproj_body(x_ref, g1_ref, wgu_ref, wd_ref, g_ref, wqv_ref, wrest_ref, qg_ref, kg_ref,
                     cost_ref, sint_ref, cosn_ref, sinn_ref, grp_ref, wgate_ref, bgate_ref,
                     h_ref, qt_ref, k_ref, vt_ref, ql_ref, kl_ref, vl_ref, rl_ref, laf_ref, lab_ref):
    h = _swiglu_half_step(x_ref[0], g1_ref[...], wgu_ref, wd_ref)
    h_ref[0] = h
    xn = _rms_rows(h, g_ref[...]).astype(BF16)
    tm = xn.shape[0]

    ut = _nt_dot(wqv_ref[...], xn)
    q = ut[:ATT_WIDTH].reshape(N_HEADS, HEAD_DIM, tm)
    q = q * lax.rsqrt(jnp.mean(q * q, axis=1, keepdims=True) + EPS) * qg_ref[...][None]
    q = q * cost_ref[...][None] + _swap16(q, 1) * sint_ref[...][None]
    qt_ref[0] = (q * (HEAD_DIM ** -0.5 * LOG2E)).reshape(ATT_WIDTH, tm).astype(BF16)

    row = lax.broadcasted_iota(jnp.int32, (V_ROWS - HEAD_DIM, tm), 0)
    ones_rows = jnp.where(row == 0, 1.0, 0.0).astype(BF16)
    for kvh in range(N_KV_HEADS):
        v = ut[ATT_WIDTH + kvh * HEAD_DIM:ATT_WIDTH + (kvh + 1) * HEAD_DIM].astype(BF16)
        vt_ref[0, kvh, 0] = jnp.concatenate([v, ones_rows], axis=0)

    rest = _bdot(xn, wrest_ref[...])

    k = rest[:, :KV_W]
    ss = k * k
    ss_hi = ss.astype(BF16)
    ss_lo = (ss - ss_hi.astype(F32)).astype(BF16)
    ms = _bdot(ss_hi, grp_ref[...]) + _bdot(ss_lo, grp_ref[...])
    k = k * lax.rsqrt(ms + EPS) * kg_ref[...]
    lane = lax.broadcasted_iota(jnp.int32, k.shape, 1)
    k_sw = jnp.where((lane % 32) < 16, pltpu.roll(k, LANES - 16, 1), pltpu.roll(k, 16, 1))
    k = k * cosn_ref[...] + k_sw * sinn_ref[...]
    tail = jnp.where(lane == HEAD_DIM, 1.0, 0.0)
    k_ref[0, 0] = jnp.where(lane < HEAD_DIM, k, tail).astype(BF16)
    k_ref[0, 1] = jnp.where(lane < HEAD_DIM, pltpu.roll(k, HEAD_DIM, 1), tail).astype(BF16)

    o = KV_W
    ql_ref[0] = rest[:, o:o + GLA_KEY_WIDTH]
    o += GLA_KEY_WIDTH
    kl_ref[0] = rest[:, o:o + GLA_KEY_WIDTH]
    o += GLA_KEY_WIDTH
    vl_ref[0] = rest[:, o:o + GLA_WIDTH].astype(BF16)
    o += GLA_WIDTH
    rl_ref[0] = rest[:, o:o + GLA_WIDTH]

    pre = _bdot(rest[:, GATE_OFF:].astype(BF16), wgate_ref[...]) + bgate_ref[...]
    logsig = jnp.minimum(pre, 0.0) - jnp.log(1.0 + jnp.exp(-jnp.abs(pre)))
    la = logsig * (1.0 / GATE_TAU)
    laf_ref[0] = la[:, :GLA_KEY_WIDTH]
    lab_ref[0] = la[:, GLA_KEY_WIDTH:]


def _ffn_inproj(x, p, tabs):
    b, s, _ = x.shape
    nt = s // TM
    c2 = lambda i, j: (0, 0)
    tok = lambda w: pl.BlockSpec((1, TM, w), lambda i, j: (i, j, 0))
    out_shape = (
        jax.ShapeDtypeStruct((b, s, D_MODEL), F32),
        jax.ShapeDtypeStruct((b, ATT_WIDTH, s), BF16),
        jax.ShapeDtypeStruct((b, N_KV_HEADS, s, LANES), BF16),
        jax.ShapeDtypeStruct((b, N_KV_HEADS, nt, V_ROWS, TM), BF16),
        jax.ShapeDtypeStruct((b, s, GLA_KEY_WIDTH), F32),
        jax.ShapeDtypeStruct((b, s, GLA_KEY_WIDTH), F32),
        jax.ShapeDtypeStruct((b, s, GLA_WIDTH), BF16),
        jax.ShapeDtypeStruct((b, s, GLA_WIDTH), F32),
        jax.ShapeDtypeStruct((b, s, GLA_KEY_WIDTH), F32),
        jax.ShapeDtypeStruct((b, s, GLA_KEY_WIDTH), F32),
    )
    out_specs = (
        tok(D_MODEL),
        pl.BlockSpec((1, ATT_WIDTH, TM), lambda i, j: (i, 0, j)),
        pl.BlockSpec((1, N_KV_HEADS, TM, LANES), lambda i, j: (i, 0, j, 0)),
        pl.BlockSpec((1, N_KV_HEADS, 1, V_ROWS, TM), lambda i, j: (i, 0, j, 0, 0)),
        tok(GLA_KEY_WIDTH), tok(GLA_KEY_WIDTH), tok(GLA_WIDTH), tok(GLA_WIDTH), tok(GLA_KEY_WIDTH),
        tok(GLA_KEY_WIDTH),
    )
    in_specs = [
        tok(D_MODEL),
        pl.BlockSpec((1, D_MODEL), c2),
        _resident((D_MODEL, 2 * D_FF)),
        _resident((D_FF, D_MODEL)),
        pl.BlockSpec((1, D_MODEL), c2),
        _resident((QV_ROWS, D_MODEL)),
        _resident((D_MODEL, REST_W)),
        pl.BlockSpec((HEAD_DIM, 1), c2),
        pl.BlockSpec((1, KV_W), c2),
        pl.BlockSpec((HEAD_DIM, TM), lambda i, j: (0, j)),
        pl.BlockSpec((HEAD_DIM, TM), lambda i, j: (0, j)),
        pl.BlockSpec((TM, KV_W), lambda i, j: (j, 0)),
        pl.BlockSpec((TM, KV_W), lambda i, j: (j, 0)),
        pl.BlockSpec((KV_W, KV_W), c2),
        pl.BlockSpec((LANES, 2 * GLA_KEY_WIDTH), c2),
        pl.BlockSpec((1, 2 * GLA_KEY_WIDTH), c2),
    ]
    return pl.pallas_call(
        _ffn_inproj_body,
        grid=(b, nt),
        in_specs=in_specs,
        out_specs=out_specs,
        out_shape=out_shape,
        compiler_params=_cparams(("parallel", "parallel")),
        name="ffn_inproj",
    )(x, p["norm_ffn1"], p["w_ffn1_gu"], p["w_ffn1_down"],
      p["norm_mix"], p["w_qv_t"], p["w_rest"], p["q_gain_col"], p["k_gain_row"],
      tabs["cos_t"], tabs["sin_t"], tabs["cos_n"], tabs["sin_n"], p["grp_avg"], p["w_gate"], p["b_gate"])


def _query_matrix(qt_ref, extra_row):
    qblk = qt_ref[0]
    top = jnp.concatenate([qblk[g * HEAD_DIM:(g + 1) * HEAD_DIM] for g in range(KV_GROUP)], axis=1)
    row = lax.broadcasted_iota(jnp.int32, top.shape, 0)
    bottom = jnp.where(row == 0, extra_row, 0.0).astype(BF16)
    return jnp.concatenate([top, bottom], axis=0)


def _write_heads(o_ref, acc, cols):
    out_t = acc[:HEAD_DIM] * (1.0 / acc[HEAD_DIM:HEAD_DIM + 1])
    heads = [out_t[:, sl].T for sl in cols]
    o_ref[0] = jnp.concatenate(heads, axis=1).astype(BF16)


def _attn_shifted_body(trips_ref, shift_ref, qt_ref, k_ref, vt_ref, o_ref, qcat_ref, acc_ref, *, n_chunks):
    tq = qt_ref.shape[2]
    cols = [slice(g * tq, (g + 1) * tq) for g in range(KV_GROUP)]
    qcat_ref[...] = _query_matrix(qt_ref, jnp.full((1, KV_GROUP * tq), -shift_ref[0], F32))
    acc_ref[...] = jnp.zeros(acc_ref.shape, F32)

    def scores(c, g):
        return _bdot(k_ref[0, 0, c * TM:(c + 1) * TM, :], qcat_ref[:, cols[g]])

    def absorb(c, g, s):
        acc_ref[:, cols[g]] += _bdot(vt_ref[0, 0, c], jnp.exp2(s).astype(BF16))

    units = [(c, g) for c in range(n_chunks) for g in range(KV_GROUP)]
    pending = [scores(*u) for u in units[:ATTN_LEAD]]
    for idx, (c, g) in enumerate(units):
        if idx + ATTN_LEAD < len(units):
            pending.append(scores(*units[idx + ATTN_LEAD]))
        absorb(c, g, pending.pop(0))
    _write_heads(o_ref, acc_ref[...], cols)


def _attn_online_body(trips_ref, shift_ref, qt_ref, k_ref, vt_ref, o_ref, qcat_ref, acc_ref, *bufs, n_chunks):
    tq = qt_ref.shape[2]
    cols = [slice(g * tq, (g + 1) * tq) for g in range(KV_GROUP)]
    qcat_ref[...] = _query_matrix(qt_ref, jnp.zeros((1, KV_GROUP * tq), F32))
    acc_ref[...] = jnp.zeros(acc_ref.shape, F32)

    def scores(c, dst_ref):
        kc = k_ref[0, 0, pl.ds(pl.multiple_of(c * TM, TM), TM), :]
        maxes = []
        for sl in cols:
            s = _bdot(kc, qcat_ref[:, sl])
            dst_ref[:, sl] = s
            maxes.append(jnp.max(s, axis=0, keepdims=True))
        return jnp.concatenate(maxes, axis=1)

    def absorb(c, src_ref, chunk_max, m_old):
        m_new = jnp.maximum(m_old, chunk_max)
        alpha = jnp.exp2(m_old - m_new)
        vt = vt_ref[0, 0, c]
        acc_ref[...] = alpha * acc_ref[...]
        for sl in cols:
            p = jnp.exp2(src_ref[:, sl] - m_new[:, sl]).astype(BF16)
            acc_ref[:, sl] += _bdot(vt, p)
        return m_new

    n_buf = len(bufs)

    def group(first, carry, last_group):
        m, chunk_max = carry
        for j in range(n_buf):
            c = first + j
            next_max = None if last_group and j + 1 == n_buf else scores(c + 1, bufs[(j + 1) % n_buf])
            m = absorb(c, bufs[j], chunk_max, m)
            chunk_max = next_max
        return m, chunk_max

    carry = (jnp.full((1, KV_GROUP * tq), NEG_BIG, F32), scores(0, bufs[0]))
    carry = lax.fori_loop(0, trips_ref[0], lambda i, cr: group(i * n_buf, cr, False), carry)
    group(n_chunks - n_buf, carry, True)
    _write_heads(o_ref, acc_ref[...], cols)


def _attn_call(body, n_score_buffers, score_bound, qt, k, vt):
    b, _, s = qt.shape
    n_chunks = s // TM
    assert n_chunks % SCORE_BUFFERS == 0
    tq = min(TQ if n_score_buffers else TQ_SHIFTED, s)
    mq = KV_GROUP * tq
    trips = jnp.full((1,), n_chunks // SCORE_BUFFERS - 1, jnp.int32)
    return pl.pallas_call(
        functools.partial(body, n_chunks=n_chunks),
        grid_spec=pltpu.PrefetchScalarGridSpec(
            num_scalar_prefetch=2,
            grid=(b, N_KV_HEADS, s // tq),
            in_specs=[
                pl.BlockSpec((1, KV_GROUP * HEAD_DIM, tq), lambda i, h, j, *_: (i, h, j)),
                pl.BlockSpec((1, 1, s, LANES), lambda i, h, j, *_: (i, h, 0, 0)),
                pl.BlockSpec((1, 1, n_chunks, V_ROWS, TM), lambda i, h, j, *_: (i, h, 0, 0, 0)),
            ],
            out_specs=pl.BlockSpec((1, tq, KV_GROUP * HEAD_DIM), lambda i, h, j, *_: (i, j, h)),
            scratch_shapes=[
                pltpu.VMEM((LANES, mq), BF16),
                pltpu.VMEM((V_ROWS, mq), F32),
            ] + [pltpu.VMEM((TM, mq + LANES), F32)] * n_score_buffers,
        ),
        out_shape=jax.ShapeDtypeStruct((b, s, ATT_WIDTH), BF16),
        compiler_params=_cparams(("parallel", "parallel", "arbitrary")),
        name="attn",
    )(trips, score_bound, qt, k, vt)


def _attn(qt, k, vt, score_bound):
    return lax.cond(score_bound[0] <= MAX_FIXED_SHIFT,
                    functools.partial(_attn_call, _attn_shifted_body, 0),
                    functools.partial(_attn_call, _attn_online_body, SCORE_BUFFERS),
                    score_bound, qt, k, vt)


def _gla_direction(q_ref, k_ref, v_ref, la_ref, o_ref, state_ref, order, cum, keep, ref_idx, last_idx):
    heads = range(GLA_HEADS)
    psl = [slice(p * LANES, (p + 1) * LANES) for p in range(GLA_HEADS * GLA_DK // LANES)]
    vsl = [slice(h * GLA_DV, (h + 1) * GLA_DV) for h in heads]
    rows = [slice(ci * CHUNK, (ci + 1) * CHUNK) for ci in order]
    n = range(len(rows))

    b = []
    for sl in rows:
        la = la_ref[0, sl, :]
        la_hi = la.astype(BF16)
        la_lo = (la - la_hi.astype(F32)).astype(BF16)
        b.append(_bdot(cum, la_hi) + _bdot(cum, la_lo))

    qr, kr, kd, qb, b_lasts, v = [], [], [], [], [], []
    for sl, bc in zip(rows, b):
        b_ref = bc[ref_idx:ref_idx + 1]
        b_last = bc[last_idx:last_idx + 1]
        qs = q_ref[0, sl, :] * (GLA_DK ** -0.5)
        k = k_ref[0, sl, :]
        qr.append((qs * jnp.exp(bc - b_ref)).astype(BF16))
        kr.append((k * jnp.exp(b_ref - bc)).astype(BF16))
        kd.append(k * jnp.exp(b_last - bc))
        qb.append((qs * jnp.exp(bc)).astype(BF16))
        b_lasts.append(b_last)
        v.append(v_ref[0, sl, :])

    pairs = range(len(psl))
    lane = lax.broadcasted_iota(jnp.int32, (1, LANES), 1)
    own = [lane < GLA_DK, lane >= GLA_DK]
    keep2 = jnp.concatenate([keep, keep], axis=0)
    rr = lax.broadcasted_iota(jnp.int32, (LANES, 2 * GLA_DV), 0)
    cc = lax.broadcasted_iota(jnp.int32, (LANES, 2 * GLA_DV), 1)
    same_head = (rr < GLA_DK) == (cc < GLA_DV)
    a = [[None] * GLA_HEADS for _ in n]
    d_state = [[None] * len(psl) for _ in n]
    dec_col = [[None] * len(psl) for _ in n]
    for i in n:
        for p, sl in enumerate(psl):
            qr_p = qr[i][:, sl]
            stacked = jnp.concatenate([jnp.where(own[0], qr_p, 0), jnp.where(own[1], qr_p, 0)], axis=0)
            a2 = jnp.where(keep2, _nt_dot(stacked, kr[i][:, sl]), 0.0).astype(BF16)
            a[i][2 * p] = a2[:CHUNK]
            a[i][2 * p + 1] = a2[CHUNK:]
            both = jnp.concatenate([kd[i][:, sl], jnp.broadcast_to(b_lasts[i][:, sl], (CHUNK, LANES))], axis=0).T
            v_pair = v[i][:, p * 2 * GLA_DV:(p + 1) * 2 * GLA_DV]
            d_state[i][p] = jnp.where(same_head, _bdot(both[:, :CHUNK].astype(BF16), v_pair), 0.0)
            dec_col[i][p] = jnp.exp(both[:, CHUNK:CHUNK + 1])
    o_intra = [[_bdot(a[i][h], v[i][:, vsl[h]]) for h in heads] for i in n]

    state = [state_ref[p] for p in pairs]
    entering = []
    for i in n:
        entering.append(state)
        state = [dec_col[i][p] * state[p] + d_state[i][p] for p in pairs]
    for p in pairs:
        state_ref[p] = state[p]

    o_inter = [[_bdot(qb[i][:, psl[p]], entering[i][p].astype(BF16)) for p in pairs] for i in n]
    for i in n:
        o_ref[0, rows[i], :] = jnp.concatenate(o_intra[i], axis=1) + jnp.concatenate(o_inter[i], axis=1)


def _gla_body(qf_ref, kf_ref, vf_ref, laf_ref, qb_ref, kb_ref, vb_ref, lab_ref, of_ref, ob_ref, sf_ref, sb_ref,
              *, n_chunks):
    @pl.when(pl.program_id(1) == 0)
    def _():
        sf_ref[...] = jnp.zeros(sf_ref.shape, F32)
        sb_ref[...] = jnp.zeros(sb_ref.shape, F32)

    r = lax.broadcasted_iota(jnp.int32, (CHUNK, CHUNK), 0)
    c = lax.broadcasted_iota(jnp.int32, (CHUNK, CHUNK), 1)
    _gla_direction(qf_ref, kf_ref, vf_ref, laf_ref, of_ref, sf_ref, range(n_chunks),
                   jnp.where(c <= r, 1.0, 0.0).astype(BF16), c <= r, CHUNK // 2, CHUNK - 1)
    _gla_direction(qb_ref, kb_ref, vb_ref, lab_ref, ob_ref, sb_ref, range(n_chunks - 1, -1, -1),
                   jnp.where(c >= r, 1.0, 0.0).astype(BF16), c > r, CHUNK // 2 - 1, 0)


def _gla(ql, kl, vl, laf, lab):
    b, s, _ = ql.shape
    tb = min(GLA_BLOCK, s)
    nb = s // tb
    fwd = lambda w: pl.BlockSpec((1, tb, w), lambda i, j: (i, j, 0))
    bwd = lambda w: pl.BlockSpec((1, tb, w), lambda i, j: (i, nb - 1 - j, 0))
    return pl.pallas_call(
        functools.partial(_gla_body, n_chunks=tb // CHUNK),
        grid=(b, nb),
        in_specs=[fwd(GLA_KEY_WIDTH), fwd(GLA_KEY_WIDTH), fwd(GLA_WIDTH), fwd(GLA_KEY_WIDTH),
                  bwd(GLA_KEY_WIDTH), bwd(GLA_KEY_WIDTH), bwd(GLA_WIDTH), bwd(GLA_KEY_WIDTH)],
        out_specs=(fwd(GLA_WIDTH), bwd(GLA_WIDTH)),
        out_shape=(jax.ShapeDtypeStruct((b, s, GLA_WIDTH), F32), jax.ShapeDtypeStruct((b, s, GLA_WIDTH), F32)),
        scratch_shapes=[pltpu.VMEM((GLA_HEADS // 2, LANES, 2 * GLA_DV), F32),
                        pltpu.VMEM((GLA_HEADS // 2, LANES, 2 * GLA_DV), F32)],
        compiler_params=_cparams(("parallel", "arbitrary")),
        name="gla",
    )(ql, kl, vl, laf, ql, kl, vl, lab)


def _outproj_ffn_body(h_ref, oa_ref, of_ref, ob_ref, r_ref, gg_ref, wa_ref, wg_ref, g2_ref, wgu_ref, wd_ref, og_ref,
                      o_ref):
    o = of_ref[...] + ob_ref[...]
    gain = gg_ref[...]
    r = r_ref[...]
    parts = []
    for hh in range(GLA_HEADS):
        sl = slice(hh * GLA_DV, (hh + 1) * GLA_DV)
        rh = r[:, sl]
        parts.append((_rms_rows(o[:, sl], gain[:, sl]) * (rh * (1.0 / (1.0 + jnp.exp(-rh))))).astype(BF16))
    og = jnp.concatenate(parts, axis=1)
    h = h_ref[...] + _bdot(oa_ref[...], wa_ref[...]) + _bdot(og, wg_ref[...])
    o_ref[...] = _rms_rows(_swiglu_half_step(h, g2_ref[...], wgu_ref, wd_ref), og_ref[...])


def _outproj_ffn(h, o_att, o_f, o_b, r, p):
    t = h.shape[0]
    const = lambda i: (0, 0)
    tok = lambda w: pl.BlockSpec((TM, w), lambda i: (i, 0))
    return pl.pallas_call(
        _outproj_ffn_body,
        grid=(t // TM,),
        in_specs=[tok(D_MODEL), tok(ATT_WIDTH), tok(GLA_WIDTH), tok(GLA_WIDTH), tok(GLA_WIDTH),
                  pl.BlockSpec((1, GLA_WIDTH), const),
                  _resident((ATT_WIDTH, D_MODEL)),
                  _resident((GLA_WIDTH, D_MODEL)),
                  pl.BlockSpec((1, D_MODEL), const),
                  _resident((D_MODEL, 2 * D_FF)),
                  _resident((D_FF, D_MODEL)),
                  pl.BlockSpec((1, D_MODEL), const)],
        out_specs=tok(D_MODEL),
        out_shape=jax.ShapeDtypeStruct((t, D_MODEL), F32),
        compiler_params=_cparams(("parallel",)),
        name="outproj_ffn",
    )(h, o_att, o_f, o_b, r, p["gla_gain"], p["w_out_att"], p["w_out_gla"],
      p["norm_ffn2"], p["w_ffn2_gu"], p["w_ffn2_down"], p["norm_out"])


def _rope_tables(seq_len):
    rows = seq_len // GRID_W
    row = jnp.repeat(jnp.arange(rows, dtype=F32), GRID_W)
    col = jnp.tile(jnp.arange(GRID_W, dtype=F32), rows)
    inv_freq = 1.0 / (ROPE_THETA ** (jnp.arange(0, AXIS_DIM, 2, dtype=F32) / AXIS_DIM))
    ang_r = row[:, None] * inv_freq[None, :]
    ang_c = col[:, None] * inv_freq[None, :]
    ang = jnp.concatenate([ang_r, ang_r, ang_c, ang_c], axis=-1)
    sign = jnp.where((jnp.arange(HEAD_DIM) % 32) < 16, -1.0, 1.0).astype(F32)
    cos, sin = jnp.cos(ang), jnp.sin(ang) * sign[None, :]
    return {
        "cos_t": cos.T, "sin_t": sin.T,
        "cos_n": jnp.tile(cos, (1, N_KV_HEADS)), "sin_n": jnp.tile(sin, (1, N_KV_HEADS)),
    }


def _prep_layer(l, norm_ffn1, w_ffn1_gu, w_ffn1_down, norm_mix, w_in, q_norm, k_norm, w_gate_f, b_gate_f,
                w_gate_b, b_gate_b, gla_norm, w_out, norm_ffn2, w_ffn2_gu, w_ffn2_down, norm_out):
    w = w_in[l]
    o_q, o_k, o_v = 0, ATT_WIDTH, ATT_WIDTH + KV_W
    o_rest = ATT_WIDTH + 2 * KV_W
    n_rest = 2 * GLA_KEY_WIDTH + 2 * GLA_WIDTH
    gates = jnp.pad(w[:, o_rest + n_rest:], ((0, 0), (0, LANES - 2 * GATE_RANK)))
    w_rest = jnp.concatenate([w[:, o_k:o_v], w[:, o_rest:o_rest + n_rest], gates], axis=1).astype(BF16)
    w_qv_t = jnp.concatenate([w[:, o_q:o_k], w[:, o_v:o_rest]], axis=1).T.astype(BF16)
    w_gate = jnp.zeros((LANES, 2 * GLA_KEY_WIDTH), F32)
    w_gate = w_gate.at[:GATE_RANK, :GLA_KEY_WIDTH].set(w_gate_f[l])
    w_gate = w_gate.at[GATE_RANK:2 * GATE_RANK, GLA_KEY_WIDTH:].set(w_gate_b[l])
    hid = jnp.arange(KV_W) // HEAD_DIM
    grp_avg = jnp.where(hid[:, None] == hid[None, :], 1.0 / HEAD_DIM, 0.0).astype(BF16)
    row = lambda v: v.reshape(1, -1).astype(F32)
    return {
        "norm_ffn1": row(norm_ffn1[l]), "w_ffn1_gu": w_ffn1_gu[l].astype(BF16), "w_ffn1_down": w_ffn1_down[l].astype(BF16),
        "norm_mix": row(norm_mix[l]), "w_qv_t": w_qv_t, "w_rest": w_rest,
        "q_gain_col": q_norm[l].reshape(HEAD_DIM, 1).astype(F32),
        "k_gain_row": row(jnp.tile(k_norm[l], N_KV_HEADS)),
        "grp_avg": grp_avg, "w_gate": w_gate.astype(BF16),
        "score_bound": (HEAD_DIM ** 0.5 * LOG2E * jnp.max(jnp.abs(q_norm[l])) * jnp.max(jnp.abs(k_norm[l]))
                        ).reshape(1).astype(F32),
        "b_gate": row(jnp.concatenate([b_gate_f[l], b_gate_b[l]])),
        "gla_gain": row(jnp.tile(gla_norm[l], GLA_HEADS)),
        "w_out_att": w_out[l][:ATT_WIDTH].astype(BF16), "w_out_gla": w_out[l][ATT_WIDTH:].astype(BF16),
        "norm_ffn2": row(norm_ffn2[l]), "w_ffn2_gu": w_ffn2_gu[l].astype(BF16), "w_ffn2_down": w_ffn2_down[l].astype(BF16),
        "norm_out": row(norm_out[l]),
    }


def _layer(x, p, tabs):
    b, s, d = x.shape
    t = b * s
    h, qt, k, vt, ql, kl, vl, rl, laf, lab = _ffn_inproj(x, p, tabs)
    o_att = _attn(qt, k, vt, p["score_bound"])
    o_f, o_b = _gla(ql, kl, vl, laf, lab)
    y = _outproj_ffn(h.reshape(t, d), o_att.reshape(t, ATT_WIDTH), o_f.reshape(t, GLA_WIDTH),
                     o_b.reshape(t, GLA_WIDTH), rl.reshape(t, GLA_WIDTH), p)
    return y.reshape(b, s, d)


def _trunk(x, layers):
    tabs = _rope_tables(x.shape[1])
    h = x
    for p in layers:
        h = _layer(h, p, tabs)
    return h


def kernel(x_prompt, x_sample, norm_ffn1, w_ffn1_gu, w_ffn1_down, norm_mix, w_in, q_norm, k_norm, w_gate_f, b_gate_f,
           w_gate_b, b_gate_b, gla_norm, w_out, norm_ffn2, w_ffn2_gu, w_ffn2_down, norm_out):
    params = (norm_ffn1, w_ffn1_gu, w_ffn1_down, norm_mix, w_in, q_norm, k_norm, w_gate_f, b_gate_f,
              w_gate_b, b_gate_b, gla_norm, w_out, norm_ffn2, w_ffn2_gu, w_ffn2_down, norm_out)
    layers = [_prep_layer(l, *params) for l in range(norm_ffn1.shape[0])]
    return (_trunk(x_prompt, layers), _trunk(x_sample, layers))
```

```python
import functools

import jax
import jax.numpy as jnp
from jax import lax
from jax.experimental import pallas as pl
from jax.experimental.pallas import tpu as pltpu

F32 = jnp.float32
BF16 = jnp.bfloat16

D_MODEL = 1024
GRID_W = 64
ATT_WIDTH = 512
HEAD_DIM = 64
N_HEADS = 8
N_KV_HEADS = 2
KV_GROUP = 4
AXIS_DIM = 32
ROPE_THETA = 10000.0
GLA_WIDTH = 512
GLA_HEADS = 4
GLA_DV = 128
GLA_DK = 64
GLA_KEY_WIDTH = 256
GATE_RANK = 16
GATE_TAU = 16.0
CHUNK = 64
D_FF = 2816
EPS = 1e-6

LANES = 128
KV_W = N_KV_HEADS * HEAD_DIM
QV_ROWS = ATT_WIDTH + KV_W
REST_W = KV_W + 2 * GLA_KEY_WIDTH + 2 * GLA_WIDTH + LANES
GATE_OFF = REST_W - LANES

TM = 512
TQ = 256
TILES_PER_STEP = 4
ATTN_LEAD = 2
MAX_FIXED_SHIFT = 48.0
SCORE_BUFFERS = 4
FF_CHUNK = 256
N_FF_CHUNKS = D_FF // FF_CHUNK
GLA_BLOCK = 512
NEG_BIG = -1e30
LOG2E = 1.4426950408889634
V_ROWS = 2 * HEAD_DIM
VMEM_LIMIT = 56 * 1024 * 1024


def _cparams(sem):
    return pltpu.CompilerParams(dimension_semantics=sem, vmem_limit_bytes=VMEM_LIMIT)


def _rms_rows(x, gain_row):
    ms = jnp.mean(x * x, axis=-1, keepdims=True)
    return x * lax.rsqrt(ms + EPS) * gain_row


def _bdot(a, b):
    return jnp.dot(a, b, preferred_element_type=F32)


def _nt_dot(a, b):
    return lax.dot_general(a, b, (((1,), (1,)), ((), ())), preferred_element_type=F32)


def _swiglu_half_step(x, gain_row, wgu_ref, wd_ref):
    xn = _rms_rows(x, gain_row).astype(BF16)
    acc = None
    for c in range(N_FF_CHUNKS):
        g = _bdot(xn, wgu_ref[:, c * FF_CHUNK:(c + 1) * FF_CHUNK])
        u = _bdot(xn, wgu_ref[:, D_FF + c * FF_CHUNK:D_FF + (c + 1) * FF_CHUNK])
        a = (g * (1.0 / (1.0 + jnp.exp(-g))) * u).astype(BF16)
        d = _bdot(a, wd_ref[c * FF_CHUNK:(c + 1) * FF_CHUNK, :])
        acc = d if acc is None else acc + d
    return x + 0.5 * acc


def _resident(shape):
    return pl.BlockSpec(shape, lambda *_: (0,) * len(shape), pipeline_mode=pl.Buffered(1))


def _swap16(t, axis):
    n = t.shape[axis] // 16
    parts = [lax.slice_in_dim(t, (i ^ 1) * 16, (i ^ 1) * 16 + 16, axis=axis) for i in range(n)]
    return jnp.concatenate(parts, axis=axis)


def _ffn_inproj_body(x_ref, g1_ref, wgu_ref, wd_ref, g_ref, wqv_ref, wrest_ref, qg_ref, kg_ref,
                     cost_ref, sint_ref, cosn_ref, sinn_ref, grp_ref, wgate_ref, bgate_ref,
                     h_ref, qt_ref, k_ref, vt_ref, ql_ref, kl_ref, vl_ref, rl_ref, laf_ref, lab_ref):
    h = _swiglu_half_step(x_ref[0], g1_ref[...], wgu_ref, wd_ref)
    h_ref[0] = h
    xn = _rms_rows(h, g_ref[...]).astype(BF16)
    tm = xn.shape[0]

    ut = _nt_dot(wqv_ref[...], xn)
    q = ut[:ATT_WIDTH].reshape(N_HEADS, HEAD_DIM, tm)
    q = q * lax.rsqrt(jnp.mean(q * q, axis=1, keepdims=True) + EPS) * qg_ref[...][None]
    q = q * cost_ref[...][None] + _swap16(q, 1) * sint_ref[...][None]
    qt_ref[0] = (q * (HEAD_DIM ** -0.5 * LOG2E)).reshape(ATT_WIDTH, tm).astype(BF16)

    row = lax.broadcasted_iota(jnp.int32, (V_ROWS - HEAD_DIM, tm), 0)
    ones_rows = jnp.where(row == 0, 1.0, 0.0).astype(BF16)
    for kvh in range(N_KV_HEADS):
        v = ut[ATT_WIDTH + kvh * HEAD_DIM:ATT_WIDTH + (kvh + 1) * HEAD_DIM].astype(BF16)
        vt_ref[0, kvh, 0] = jnp.concatenate([v, ones_rows], axis=0)

    rest = _bdot(xn, wrest_ref[...])

    k = rest[:, :KV_W]
    ss = k * k
    ss_hi = ss.astype(BF16)
    ss_lo = (ss - ss_hi.astype(F32)).astype(BF16)
    ms = _bdot(ss_hi, grp_ref[...]) + _bdot(ss_lo, grp_ref[...])
    k = k * lax.rsqrt(ms + EPS) * kg_ref[...]
    lane = lax.broadcasted_iota(jnp.int32, k.shape, 1)
    k_sw = jnp.where((lane % 32) < 16, pltpu.roll(k, LANES - 16, 1), pltpu.roll(k, 16, 1))
    k = k * cosn_ref[...] + k_sw * sinn_ref[...]
    tail = jnp.where(lane == HEAD_DIM, 1.0, 0.0)
    k_ref[0, 0] = jnp.where(lane < HEAD_DIM, k, tail).astype(BF16)
    k_ref[0, 1] = jnp.where(lane < HEAD_DIM, pltpu.roll(k, HEAD_DIM, 1), tail).astype(BF16)

    o = KV_W
    ql_ref[0] = rest[:, o:o + GLA_KEY_WIDTH]
    o += GLA_KEY_WIDTH
    kl_ref[0] = rest[:, o:o + GLA_KEY_WIDTH]
    o += GLA_KEY_WIDTH
    vl_ref[0] = rest[:, o:o + GLA_WIDTH].astype(BF16)
    o += GLA_WIDTH
    rl_ref[0] = rest[:, o:o + GLA_WIDTH]

    pre = _bdot(rest[:, GATE_OFF:].astype(BF16), wgate_ref[...]) + bgate_ref[...]
    logsig = jnp.minimum(pre, 0.0) - jnp.log(1.0 + jnp.exp(-jnp.abs(pre)))
    la = logsig * (1.0 / GATE_TAU)
    laf_ref[0] = la[:, :GLA_KEY_WIDTH]
    lab_ref[0] = la[:, GLA_KEY_WIDTH:]


def _ffn_inproj(x, p, tabs):
    b, s, _ = x.shape
    nt = s // TM
    c2 = lambda i, j: (0, 0)
    tok = lambda w: pl.BlockSpec((1, TM, w), lambda i, j: (i, j, 0))
    out_shape = (
        jax.ShapeDtypeStruct((b, s, D_MODEL), F32),
        jax.ShapeDtypeStruct((b, ATT_WIDTH, s), BF16),
        jax.ShapeDtypeStruct((b, N_KV_HEADS, s, LANES), BF16),
        jax.ShapeDtypeStruct((b, N_KV_HEADS, nt, V_ROWS, TM), BF16),
        jax.ShapeDtypeStruct((b, s, GLA_KEY_WIDTH), F32),
        jax.ShapeDtypeStruct((b, s, GLA_KEY_WIDTH), F32),
        jax.ShapeDtypeStruct((b, s, GLA_WIDTH), BF16),
        jax.ShapeDtypeStruct((b, s, GLA_WIDTH), F32),
        jax.ShapeDtypeStruct((b, s, GLA_KEY_WIDTH), F32),
        jax.ShapeDtypeStruct((b, s, GLA_KEY_WIDTH), F32),
    )
    out_specs = (
        tok(D_MODEL),
        pl.BlockSpec((1, ATT_WIDTH, TM), lambda i, j: (i, 0, j)),
        pl.BlockSpec((1, N_KV_HEADS, TM, LANES), lambda i, j: (i, 0, j, 0)),
        pl.BlockSpec((1, N_KV_HEADS, 1, V_ROWS, TM), lambda i, j: (i, 0, j, 0, 0)),
        tok(GLA_KEY_WIDTH), tok(GLA_KEY_WIDTH), tok(GLA_WIDTH), tok(GLA_WIDTH), tok(GLA_KEY_WIDTH),
        tok(GLA_KEY_WIDTH),
    )
    in_specs = [
        tok(D_MODEL),
        pl.BlockSpec((1, D_MODEL), c2),
        _resident((D_MODEL, 2 * D_FF)),
        _resident((D_FF, D_MODEL)),
        pl.BlockSpec((1, D_MODEL), c2),
        _resident((QV_ROWS, D_MODEL)),
        _resident((D_MODEL, REST_W)),
        pl.BlockSpec((HEAD_DIM, 1), c2),
        pl.BlockSpec((1, KV_W), c2),
        pl.BlockSpec((HEAD_DIM, TM), lambda i, j: (0, j)),
        pl.BlockSpec((HEAD_DIM, TM), lambda i, j: (0, j)),
        pl.BlockSpec((TM, KV_W), lambda i, j: (j, 0)),
        pl.BlockSpec((TM, KV_W), lambda i, j: (j, 0)),
        pl.BlockSpec((KV_W, KV_W), c2),
        pl.BlockSpec((LANES, 2 * GLA_KEY_WIDTH), c2),
        pl.BlockSpec((1, 2 * GLA_KEY_WIDTH), c2),
    ]
    return pl.pallas_call(
        _ffn_inproj_body,
        grid=(b, nt),
        in_specs=in_specs,
        out_specs=out_specs,
        out_shape=out_shape,
        compiler_params=_cparams(("parallel", "parallel")),
        name="ffn_inproj",
    )(x, p["norm_ffn1"], p["w_ffn1_gu"], p["w_ffn1_down"],
      p["norm_mix"], p["w_qv_t"], p["w_rest"], p["q_gain_col"], p["k_gain_row"],
      tabs["cos_t"], tabs["sin_t"], tabs["cos_n"], tabs["sin_n"], p["grp_avg"], p["w_gate"], p["b_gate"])


def _query_matrix(qblk, extra_row):
    top = jnp.concatenate([qblk[g * HEAD_DIM:(g + 1) * HEAD_DIM] for g in range(KV_GROUP)], axis=1)
    row = lax.broadcasted_iota(jnp.int32, top.shape, 0)
    bottom = jnp.where(row == 0, extra_row, 0.0).astype(BF16)
    return jnp.concatenate([top, bottom], axis=0)


def _head_outputs(acc, cols):
    out_t = acc[:HEAD_DIM] * (1.0 / acc[HEAD_DIM:HEAD_DIM + 1])
    heads = [out_t[:, sl].T for sl in cols]
    return jnp.concatenate(heads, axis=1).astype(BF16)


def _attn_shifted_body(trips_ref, shift_ref, qt_ref, k_ref, vt_ref, o_ref, qcat_ref, acc_ref, *, n_chunks):
    n_tiles = qcat_ref.shape[0]
    tq = qt_ref.shape[2] // n_tiles
    cols = [slice(g * tq, (g + 1) * tq) for g in range(KV_GROUP)]
    shift_row = jnp.full((1, KV_GROUP * tq), -shift_ref[0], F32)
    for t in range(n_tiles):
        qcat_ref[t] = _query_matrix(qt_ref[0, :, t * tq:(t + 1) * tq], shift_row)
    acc_ref[...] = jnp.zeros(acc_ref.shape, F32)

    def scores(t, c, g):
        return _bdot(k_ref[0, 0, c * TM:(c + 1) * TM, :], qcat_ref[t, :, cols[g]])

    def absorb(t, c, g, s):
        acc_ref[t, :, cols[g]] += _bdot(vt_ref[0, 0, c], jnp.exp2(s).astype(BF16))

    units = [(t, c, g) for t in range(n_tiles) for c in range(n_chunks) for g in range(KV_GROUP)]
    pending = [scores(*u) for u in units[:ATTN_LEAD]]
    for idx, (t, c, g) in enumerate(units):
        if idx + ATTN_LEAD < len(units):
            pending.append(scores(*units[idx + ATTN_LEAD]))
        absorb(t, c, g, pending.pop(0))
        if c == n_chunks - 1 and g == KV_GROUP - 1:
            o_ref[0, t * tq:(t + 1) * tq, :] = _head_outputs(acc_ref[t], cols)


def _attn_online_body(trips_ref, shift_ref, qt_ref, k_ref, vt_ref, o_ref, qcat_ref, acc_ref, *bufs, n_chunks):
    tq = qt_ref.shape[2]
    cols = [slice(g * tq, (g + 1) * tq) for g in range(KV_GROUP)]
    qcat_ref[...] = _query_matrix(qt_ref[0], jnp.zeros((1, KV_GROUP * tq), F32))
    acc_ref[...] = jnp.zeros(acc_ref.shape, F32)

    def scores(c, dst_ref):
        kc = k_ref[0, 0, pl.ds(pl.multiple_of(c * TM, TM), TM), :]
        maxes = []
        for sl in cols:
            s = _bdot(kc, qcat_ref[:, sl])
            dst_ref[:, sl] = s
            maxes.append(jnp.max(s, axis=0, keepdims=True))
        return jnp.concatenate(maxes, axis=1)

    def absorb(c, src_ref, chunk_max, m_old):
        m_new = jnp.maximum(m_old, chunk_max)
        alpha = jnp.exp2(m_old - m_new)
        vt = vt_ref[0, 0, c]
        acc_ref[...] = alpha * acc_ref[...]
        for sl in cols:
            p = jnp.exp2(src_ref[:, sl] - m_new[:, sl]).astype(BF16)
            acc_ref[:, sl] += _bdot(vt, p)
        return m_new

    n_buf = len(bufs)

    def group(first, carry, last_group):
        m, chunk_max = carry
        for j in range(n_buf):
            c = first + j
            next_max = None if last_group and j + 1 == n_buf else scores(c + 1, bufs[(j + 1) % n_buf])
            m = absorb(c, bufs[j], chunk_max, m)
            chunk_max = next_max
        return m, chunk_max

    carry = (jnp.full((1, KV_GROUP * tq), NEG_BIG, F32), scores(0, bufs[0]))
    carry = lax.fori_loop(0, trips_ref[0], lambda i, cr: group(i * n_buf, cr, False), carry)
    group(n_chunks - n_buf, carry, True)
    o_ref[0] = _head_outputs(acc_ref[...], cols)


def _attn_call(body, tiles_per_step, n_score_buffers, score_bound, qt, k, vt):
    b, _, s = qt.shape
    n_chunks = s // TM
    assert n_chunks % SCORE_BUFFERS == 0
    tq = min(TQ, s)
    mq = KV_GROUP * tq
    step = tiles_per_step * tq
    assert s % step == 0
    lead = (tiles_per_step,) if tiles_per_step > 1 else ()
    trips = jnp.full((1,), n_chunks // SCORE_BUFFERS - 1, jnp.int32)
    return pl.pallas_call(
        functools.partial(body, n_chunks=n_chunks),
        grid_spec=pltpu.PrefetchScalarGridSpec(
            num_scalar_prefetch=2,
            grid=(b, N_KV_HEADS, s // step),
            in_specs=[
                pl.BlockSpec((1, KV_GROUP * HEAD_DIM, step), lambda i, h, j, *_: (i, h, j)),
                pl.BlockSpec((1, 1, s, LANES), lambda i, h, j, *_: (i, h, 0, 0)),
                pl.BlockSpec((1, 1, n_chunks, V_ROWS, TM), lambda i, h, j, *_: (i, h, 0, 0, 0)),
            ],
            out_specs=pl.BlockSpec((1, step, KV_GROUP * HEAD_DIM), lambda i, h, j, *_: (i, j, h)),
            scratch_shapes=[
                pltpu.VMEM(lead + (LANES, mq), BF16),
                pltpu.VMEM(lead + (V_ROWS, mq), F32),
            ] + [pltpu.VMEM((TM, mq + LANES), F32)] * n_score_buffers,
        ),
        out_shape=jax.ShapeDtypeStruct((b, s, ATT_WIDTH), BF16),
        compiler_params=_cparams(("parallel", "parallel", "arbitrary")),
        name="attn",
    )(trips, score_bound, qt, k, vt)


def _attn(qt, k, vt, score_bound):
    return lax.cond(score_bound[0] <= MAX_FIXED_SHIFT,
                    functools.partial(_attn_call, _attn_shifted_body, TILES_PER_STEP, 0),
                    functools.partial(_attn_call, _attn_online_body, 1, SCORE_BUFFERS),
                    score_bound, qt, k, vt)


def _gla_direction(q_ref, k_ref, v_ref, la_ref, o_ref, state_ref, order, cum, keep, ref_idx, last_idx):
    heads = range(GLA_HEADS)
    psl = [slice(p * LANES, (p + 1) * LANES) for p in range(GLA_HEADS * GLA_DK // LANES)]
    vsl = [slice(h * GLA_DV, (h + 1) * GLA_DV) for h in heads]
    rows = [slice(ci * CHUNK, (ci + 1) * CHUNK) for ci in order]
    n = range(len(rows))

    b = []
    for sl in rows:
        la = la_ref[0, sl, :]
        la_hi = la.astype(BF16)
        la_lo = (la - la_hi.astype(F32)).astype(BF16)
        b.append(_bdot(cum, la_hi) + _bdot(cum, la_lo))

    qr, kr, kd, qb, b_lasts, v = [], [], [], [], [], []
    for sl, bc in zip(rows, b):
        b_ref = bc[ref_idx:ref_idx + 1]
        b_last = bc[last_idx:last_idx + 1]
        qs = q_ref[0, sl, :] * (GLA_DK ** -0.5)
        k = k_ref[0, sl, :]
        qr.append((qs * jnp.exp(bc - b_ref)).astype(BF16))
        kr.append((k * jnp.exp(b_ref - bc)).astype(BF16))
        kd.append(k * jnp.exp(b_last - bc))
        qb.append((qs * jnp.exp(bc)).astype(BF16))
        b_lasts.append(b_last)
        v.append(v_ref[0, sl, :])

    pairs = range(len(psl))
    lane = lax.broadcasted_iota(jnp.int32, (1, LANES), 1)
    own = [lane < GLA_DK, lane >= GLA_DK]
    keep2 = jnp.concatenate([keep, keep], axis=0)
    rr = lax.broadcasted_iota(jnp.int32, (LANES, 2 * GLA_DV), 0)
    cc = lax.broadcasted_iota(jnp.int32, (LANES, 2 * GLA_DV), 1)
    same_head = (rr < GLA_DK) == (cc < GLA_DV)
    a = [[None] * GLA_HEADS for _ in n]
    d_state = [[None] * len(psl) for _ in n]
    dec_col = [[None] * len(psl) for _ in n]
    for i in n:
        for p, sl in enumerate(psl):
            qr_p = qr[i][:, sl]
            stacked = jnp.concatenate([jnp.where(own[0], qr_p, 0), jnp.where(own[1], qr_p, 0)], axis=0)
            a2 = jnp.where(keep2, _nt_dot(stacked, kr[i][:, sl]), 0.0).astype(BF16)
            a[i][2 * p] = a2[:CHUNK]
            a[i][2 * p + 1] = a2[CHUNK:]
            both = jnp.concatenate([kd[i][:, sl], jnp.broadcast_to(b_lasts[i][:, sl], (CHUNK, LANES))], axis=0).T
            v_pair = v[i][:, p * 2 * GLA_DV:(p + 1) * 2 * GLA_DV]
            d_state[i][p] = jnp.where(same_head, _bdot(both[:, :CHUNK].astype(BF16), v_pair), 0.0)
            dec_col[i][p] = jnp.exp(both[:, CHUNK:CHUNK + 1])
    o_intra = [[_bdot(a[i][h], v[i][:, vsl[h]]) for h in heads] for i in n]

    state = [state_ref[p] for p in pairs]
    entering = []
    for i in n:
        entering.append(state)
        state = [dec_col[i][p] * state[p] + d_state[i][p] for p in pairs]
    for p in pairs:
        state_ref[p] = state[p]

    o_inter = [[_bdot(qb[i][:, psl[p]], entering[i][p].astype(BF16)) for p in pairs] for i in n]
    for i in n:
        o_ref[0, rows[i], :] = jnp.concatenate(o_intra[i], axis=1) + jnp.concatenate(o_inter[i], axis=1)


def _gla_body(qf_ref, kf_ref, vf_ref, laf_ref, qb_ref, kb_ref, vb_ref, lab_ref, of_ref, ob_ref, sf_ref, sb_ref,
              *, n_chunks):
    @pl.when(pl.program_id(1) == 0)
    def _():
        sf_ref[...] = jnp.zeros(sf_ref.shape, F32)
        sb_ref[...] = jnp.zeros(sb_ref.shape, F32)

    r = lax.broadcasted_iota(jnp.int32, (CHUNK, CHUNK), 0)
    c = lax.broadcasted_iota(jnp.int32, (CHUNK, CHUNK), 1)
    _gla_direction(qf_ref, kf_ref, vf_ref, laf_ref, of_ref, sf_ref, range(n_chunks),
                   jnp.where(c <= r, 1.0, 0.0).astype(BF16), c <= r, CHUNK // 2, CHUNK - 1)
    _gla_direction(qb_ref, kb_ref, vb_ref, lab_ref, ob_ref, sb_ref, range(n_chunks - 1, -1, -1),
                   jnp.where(c >= r, 1.0, 0.0).astype(BF16), c > r, CHUNK // 2 - 1, 0)


def _gla(ql, kl, vl, laf, lab):
    b, s, _ = ql.shape
    tb = min(GLA_BLOCK, s)
    nb = s // tb
    fwd = lambda w: pl.BlockSpec((1, tb, w), lambda i, j: (i, j, 0))
    bwd = lambda w: pl.BlockSpec((1, tb, w), lambda i, j: (i, nb - 1 - j, 0))
    return pl.pallas_call(
        functools.partial(_gla_body, n_chunks=tb // CHUNK),
        grid=(b, nb),
        in_specs=[fwd(GLA_KEY_WIDTH), fwd(GLA_KEY_WIDTH), fwd(GLA_WIDTH), fwd(GLA_KEY_WIDTH),
                  bwd(GLA_KEY_WIDTH), bwd(GLA_KEY_WIDTH), bwd(GLA_WIDTH), bwd(GLA_KEY_WIDTH)],
        out_specs=(fwd(GLA_WIDTH), bwd(GLA_WIDTH)),
        out_shape=(jax.ShapeDtypeStruct((b, s, GLA_WIDTH), F32), jax.ShapeDtypeStruct((b, s, GLA_WIDTH), F32)),
        scratch_shapes=[pltpu.VMEM((GLA_HEADS // 2, LANES, 2 * GLA_DV), F32),
                        pltpu.VMEM((GLA_HEADS // 2, LANES, 2 * GLA_DV), F32)],
        compiler_params=_cparams(("parallel", "arbitrary")),
        name="gla",
    )(ql, kl, vl, laf, ql, kl, vl, lab)


def _outproj_ffn_body(h_ref, oa_ref, of_ref, ob_ref, r_ref, gg_ref, wa_ref, wg_ref, g2_ref, wgu_ref, wd_ref, og_ref,
                      o_ref):
    o = of_ref[...] + ob_ref[...]
    gain = gg_ref[...]
    r = r_ref[...]
    parts = []
    for hh in range(GLA_HEADS):
        sl = slice(hh * GLA_DV, (hh + 1) * GLA_DV)
        rh = r[:, sl]
        parts.append((_rms_rows(o[:, sl], gain[:, sl]) * (rh * (1.0 / (1.0 + jnp.exp(-rh))))).astype(BF16))
    og = jnp.concatenate(parts, axis=1)
    h = h_ref[...] + _bdot(oa_ref[...], wa_ref[...]) + _bdot(og, wg_ref[...])
    o_ref[...] = _rms_rows(_swiglu_half_step(h, g2_ref[...], wgu_ref, wd_ref), og_ref[...])


def _outproj_ffn(h, o_att, o_f, o_b, r, p):
    t = h.shape[0]
    const = lambda i: (0, 0)
    tok = lambda w: pl.BlockSpec((TM, w), lambda i: (i, 0))
    return pl.pallas_call(
        _outproj_ffn_body,
        grid=(t // TM,),
        in_specs=[tok(D_MODEL), tok(ATT_WIDTH), tok(GLA_WIDTH), tok(GLA_WIDTH), tok(GLA_WIDTH),
                  pl.BlockSpec((1, GLA_WIDTH), const),
                  _resident((ATT_WIDTH, D_MODEL)),
                  _resident((GLA_WIDTH, D_MODEL)),
                  pl.BlockSpec((1, D_MODEL), const),
                  _resident((D_MODEL, 2 * D_FF)),
                  _resident((D_FF, D_MODEL)),
                  pl.BlockSpec((1, D_MODEL), const)],
        out_specs=tok(D_MODEL),
        out_shape=jax.ShapeDtypeStruct((t, D_MODEL), F32),
        compiler_params=_cparams(("parallel",)),
        name="outproj_ffn",
    )(h, o_att, o_f, o_b, r, p["gla_gain"], p["w_out_att"], p["w_out_gla"],
      p["norm_ffn2"], p["w_ffn2_gu"], p["w_ffn2_down"], p["norm_out"])


def _rope_tables(seq_len):
    rows = seq_len // GRID_W
    row = jnp.repeat(jnp.arange(rows, dtype=F32), GRID_W)
    col = jnp.tile(jnp.arange(GRID_W, dtype=F32), rows)
    inv_freq = 1.0 / (ROPE_THETA ** (jnp.arange(0, AXIS_DIM, 2, dtype=F32) / AXIS_DIM))
    ang_r = row[:, None] * inv_freq[None, :]
    ang_c = col[:, None] * inv_freq[None, :]
    ang = jnp.concatenate([ang_r, ang_r, ang_c, ang_c], axis=-1)
    sign = jnp.where((jnp.arange(HEAD_DIM) % 32) < 16, -1.0, 1.0).astype(F32)
    cos, sin = jnp.cos(ang), jnp.sin(ang) * sign[None, :]
    return {
        "cos_t": cos.T, "sin_t": sin.T,
        "cos_n": jnp.tile(cos, (1, N_KV_HEADS)), "sin_n": jnp.tile(sin, (1, N_KV_HEADS)),
    }


def _prep_layer(l, norm_ffn1, w_ffn1_gu, w_ffn1_down, norm_mix, w_in, q_norm, k_norm, w_gate_f, b_gate_f,
                w_gate_b, b_gate_b, gla_norm, w_out, norm_ffn2, w_ffn2_gu, w_ffn2_down, norm_out):
    w = w_in[l]
    o_q, o_k, o_v = 0, ATT_WIDTH, ATT_WIDTH + KV_W
    o_rest = ATT_WIDTH + 2 * KV_W
    n_rest = 2 * GLA_KEY_WIDTH + 2 * GLA_WIDTH
    gates = jnp.pad(w[:, o_rest + n_rest:], ((0, 0), (0, LANES - 2 * GATE_RANK)))
    w_rest = jnp.concatenate([w[:, o_k:o_v], w[:, o_rest:o_rest + n_rest], gates], axis=1).astype(BF16)
    w_qv_t = jnp.concatenate([w[:, o_q:o_k], w[:, o_v:o_rest]], axis=1).T.astype(BF16)
    w_gate = jnp.zeros((LANES, 2 * GLA_KEY_WIDTH), F32)
    w_gate = w_gate.at[:GATE_RANK, :GLA_KEY_WIDTH].set(w_gate_f[l])
    w_gate = w_gate.at[GATE_RANK:2 * GATE_RANK, GLA_KEY_WIDTH:].set(w_gate_b[l])
    hid = jnp.arange(KV_W) // HEAD_DIM
    grp_avg = jnp.where(hid[:, None] == hid[None, :], 1.0 / HEAD_DIM, 0.0).astype(BF16)
    row = lambda v: v.reshape(1, -1).astype(F32)
    return {
        "norm_ffn1": row(norm_ffn1[l]), "w_ffn1_gu": w_ffn1_gu[l].astype(BF16), "w_ffn1_down": w_ffn1_down[l].astype(BF16),
        "norm_mix": row(norm_mix[l]), "w_qv_t": w_qv_t, "w_rest": w_rest,
        "q_gain_col": q_norm[l].reshape(HEAD_DIM, 1).astype(F32),
        "k_gain_row": row(jnp.tile(k_norm[l], N_KV_HEADS)),
        "grp_avg": grp_avg, "w_gate": w_gate.astype(BF16),
        "score_bound": (HEAD_DIM ** 0.5 * LOG2E * jnp.max(jnp.abs(q_norm[l])) * jnp.max(jnp.abs(k_norm[l]))
                        ).reshape(1).astype(F32),
        "b_gate": row(jnp.concatenate([b_gate_f[l], b_gate_b[l]])),
        "gla_gain": row(jnp.tile(gla_norm[l], GLA_HEADS)),
        "w_out_att": w_out[l][:ATT_WIDTH].astype(BF16), "w_out_gla": w_out[l][ATT_WIDTH:].astype(BF16),
        "norm_ffn2": row(norm_ffn2[l]), "w_ffn2_gu": w_ffn2_gu[l].astype(BF16), "w_ffn2_down": w_ffn2_down[l].astype(BF16),
        "norm_out": row(norm_out[l]),
    }


def _layer(x, p, tabs):
    b, s, d = x.shape
    t = b * s
    h, qt, k, vt, ql, kl, vl, rl, laf, lab = _ffn_inproj(x, p, tabs)
    o_att = _attn(qt, k, vt, p["score_bound"])
    o_f, o_b = _gla(ql, kl, vl, laf, lab)
    y = _outproj_ffn(h.reshape(t, d), o_att.reshape(t, ATT_WIDTH), o_f.reshape(t, GLA_WIDTH),
                     o_b.reshape(t, GLA_WIDTH), rl.reshape(t, GLA_WIDTH), p)
    return y.reshape(b, s, d)


def _trunk(x, layers):
    tabs = _rope_tables(x.shape[1])
    h = x
    for p in layers:
        h = _layer(h, p, tabs)
    return h


def kernel(x_prompt, x_sample, norm_ffn1, w_ffn1_gu, w_ffn1_down, norm_mix, w_in, q_norm, k_norm, w_gate_f, b_gate_f,
           w_gate_b, b_gate_b, gla_norm, w_out, norm_ffn2, w_ffn2_gu, w_ffn2_down, norm_out):
    params = (norm_ffn1, w_ffn1_gu, w_ffn1_down, norm_mix, w_in, q_norm, k_norm, w_gate_f, b_gate_f,
              w_gate_b, b_gate_b, gla_norm, w_out, norm_ffn2, w_ffn2_gu, w_ffn2_down, norm_out)
    layers = [_prep_layer(l, *params) for l in range(norm_ffn1.shape[0])]
    return (_trunk(x_prompt, layers), _trunk(x_sample, layers))
```

```python
import functools

import jax
import jax.numpy as jnp
from jax import lax
from jax.experimental import pallas as pl
from jax.experimental.pallas import tpu as pltpu

F32 = jnp.float32
BF16 = jnp.bfloat16

D_MODEL = 1024
GRID_W = 64
ATT_WIDTH = 512
HEAD_DIM = 64
N_HEADS = 8
N_KV_HEADS = 2
KV_GROUP = 4
AXIS_DIM = 32
ROPE_THETA = 10000.0
GLA_WIDTH = 512
GLA_HEADS = 4
GLA_DV = 128
GLA_DK = 64
GLA_KEY_WIDTH = 256
GATE_RANK = 16
GATE_TAU = 16.0
CHUNK = 64
D_FF = 2816
EPS = 1e-6

LANES = 128
KV_W = N_KV_HEADS * HEAD_DIM
QV_ROWS = ATT_WIDTH + KV_W
REST_W = KV_W + 2 * GLA_KEY_WIDTH + 2 * GLA_WIDTH + LANES
GATE_OFF = REST_W - LANES

TM = 512
TQ = 256
TILES_PER_STEP = 8
ATTN_LEAD = 2
MAX_FIXED_SHIFT = 48.0
SCORE_BUFFERS = 4
FF_CHUNK = 256
N_FF_CHUNKS = D_FF // FF_CHUNK
GLA_BLOCK = 512
NEG_BIG = -1e30
LOG2E = 1.4426950408889634
V_ROWS = 2 * HEAD_DIM
VMEM_LIMIT = 56 * 1024 * 1024


def _cparams(sem):
    return pltpu.CompilerParams(dimension_semantics=sem, vmem_limit_bytes=VMEM_LIMIT)


def _rms_rows(x, gain_row):
    ms = jnp.mean(x * x, axis=-1, keepdims=True)
    return x * lax.rsqrt(ms + EPS) * gain_row


def _bdot(a, b):
    return jnp.dot(a, b, preferred_element_type=F32)


def _nt_dot(a, b):
    return lax.dot_general(a, b, (((1,), (1,)), ((), ())), preferred_element_type=F32)


def _swiglu_half_step(x, gain_row, wgu_ref, wd_ref):
    xn = _rms_rows(x, gain_row).astype(BF16)
    acc = None
    for c in range(N_FF_CHUNKS):
        g = _bdot(xn, wgu_ref[:, c * FF_CHUNK:(c + 1) * FF_CHUNK])
        u = _bdot(xn, wgu_ref[:, D_FF + c * FF_CHUNK:D_FF + (c + 1) * FF_CHUNK])
        a = (g * (1.0 / (1.0 + jnp.exp(-g))) * u).astype(BF16)
        d = _bdot(a, wd_ref[c * FF_CHUNK:(c + 1) * FF_CHUNK, :])
        acc = d if acc is None else acc + d
    return x + 0.5 * acc


def _resident(shape):
    return pl.BlockSpec(shape, lambda *_: (0,) * len(shape), pipeline_mode=pl.Buffered(1))


def _swap16(t, axis):
    n = t.shape[axis] // 16
    parts = [lax.slice_in_dim(t, (i ^ 1) * 16, (i ^ 1) * 16 + 16, axis=axis) for i in range(n)]
    return jnp.concatenate(parts, axis=axis)


def _ffn_inproj_body(x_ref, g1_ref, wgu_ref, wd_ref, g_ref, wqv_ref, wrest_ref, qg_ref, kg_ref,
                     cost_ref, sint_ref, cosn_ref, sinn_ref, grp_ref, wgate_ref, bgate_ref,
                     h_ref, qt_ref, k_ref, vt_ref, ql_ref, kl_ref, vl_ref, rl_ref, laf_ref, lab_ref):
    h = _swiglu_half_step(x_ref[0], g1_ref[...], wgu_ref, wd_ref)
    h_ref[0] = h
    xn = _rms_rows(h, g_ref[...]).astype(BF16)
    tm = xn.shape[0]

    ut = _nt_dot(wqv_ref[...], xn)
    q = ut[:ATT_WIDTH].reshape(N_HEADS, HEAD_DIM, tm)
    q = q * lax.rsqrt(jnp.mean(q * q, axis=1, keepdims=True) + EPS) * qg_ref[...][None]
    q = q * cost_ref[...][None] + _swap16(q, 1) * sint_ref[...][None]
    qt_ref[0] = (q * (HEAD_DIM ** -0.5 * LOG2E)).reshape(ATT_WIDTH, tm).astype(BF16)

    row = lax.broadcasted_iota(jnp.int32, (V_ROWS - HEAD_DIM, tm), 0)
    ones_rows = jnp.where(row == 0, 1.0, 0.0).astype(BF16)
    for kvh in range(N_KV_HEADS):
        v = ut[ATT_WIDTH + kvh * HEAD_DIM:ATT_WIDTH + (kvh + 1) * HEAD_DIM].astype(BF16)
        vt_ref[0, kvh, 0] = jnp.concatenate([v, ones_rows], axis=0)

    rest = _bdot(xn, wrest_ref[...])

    k = rest[:, :KV_W]
    ss = k * k
    ss_hi = ss.astype(BF16)
    ss_lo = (ss - ss_hi.astype(F32)).astype(BF16)
    ms = _bdot(ss_hi, grp_ref[...]) + _bdot(ss_lo, grp_ref[...])
    k = k * lax.rsqrt(ms + EPS) * kg_ref[...]
    lane = lax.broadcasted_iota(jnp.int32, k.shape, 1)
    k_sw = jnp.where((lane % 32) < 16, pltpu.roll(k, LANES - 16, 1), pltpu.roll(k, 16, 1))
    k = k * cosn_ref[...] + k_sw * sinn_ref[...]
    tail = jnp.where(lane == HEAD_DIM, 1.0, 0.0)
    k_ref[0, 0] = jnp.where(lane < HEAD_DIM, k, tail).astype(BF16)
    k_ref[0, 1] = jnp.where(lane < HEAD_DIM, pltpu.roll(k, HEAD_DIM, 1), tail).astype(BF16)

    o = KV_W
    ql_ref[0] = rest[:, o:o + GLA_KEY_WIDTH]
    o += GLA_KEY_WIDTH
    kl_ref[0] = rest[:, o:o + GLA_KEY_WIDTH]
    o += GLA_KEY_WIDTH
    vl_ref[0] = rest[:, o:o + GLA_WIDTH].astype(BF16)
    o += GLA_WIDTH
    rl_ref[0] = rest[:, o:o + GLA_WIDTH]

    pre = _bdot(rest[:, GATE_OFF:].astype(BF16), wgate_ref[...]) + bgate_ref[...]
    logsig = jnp.minimum(pre, 0.0) - jnp.log(1.0 + jnp.exp(-jnp.abs(pre)))
    la = logsig * (1.0 / GATE_TAU)
    laf_ref[0] = la[:, :GLA_KEY_WIDTH]
    lab_ref[0] = la[:, GLA_KEY_WIDTH:]


def _ffn_inproj(x, p, tabs):
    b, s, _ = x.shape
    nt = s // TM
    c2 = lambda i, j: (0, 0)
    tok = lambda w: pl.BlockSpec((1, TM, w), lambda i, j: (i, j, 0))
    out_shape = (
        jax.ShapeDtypeStruct((b, s, D_MODEL), F32),
        jax.ShapeDtypeStruct((b, ATT_WIDTH, s), BF16),
        jax.ShapeDtypeStruct((b, N_KV_HEADS, s, LANES), BF16),
        jax.ShapeDtypeStruct((b, N_KV_HEADS, nt, V_ROWS, TM), BF16),
        jax.ShapeDtypeStruct((b, s, GLA_KEY_WIDTH), F32),
        jax.ShapeDtypeStruct((b, s, GLA_KEY_WIDTH), F32),
        jax.ShapeDtypeStruct((b, s, GLA_WIDTH), BF16),
        jax.ShapeDtypeStruct((b, s, GLA_WIDTH), F32),
        jax.ShapeDtypeStruct((b, s, GLA_KEY_WIDTH), F32),
        jax.ShapeDtypeStruct((b, s, GLA_KEY_WIDTH), F32),
    )
    out_specs = (
        tok(D_MODEL),
        pl.BlockSpec((1, ATT_WIDTH, TM), lambda i, j: (i, 0, j)),
        pl.BlockSpec((1, N_KV_HEADS, TM, LANES), lambda i, j: (i, 0, j, 0)),
        pl.BlockSpec((1, N_KV_HEADS, 1, V_ROWS, TM), lambda i, j: (i, 0, j, 0, 0)),
        tok(GLA_KEY_WIDTH), tok(GLA_KEY_WIDTH), tok(GLA_WIDTH), tok(GLA_WIDTH), tok(GLA_KEY_WIDTH),
        tok(GLA_KEY_WIDTH),
    )
    in_specs = [
        tok(D_MODEL),
        pl.BlockSpec((1, D_MODEL), c2),
        _resident((D_MODEL, 2 * D_FF)),
        _resident((D_FF, D_MODEL)),
        pl.BlockSpec((1, D_MODEL), c2),
        _resident((QV_ROWS, D_MODEL)),
        _resident((D_MODEL, REST_W)),
        pl.BlockSpec((HEAD_DIM, 1), c2),
        pl.BlockSpec((1, KV_W), c2),
        pl.BlockSpec((HEAD_DIM, TM), lambda i, j: (0, j)),
        pl.BlockSpec((HEAD_DIM, TM), lambda i, j: (0, j)),
        pl.BlockSpec((TM, KV_W), lambda i, j: (j, 0)),
        pl.BlockSpec((TM, KV_W), lambda i, j: (j, 0)),
        pl.BlockSpec((KV_W, KV_W), c2),
        pl.BlockSpec((LANES, 2 * GLA_KEY_WIDTH), c2),
        pl.BlockSpec((1, 2 * GLA_KEY_WIDTH), c2),
    ]
    return pl.pallas_call(
        _ffn_inproj_body,
        grid=(b, nt),
        in_specs=in_specs,
        out_specs=out_specs,
        out_shape=out_shape,
        compiler_params=_cparams(("parallel", "parallel")),
        name="ffn_inproj",
    )(x, p["norm_ffn1"], p["w_ffn1_gu"], p["w_ffn1_down"],
      p["norm_mix"], p["w_qv_t"], p["w_rest"], p["q_gain_col"], p["k_gain_row"],
      tabs["cos_t"], tabs["sin_t"], tabs["cos_n"], tabs["sin_n"], p["grp_avg"], p["w_gate"], p["b_gate"])


def _query_matrix(qblk, extra_row):
    top = jnp.concatenate([qblk[g * HEAD_DIM:(g + 1) * HEAD_DIM] for g in range(KV_GROUP)], axis=1)
    row = lax.broadcasted_iota(jnp.int32, top.shape, 0)
    bottom = jnp.where(row == 0, extra_row, 0.0).astype(BF16)
    return jnp.concatenate([top, bottom], axis=0)


def _head_outputs(acc, cols):
    out_t = acc[:HEAD_DIM] * (1.0 / acc[HEAD_DIM:HEAD_DIM + 1])
    heads = [out_t[:, sl].T for sl in cols]
    return jnp.concatenate(heads, axis=1).astype(BF16)


def _attn_shifted_body(trips_ref, shift_ref, qt_ref, k_ref, vt_ref, o_ref, qcat_ref, acc_ref, *, n_chunks):
    n_tiles = qcat_ref.shape[0]
    tq = qt_ref.shape[2] // n_tiles
    cols = [slice(g * tq, (g + 1) * tq) for g in range(KV_GROUP)]
    shift_row = jnp.full((1, KV_GROUP * tq), -shift_ref[0], F32)
    for t in range(n_tiles):
        qcat_ref[t] = _query_matrix(qt_ref[0, :, t * tq:(t + 1) * tq], shift_row)
    acc_ref[...] = jnp.zeros(acc_ref.shape, F32)

    def scores(t, c, g):
        return _bdot(k_ref[0, 0, c * TM:(c + 1) * TM, :], qcat_ref[t, :, cols[g]])

    def absorb(t, c, g, s):
        acc_ref[t, :, cols[g]] += _bdot(vt_ref[0, 0, c], jnp.exp2(s).astype(BF16))

    units = [(t, c, g) for t in range(n_tiles) for c in range(n_chunks) for g in range(KV_GROUP)]
    pending = [scores(*u) for u in units[:ATTN_LEAD]]
    for idx, (t, c, g) in enumerate(units):
        if idx + ATTN_LEAD < len(units):
            pending.append(scores(*units[idx + ATTN_LEAD]))
        absorb(t, c, g, pending.pop(0))
        if c == n_chunks - 1 and g == KV_GROUP - 1:
            o_ref[0, t * tq:(t + 1) * tq, :] = _head_outputs(acc_ref[t], cols)


def _attn_online_body(trips_ref, shift_ref, qt_ref, k_ref, vt_ref, o_ref, qcat_ref, acc_ref, *bufs, n_chunks):
    tq = qt_ref.shape[2]
    cols = [slice(g * tq, (g + 1) * tq) for g in range(KV_GROUP)]
    qcat_ref[...] = _query_matrix(qt_ref[0], jnp.zeros((1, KV_GROUP * tq), F32))
    acc_ref[...] = jnp.zeros(acc_ref.shape, F32)

    def scores(c, dst_ref):
        kc = k_ref[0, 0, pl.ds(pl.multiple_of(c * TM, TM), TM), :]
        maxes = []
        for sl in cols:
            s = _bdot(kc, qcat_ref[:, sl])
            dst_ref[:, sl] = s
            maxes.append(jnp.max(s, axis=0, keepdims=True))
        return jnp.concatenate(maxes, axis=1)

    def absorb(c, src_ref, chunk_max, m_old):
        m_new = jnp.maximum(m_old, chunk_max)
        alpha = jnp.exp2(m_old - m_new)
        vt = vt_ref[0, 0, c]
        acc_ref[...] = alpha * acc_ref[...]
        for sl in cols:
            p = jnp.exp2(src_ref[:, sl] - m_new[:, sl]).astype(BF16)
            acc_ref[:, sl] += _bdot(vt, p)
        return m_new

    n_buf = len(bufs)

    def group(first, carry, last_group):
        m, chunk_max = carry
        for j in range(n_buf):
            c = first + j
            next_max = None if last_group and j + 1 == n_buf else scores(c + 1, bufs[(j + 1) % n_buf])
            m = absorb(c, bufs[j], chunk_max, m)
            chunk_max = next_max
        return m, chunk_max

    carry = (jnp.full((1, KV_GROUP * tq), NEG_BIG, F32), scores(0, bufs[0]))
    carry = lax.fori_loop(0, trips_ref[0], lambda i, cr: group(i * n_buf, cr, False), carry)
    group(n_chunks - n_buf, carry, True)
    o_ref[0] = _head_outputs(acc_ref[...], cols)


def _attn_call(body, tiles_per_step, n_score_buffers, score_bound, qt, k, vt):
    b, _, s = qt.shape
    n_chunks = s // TM
    assert n_chunks % SCORE_BUFFERS == 0
    tq = min(TQ, s)
    mq = KV_GROUP * tq
    step = tiles_per_step * tq
    assert s % step == 0
    lead = (tiles_per_step,) if tiles_per_step > 1 else ()
    trips = jnp.full((1,), n_chunks // SCORE_BUFFERS - 1, jnp.int32)
    return pl.pallas_call(
        functools.partial(body, n_chunks=n_chunks),
        grid_spec=pltpu.PrefetchScalarGridSpec(
            num_scalar_prefetch=2,
            grid=(b, N_KV_HEADS, s // step),
            in_specs=[
                pl.BlockSpec((1, KV_GROUP * HEAD_DIM, step), lambda i, h, j, *_: (i, h, j)),
                pl.BlockSpec((1, 1, s, LANES), lambda i, h, j, *_: (i, h, 0, 0)),
                pl.BlockSpec((1, 1, n_chunks, V_ROWS, TM), lambda i, h, j, *_: (i, h, 0, 0, 0)),
            ],
            out_specs=pl.BlockSpec((1, step, KV_GROUP * HEAD_DIM), lambda i, h, j, *_: (i, j, h)),
            scratch_shapes=[
                pltpu.VMEM(lead + (LANES, mq), BF16),
                pltpu.VMEM(lead + (V_ROWS, mq), F32),
            ] + [pltpu.VMEM((TM, mq + LANES), F32)] * n_score_buffers,
        ),
        out_shape=jax.ShapeDtypeStruct((b, s, ATT_WIDTH), BF16),
        compiler_params=_cparams(("parallel", "parallel", "arbitrary")),
        name="attn",
    )(trips, score_bound, qt, k, vt)


def _attn(qt, k, vt, score_bound):
    return lax.cond(score_bound[0] <= MAX_FIXED_SHIFT,
                    functools.partial(_attn_call, _attn_shifted_body, TILES_PER_STEP, 0),
                    functools.partial(_attn_call, _attn_online_body, 1, SCORE_BUFFERS),
                    score_bound, qt, k, vt)


def _gla_direction(q_ref, k_ref, v_ref, la_ref, o_ref, state_ref, order, cum, keep, ref_idx, last_idx):
    heads = range(GLA_HEADS)
    psl = [slice(p * LANES, (p + 1) * LANES) for p in range(GLA_HEADS * GLA_DK // LANES)]
    vsl = [slice(h * GLA_DV, (h + 1) * GLA_DV) for h in heads]
    rows = [slice(ci * CHUNK, (ci + 1) * CHUNK) for ci in order]
    n = range(len(rows))

    b = []
    for sl in rows:
        la = la_ref[0, sl, :]
        la_hi = la.astype(BF16)
        la_lo = (la - la_hi.astype(F32)).astype(BF16)
        b.append(_bdot(cum, la_hi) + _bdot(cum, la_lo))

    qr, kr, kd, qb, b_lasts, v = [], [], [], [], [], []
    for sl, bc in zip(rows, b):
        b_ref = bc[ref_idx:ref_idx + 1]
        b_last = bc[last_idx:last_idx + 1]
        qs = q_ref[0, sl, :] * (GLA_DK ** -0.5)
        k = k_ref[0, sl, :]
        qr.append((qs * jnp.exp(bc - b_ref)).astype(BF16))
        kr.append((k * jnp.exp(b_ref - bc)).astype(BF16))
        kd.append(k * jnp.exp(b_last - bc))
        qb.append((qs * jnp.exp(bc)).astype(BF16))
        b_lasts.append(b_last)
        v.append(v_ref[0, sl, :])

    pairs = range(len(psl))
    lane = lax.broadcasted_iota(jnp.int32, (1, LANES), 1)
    own = [lane < GLA_DK, lane >= GLA_DK]
    keep2 = jnp.concatenate([keep, keep], axis=0)
    rr = lax.broadcasted_iota(jnp.int32, (LANES, 2 * GLA_DV), 0)
    cc = lax.broadcasted_iota(jnp.int32, (LANES, 2 * GLA_DV), 1)
    same_head = (rr < GLA_DK) == (cc < GLA_DV)
    a = [[None] * GLA_HEADS for _ in n]
    d_state = [[None] * len(psl) for _ in n]
    dec_col = [[None] * len(psl) for _ in n]
    for i in n:
        for p, sl in enumerate(psl):
            qr_p = qr[i][:, sl]
            stacked = jnp.concatenate([jnp.where(own[0], qr_p, 0), jnp.where(own[1], qr_p, 0)], axis=0)
            a2 = jnp.where(keep2, _nt_dot(stacked, kr[i][:, sl]), 0.0).astype(BF16)
            a[i][2 * p] = a2[:CHUNK]
            a[i][2 * p + 1] = a2[CHUNK:]
            both = jnp.concatenate([kd[i][:, sl], jnp.broadcast_to(b_lasts[i][:, sl], (CHUNK, LANES))], axis=0).T
            v_pair = v[i][:, p * 2 * GLA_DV:(p + 1) * 2 * GLA_DV]
            d_state[i][p] = jnp.where(same_head, _bdot(both[:, :CHUNK].astype(BF16), v_pair), 0.0)
            dec_col[i][p] = jnp.exp(both[:, CHUNK:CHUNK + 1])
    o_intra = [[_bdot(a[i][h], v[i][:, vsl[h]]) for h in heads] for i in n]

    state = [state_ref[p] for p in pairs]
    entering = []
    for i in n:
        entering.append(state)
        state = [dec_col[i][p] * state[p] + d_state[i][p] for p in pairs]
    for p in pairs:
        state_ref[p] = state[p]

    o_inter = [[_bdot(qb[i][:, psl[p]], entering[i][p].astype(BF16)) for p in pairs] for i in n]
    for i in n:
        o_ref[0, rows[i], :] = jnp.concatenate(o_intra[i], axis=1) + jnp.concatenate(o_inter[i], axis=1)


def _gla_body(qf_ref, kf_ref, vf_ref, laf_ref, qb_ref, kb_ref, vb_ref, lab_ref, of_ref, ob_ref, sf_ref, sb_ref,
              *, n_chunks):
    @pl.when(pl.program_id(1) == 0)
    def _():
        sf_ref[...] = jnp.zeros(sf_ref.shape, F32)
        sb_ref[...] = jnp.zeros(sb_ref.shape, F32)

    r = lax.broadcasted_iota(jnp.int32, (CHUNK, CHUNK), 0)
    c = lax.broadcasted_iota(jnp.int32, (CHUNK, CHUNK), 1)
    _gla_direction(qf_ref, kf_ref, vf_ref, laf_ref, of_ref, sf_ref, range(n_chunks),
                   jnp.where(c <= r, 1.0, 0.0).astype(BF16), c <= r, CHUNK // 2, CHUNK - 1)
    _gla_direction(qb_ref, kb_ref, vb_ref, lab_ref, ob_ref, sb_ref, range(n_chunks - 1, -1, -1),
                   jnp.where(c >= r, 1.0, 0.0).astype(BF16), c > r, CHUNK // 2 - 1, 0)


def _gla(ql, kl, vl, laf, lab):
    b, s, _ = ql.shape
    tb = min(GLA_BLOCK, s)
    nb = s // tb
    fwd = lambda w: pl.BlockSpec((1, tb, w), lambda i, j: (i, j, 0))
    bwd = lambda w: pl.BlockSpec((1, tb, w), lambda i, j: (i, nb - 1 - j, 0))
    return pl.pallas_call(
        functools.partial(_gla_body, n_chunks=tb // CHUNK),
        grid=(b, nb),
        in_specs=[fwd(GLA_KEY_WIDTH), fwd(GLA_KEY_WIDTH), fwd(GLA_WIDTH), fwd(GLA_KEY_WIDTH),
                  bwd(GLA_KEY_WIDTH), bwd(GLA_KEY_WIDTH), bwd(GLA_WIDTH), bwd(GLA_KEY_WIDTH)],
        out_specs=(fwd(GLA_WIDTH), bwd(GLA_WIDTH)),
        out_shape=(jax.ShapeDtypeStruct((b, s, GLA_WIDTH), F32), jax.ShapeDtypeStruct((b, s, GLA_WIDTH), F32)),
        scratch_shapes=[pltpu.VMEM((GLA_HEADS // 2, LANES, 2 * GLA_DV), F32),
                        pltpu.VMEM((GLA_HEADS // 2, LANES, 2 * GLA_DV), F32)],
        compiler_params=_cparams(("parallel", "arbitrary")),
        name="gla",
    )(ql, kl, vl, laf, ql, kl, vl, lab)


def _outproj_ffn_body(h_ref, oa_ref, of_ref, ob_ref, r_ref, gg_ref, wa_ref, wg_ref, g2_ref, wgu_ref, wd_ref, og_ref,
                      o_ref):
    o = of_ref[...] + ob_ref[...]
    gain = gg_ref[...]
    r = r_ref[...]
    parts = []
    for hh in range(GLA_HEADS):
        sl = slice(hh * GLA_DV, (hh + 1) * GLA_DV)
        rh = r[:, sl]
        parts.append((_rms_rows(o[:, sl], gain[:, sl]) * (rh * (1.0 / (1.0 + jnp.exp(-rh))))).astype(BF16))
    og = jnp.concatenate(parts, axis=1)
    h = h_ref[...] + _bdot(oa_ref[...], wa_ref[...]) + _bdot(og, wg_ref[...])
    o_ref[...] = _rms_rows(_swiglu_half_step(h, g2_ref[...], wgu_ref, wd_ref), og_ref[...])


def _outproj_ffn(h, o_att, o_f, o_b, r, p):
    t = h.shape[0]
    const = lambda i: (0, 0)
    tok = lambda w: pl.BlockSpec((TM, w), lambda i: (i, 0))
    return pl.pallas_call(
        _outproj_ffn_body,
        grid=(t // TM,),
        in_specs=[tok(D_MODEL), tok(ATT_WIDTH), tok(GLA_WIDTH), tok(GLA_WIDTH), tok(GLA_WIDTH),
                  pl.BlockSpec((1, GLA_WIDTH), const),
                  _resident((ATT_WIDTH, D_MODEL)),
                  _resident((GLA_WIDTH, D_MODEL)),
                  pl.BlockSpec((1, D_MODEL), const),
                  _resident((D_MODEL, 2 * D_FF)),
                  _resident((D_FF, D_MODEL)),
                  pl.BlockSpec((1, D_MODEL), const)],
        out_specs=tok(D_MODEL),
        out_shape=jax.ShapeDtypeStruct((t, D_MODEL), F32),
        compiler_params=_cparams(("parallel",)),
        name="outproj_ffn",
    )(h, o_att, o_f, o_b, r, p["gla_gain"], p["w_out_att"], p["w_out_gla"],
      p["norm_ffn2"], p["w_ffn2_gu"], p["w_ffn2_down"], p["norm_out"])


def _rope_tables(seq_len):
    rows = seq_len // GRID_W
    row = jnp.repeat(jnp.arange(rows, dtype=F32), GRID_W)
    col = jnp.tile(jnp.arange(GRID_W, dtype=F32), rows)
    inv_freq = 1.0 / (ROPE_THETA ** (jnp.arange(0, AXIS_DIM, 2, dtype=F32) / AXIS_DIM))
    ang_r = row[:, None] * inv_freq[None, :]
    ang_c = col[:, None] * inv_freq[None, :]
    ang = jnp.concatenate([ang_r, ang_r, ang_c, ang_c], axis=-1)
    sign = jnp.where((jnp.arange(HEAD_DIM) % 32) < 16, -1.0, 1.0).astype(F32)
    cos, sin = jnp.cos(ang), jnp.sin(ang) * sign[None, :]
    return {
        "cos_t": cos.T, "sin_t": sin.T,
        "cos_n": jnp.tile(cos, (1, N_KV_HEADS)), "sin_n": jnp.tile(sin, (1, N_KV_HEADS)),
    }


def _prep_layer(l, norm_ffn1, w_ffn1_gu, w_ffn1_down, norm_mix, w_in, q_norm, k_norm, w_gate_f, b_gate_f,
                w_gate_b, b_gate_b, gla_norm, w_out, norm_ffn2, w_ffn2_gu, w_ffn2_down, norm_out):
    w = w_in[l]
    o_q, o_k, o_v = 0, ATT_WIDTH, ATT_WIDTH + KV_W
    o_rest = ATT_WIDTH + 2 * KV_W
    n_rest = 2 * GLA_KEY_WIDTH + 2 * GLA_WIDTH
    gates = jnp.pad(w[:, o_rest + n_rest:], ((0, 0), (0, LANES - 2 * GATE_RANK)))
    w_rest = jnp.concatenate([w[:, o_k:o_v], w[:, o_rest:o_rest + n_rest], gates], axis=1).astype(BF16)
    w_qv_t = jnp.concatenate([w[:, o_q:o_k], w[:, o_v:o_rest]], axis=1).T.astype(BF16)
    w_gate = jnp.zeros((LANES, 2 * GLA_KEY_WIDTH), F32)
    w_gate = w_gate.at[:GATE_RANK, :GLA_KEY_WIDTH].set(w_gate_f[l])
    w_gate = w_gate.at[GATE_RANK:2 * GATE_RANK, GLA_KEY_WIDTH:].set(w_gate_b[l])
    hid = jnp.arange(KV_W) // HEAD_DIM
    grp_avg = jnp.where(hid[:, None] == hid[None, :], 1.0 / HEAD_DIM, 0.0).astype(BF16)
    row = lambda v: v.reshape(1, -1).astype(F32)
    return {
        "norm_ffn1": row(norm_ffn1[l]), "w_ffn1_gu": w_ffn1_gu[l].astype(BF16), "w_ffn1_down": w_ffn1_down[l].astype(BF16),
        "norm_mix": row(norm_mix[l]), "w_qv_t": w_qv_t, "w_rest": w_rest,
        "q_gain_col": q_norm[l].reshape(HEAD_DIM, 1).astype(F32),
        "k_gain_row": row(jnp.tile(k_norm[l], N_KV_HEADS)),
        "grp_avg": grp_avg, "w_gate": w_gate.astype(BF16),
        "score_bound": (HEAD_DIM ** 0.5 * LOG2E * jnp.max(jnp.abs(q_norm[l])) * jnp.max(jnp.abs(k_norm[l]))
                        ).reshape(1).astype(F32),
        "b_gate": row(jnp.concatenate([b_gate_f[l], b_gate_b[l]])),
        "gla_gain": row(jnp.tile(gla_norm[l], GLA_HEADS)),
        "w_out_att": w_out[l][:ATT_WIDTH].astype(BF16), "w_out_gla": w_out[l][ATT_WIDTH:].astype(BF16),
        "norm_ffn2": row(norm_ffn2[l]), "w_ffn2_gu": w_ffn2_gu[l].astype(BF16), "w_ffn2_down": w_ffn2_down[l].astype(BF16),
        "norm_out": row(norm_out[l]),
    }


def _layer(x, p, tabs):
    b, s, d = x.shape
    t = b * s
    h, qt, k, vt, ql, kl, vl, rl, laf, lab = _ffn_inproj(x, p, tabs)
    o_att = _attn(qt, k, vt, p["score_bound"])
    o_f, o_b = _gla(ql, kl, vl, laf, lab)
    y = _outproj_ffn(h.reshape(t, d), o_att.reshape(t, ATT_WIDTH), o_f.reshape(t, GLA_WIDTH),
                     o_b.reshape(t, GLA_WIDTH), rl.reshape(t, GLA_WIDTH), p)
    return y.reshape(b, s, d)


def _trunk(x, layers):
    tabs = _rope_tables(x.shape[1])
    h = x
    for p in layers:
        h = _layer(h, p, tabs)
    return h


def kernel(x_prompt, x_sample, norm_ffn1, w_ffn1_gu, w_ffn1_down, norm_mix, w_in, q_norm, k_norm, w_gate_f, b_gate_f,
           w_gate_b, b_gate_b, gla_norm, w_out, norm_ffn2, w_ffn2_gu, w_ffn2_down, norm_out):
    params = (norm_ffn1, w_ffn1_gu, w_ffn1_down, norm_mix, w_in, q_norm, k_norm, w_gate_f, b_gate_f,
              w_gate_b, b_gate_b, gla_norm, w_out, norm_ffn2, w_ffn2_gu, w_ffn2_down, norm_out)
    layers = [_prep_layer(l, *params) for l in range(norm_ffn1.shape[0])]
    return (_trunk(x_prompt, layers), _trunk(x_sample, layers))
```

```python
import functools

import jax
import jax.numpy as jnp
from jax import lax
from jax.experimental import pallas as pl
from jax.experimental.pallas import tpu as pltpu

F32 = jnp.float32
BF16 = jnp.bfloat16

D_MODEL = 1024
GRID_W = 64
ATT_WIDTH = 512
HEAD_DIM = 64
N_HEADS = 8
N_KV_HEADS = 2
KV_GROUP = 4
AXIS_DIM = 32
ROPE_THETA = 10000.0
GLA_WIDTH = 512
GLA_HEADS = 4
GLA_DV = 128
GLA_DK = 64
GLA_KEY_WIDTH = 256
GATE_RANK = 16
GATE_TAU = 16.0
CHUNK = 64
D_FF = 2816
EPS = 1e-6

LANES = 128
KV_W = N_KV_HEADS * HEAD_DIM
QV_ROWS = ATT_WIDTH + KV_W
REST_W = KV_W + 2 * GLA_KEY_WIDTH + 2 * GLA_WIDTH + LANES
GATE_OFF = REST_W - LANES

TM = 512
TQ = 256
MAX_UNITS_PER_STEP = 256
ATTN_LEAD = 2
MAX_FIXED_SHIFT = 48.0
SCORE_BUFFERS = 4
FF_CHUNK = 256
N_FF_CHUNKS = D_FF // FF_CHUNK
GLA_BLOCK = 512
NEG_BIG = -1e30
LOG2E = 1.4426950408889634
V_ROWS = 2 * HEAD_DIM
VMEM_LIMIT = 56 * 1024 * 1024


def _cparams(sem):
    return pltpu.CompilerParams(dimension_semantics=sem, vmem_limit_bytes=VMEM_LIMIT)


def _rms_rows(x, gain_row):
    ms = jnp.mean(x * x, axis=-1, keepdims=True)
    return x * lax.rsqrt(ms + EPS) * gain_row


def _bdot(a, b):
    return jnp.dot(a, b, preferred_element_type=F32)


def _nt_dot(a, b):
    return lax.dot_general(a, b, (((1,), (1,)), ((), ())), preferred_element_type=F32)


def _swiglu_half_step(x, gain_row, wgu_ref, wd_ref):
    xn = _rms_rows(x, gain_row).astype(BF16)
    acc = None
    for c in range(N_FF_CHUNKS):
        g = _bdot(xn, wgu_ref[:, c * FF_CHUNK:(c + 1) * FF_CHUNK])
        u = _bdot(xn, wgu_ref[:, D_FF + c * FF_CHUNK:D_FF + (c + 1) * FF_CHUNK])
        a = (g * (1.0 / (1.0 + jnp.exp(-g))) * u).astype(BF16)
        d = _bdot(a, wd_ref[c * FF_CHUNK:(c + 1) * FF_CHUNK, :])
        acc = d if acc is None else acc + d
    return x + 0.5 * acc


def _resident(shape):
    return pl.BlockSpec(shape, lambda *_: (0,) * len(shape), pipeline_mode=pl.Buffered(1))


def _swap16(t, axis):
    n = t.shape[axis] // 16
    parts = [lax.slice_in_dim(t, (i ^ 1) * 16, (i ^ 1) * 16 + 16, axis=axis) for i in range(n)]
    return jnp.concatenate(parts, axis=axis)


def _ffn_inproj_body(x_ref, g1_ref, wgu_ref, wd_ref, g_ref, wqv_ref, wrest_ref, qg_ref, kg_ref,
                     cost_ref, sint_ref, cosn_ref, sinn_ref, grp_ref, wgate_ref, bgate_ref,
                     h_ref, qt_ref, k_ref, vt_ref, ql_ref, kl_ref, vl_ref, rl_ref, laf_ref, lab_ref):
    h = _swiglu_half_step(x_ref[0], g1_ref[...], wgu_ref, wd_ref)
    h_ref[0] = h
    xn = _rms_rows(h, g_ref[...]).astype(BF16)
    tm = xn.shape[0]

    ut = _nt_dot(wqv_ref[...], xn)
    q = ut[:ATT_WIDTH].reshape(N_HEADS, HEAD_DIM, tm)
    q = q * lax.rsqrt(jnp.mean(q * q, axis=1, keepdims=True) + EPS) * qg_ref[...][None]
    q = q * cost_ref[...][None] + _swap16(q, 1) * sint_ref[...][None]
    qt_ref[0] = (q * (HEAD_DIM ** -0.5 * LOG2E)).reshape(ATT_WIDTH, tm).astype(BF16)

    row = lax.broadcasted_iota(jnp.int32, (V_ROWS - HEAD_DIM, tm), 0)
    ones_rows = jnp.where(row == 0, 1.0, 0.0).astype(BF16)
    for kvh in range(N_KV_HEADS):
        v = ut[ATT_WIDTH + kvh * HEAD_DIM:ATT_WIDTH + (kvh + 1) * HEAD_DIM].astype(BF16)
        vt_ref[0, kvh, 0] = jnp.concatenate([v, ones_rows], axis=0)

    rest = _bdot(xn, wrest_ref[...])

    k = rest[:, :KV_W]
    ss = k * k
    ss_hi = ss.astype(BF16)
    ss_lo = (ss - ss_hi.astype(F32)).astype(BF16)
    ms = _bdot(ss_hi, grp_ref[...]) + _bdot(ss_lo, grp_ref[...])
    k = k * lax.rsqrt(ms + EPS) * kg_ref[...]
    lane = lax.broadcasted_iota(jnp.int32, k.shape, 1)
    k_sw = jnp.where((lane % 32) < 16, pltpu.roll(k, LANES - 16, 1), pltpu.roll(k, 16, 1))
    k = k * cosn_ref[...] + k_sw * sinn_ref[...]
    tail = jnp.where(lane == HEAD_DIM, 1.0, 0.0)
    k_ref[0, 0] = jnp.where(lane < HEAD_DIM, k, tail).astype(BF16)
    k_ref[0, 1] = jnp.where(lane < HEAD_DIM, pltpu.roll(k, HEAD_DIM, 1), tail).astype(BF16)

    o = KV_W
    ql_ref[0] = rest[:, o:o + GLA_KEY_WIDTH]
    o += GLA_KEY_WIDTH
    kl_ref[0] = rest[:, o:o + GLA_KEY_WIDTH]
    o += GLA_KEY_WIDTH
    vl_ref[0] = rest[:, o:o + GLA_WIDTH].astype(BF16)
    o += GLA_WIDTH
    rl_ref[0] = rest[:, o:o + GLA_WIDTH]

    pre = _bdot(rest[:, GATE_OFF:].astype(BF16), wgate_ref[...]) + bgate_ref[...]
    logsig = jnp.minimum(pre, 0.0) - jnp.log(1.0 + jnp.exp(-jnp.abs(pre)))
    la = logsig * (1.0 / GATE_TAU)
    laf_ref[0] = la[:, :GLA_KEY_WIDTH]
    lab_ref[0] = la[:, GLA_KEY_WIDTH:]


def _ffn_inproj(x, p, tabs):
    b, s, _ = x.shape
    nt = s // TM
    c2 = lambda i, j: (0, 0)
    tok = lambda w: pl.BlockSpec((1, TM, w), lambda i, j: (i, j, 0))
    out_shape = (
        jax.ShapeDtypeStruct((b, s, D_MODEL), F32),
        jax.ShapeDtypeStruct((b, ATT_WIDTH, s), BF16),
        jax.ShapeDtypeStruct((b, N_KV_HEADS, s, LANES), BF16),
        jax.ShapeDtypeStruct((b, N_KV_HEADS, nt, V_ROWS, TM), BF16),
        jax.ShapeDtypeStruct((b, s, GLA_KEY_WIDTH), F32),
        jax.ShapeDtypeStruct((b, s, GLA_KEY_WIDTH), F32),
        jax.ShapeDtypeStruct((b, s, GLA_WIDTH), BF16),
        jax.ShapeDtypeStruct((b, s, GLA_WIDTH), F32),
        jax.ShapeDtypeStruct((b, s, GLA_KEY_WIDTH), F32),
        jax.ShapeDtypeStruct((b, s, GLA_KEY_WIDTH), F32),
    )
    out_specs = (
        tok(D_MODEL),
        pl.BlockSpec((1, ATT_WIDTH, TM), lambda i, j: (i, 0, j)),
        pl.BlockSpec((1, N_KV_HEADS, TM, LANES), lambda i, j: (i, 0, j, 0)),
        pl.BlockSpec((1, N_KV_HEADS, 1, V_ROWS, TM), lambda i, j: (i, 0, j, 0, 0)),
        tok(GLA_KEY_WIDTH), tok(GLA_KEY_WIDTH), tok(GLA_WIDTH), tok(GLA_WIDTH), tok(GLA_KEY_WIDTH),
        tok(GLA_KEY_WIDTH),
    )
    in_specs = [
        tok(D_MODEL),
        pl.BlockSpec((1, D_MODEL), c2),
        _resident((D_MODEL, 2 * D_FF)),
        _resident((D_FF, D_MODEL)),
        pl.BlockSpec((1, D_MODEL), c2),
        _resident((QV_ROWS, D_MODEL)),
        _resident((D_MODEL, REST_W)),
        pl.BlockSpec((HEAD_DIM, 1), c2),
        pl.BlockSpec((1, KV_W), c2),
        pl.BlockSpec((HEAD_DIM, TM), lambda i, j: (0, j)),
        pl.BlockSpec((HEAD_DIM, TM), lambda i, j: (0, j)),
        pl.BlockSpec((TM, KV_W), lambda i, j: (j, 0)),
        pl.BlockSpec((TM, KV_W), lambda i, j: (j, 0)),
        pl.BlockSpec((KV_W, KV_W), c2),
        pl.BlockSpec((LANES, 2 * GLA_KEY_WIDTH), c2),
        pl.BlockSpec((1, 2 * GLA_KEY_WIDTH), c2),
    ]
    return pl.pallas_call(
        _ffn_inproj_body,
        grid=(b, nt),
        in_specs=in_specs,
        out_specs=out_specs,
        out_shape=out_shape,
        compiler_params=_cparams(("parallel", "parallel")),
        name="ffn_inproj",
    )(x, p["norm_ffn1"], p["w_ffn1_gu"], p["w_ffn1_down"],
      p["norm_mix"], p["w_qv_t"], p["w_rest"], p["q_gain_col"], p["k_gain_row"],
      tabs["cos_t"], tabs["sin_t"], tabs["cos_n"], tabs["sin_n"], p["grp_avg"], p["w_gate"], p["b_gate"])


def _query_matrix(qblk, extra_row):
    top = jnp.concatenate([qblk[g * HEAD_DIM:(g + 1) * HEAD_DIM] for g in range(KV_GROUP)], axis=1)
    row = lax.broadcasted_iota(jnp.int32, top.shape, 0)
    bottom = jnp.where(row == 0, extra_row, 0.0).astype(BF16)
    return jnp.concatenate([top, bottom], axis=0)


def _head_outputs(acc, cols):
    out_t = acc[:HEAD_DIM] * (1.0 / acc[HEAD_DIM:HEAD_DIM + 1])
    heads = [out_t[:, sl].T for sl in cols]
    return jnp.concatenate(heads, axis=1).astype(BF16)


def _attn_shifted_body(trips_ref, shift_ref, qt_ref, k_ref, vt_ref, o_ref, qcat_ref, acc_ref, *, n_chunks):
    n_tiles = qcat_ref.shape[0]
    tq = qt_ref.shape[2] // n_tiles
    cols = [slice(g * tq, (g + 1) * tq) for g in range(KV_GROUP)]
    shift_row = jnp.full((1, KV_GROUP * tq), -shift_ref[0], F32)
    for t in range(n_tiles):
        qcat_ref[t] = _query_matrix(qt_ref[0, :, t * tq:(t + 1) * tq], shift_row)
    acc_ref[...] = jnp.zeros(acc_ref.shape, F32)

    def scores(t, c, g):
        return _bdot(k_ref[0, 0, c * TM:(c + 1) * TM, :], qcat_ref[t, :, cols[g]])

    def absorb(t, c, g, s):
        acc_ref[t, :, cols[g]] += _bdot(vt_ref[0, 0, c], jnp.exp2(s).astype(BF16))

    units = [(t, c, g) for t in range(n_tiles) for c in range(n_chunks) for g in range(KV_GROUP)]
    pending = [scores(*u) for u in units[:ATTN_LEAD]]
    for idx, (t, c, g) in enumerate(units):
        if idx + ATTN_LEAD < len(units):
            pending.append(scores(*units[idx + ATTN_LEAD]))
        absorb(t, c, g, pending.pop(0))
        if c == n_chunks - 1 and g == KV_GROUP - 1:
            o_ref[0, t * tq:(t + 1) * tq, :] = _head_outputs(acc_ref[t], cols)


def _attn_online_body(trips_ref, shift_ref, qt_ref, k_ref, vt_ref, o_ref, qcat_ref, acc_ref, *bufs, n_chunks):
    tq = qt_ref.shape[2]
    cols = [slice(g * tq, (g + 1) * tq) for g in range(KV_GROUP)]
    qcat_ref[...] = _query_matrix(qt_ref[0], jnp.zeros((1, KV_GROUP * tq), F32))
    acc_ref[...] = jnp.zeros(acc_ref.shape, F32)

    def scores(c, dst_ref):
        kc = k_ref[0, 0, pl.ds(pl.multiple_of(c * TM, TM), TM), :]
        maxes = []
        for sl in cols:
            s = _bdot(kc, qcat_ref[:, sl])
            dst_ref[:, sl] = s
            maxes.append(jnp.max(s, axis=0, keepdims=True))
        return jnp.concatenate(maxes, axis=1)

    def absorb(c, src_ref, chunk_max, m_old):
        m_new = jnp.maximum(m_old, chunk_max)
        alpha = jnp.exp2(m_old - m_new)
        vt = vt_ref[0, 0, c]
        acc_ref[...] = alpha * acc_ref[...]
        for sl in cols:
            p = jnp.exp2(src_ref[:, sl] - m_new[:, sl]).astype(BF16)
            acc_ref[:, sl] += _bdot(vt, p)
        return m_new

    n_buf = len(bufs)

    def group(first, carry, last_group):
        m, chunk_max = carry
        for j in range(n_buf):
            c = first + j
            next_max = None if last_group and j + 1 == n_buf else scores(c + 1, bufs[(j + 1) % n_buf])
            m = absorb(c, bufs[j], chunk_max, m)
            chunk_max = next_max
        return m, chunk_max

    carry = (jnp.full((1, KV_GROUP * tq), NEG_BIG, F32), scores(0, bufs[0]))
    carry = lax.fori_loop(0, trips_ref[0], lambda i, cr: group(i * n_buf, cr, False), carry)
    group(n_chunks - n_buf, carry, True)
    o_ref[0] = _head_outputs(acc_ref[...], cols)


def _attn_call(body, unrolled, n_score_buffers, score_bound, qt, k, vt):
    b, _, s = qt.shape
    n_chunks = s // TM
    assert n_chunks % SCORE_BUFFERS == 0
    tq = min(TQ, s)
    mq = KV_GROUP * tq
    tiles_per_step = min(s // tq, max(1, MAX_UNITS_PER_STEP // (n_chunks * KV_GROUP))) if unrolled else 1
    step = tiles_per_step * tq
    assert s % step == 0
    lead = (tiles_per_step,) if unrolled else ()
    trips = jnp.full((1,), n_chunks // SCORE_BUFFERS - 1, jnp.int32)
    return pl.pallas_call(
        functools.partial(body, n_chunks=n_chunks),
        grid_spec=pltpu.PrefetchScalarGridSpec(
            num_scalar_prefetch=2,
            grid=(b, N_KV_HEADS, s // step),
            in_specs=[
                pl.BlockSpec((1, KV_GROUP * HEAD_DIM, step), lambda i, h, j, *_: (i, h, j)),
                pl.BlockSpec((1, 1, s, LANES), lambda i, h, j, *_: (i, h, 0, 0)),
                pl.BlockSpec((1, 1, n_chunks, V_ROWS, TM), lambda i, h, j, *_: (i, h, 0, 0, 0)),
            ],
            out_specs=pl.BlockSpec((1, step, KV_GROUP * HEAD_DIM), lambda i, h, j, *_: (i, j, h)),
            scratch_shapes=[
                pltpu.VMEM(lead + (LANES, mq), BF16),
                pltpu.VMEM(lead + (V_ROWS, mq), F32),
            ] + [pltpu.VMEM((TM, mq + LANES), F32)] * n_score_buffers,
        ),
        out_shape=jax.ShapeDtypeStruct((b, s, ATT_WIDTH), BF16),
        compiler_params=_cparams(("parallel", "parallel", "arbitrary")),
        name="attn",
    )(trips, score_bound, qt, k, vt)


def _attn(qt, k, vt, score_bound):
    return lax.cond(score_bound[0] <= MAX_FIXED_SHIFT,
                    functools.partial(_attn_call, _attn_shifted_body, True, 0),
                    functools.partial(_attn_call, _attn_online_body, False, SCORE_BUFFERS),
                    score_bound, qt, k, vt)


def _gla_direction(q_ref, k_ref, v_ref, la_ref, o_ref, state_ref, order, cum, keep, ref_idx, last_idx):
    heads = range(GLA_HEADS)
    psl = [slice(p * LANES, (p + 1) * LANES) for p in range(GLA_HEADS * GLA_DK // LANES)]
    vsl = [slice(h * GLA_DV, (h + 1) * GLA_DV) for h in heads]
    rows = [slice(ci * CHUNK, (ci + 1) * CHUNK) for ci in order]
    n = range(len(rows))

    b = []
    for sl in rows:
        la = la_ref[0, sl, :]
        la_hi = la.astype(BF16)
        la_lo = (la - la_hi.astype(F32)).astype(BF16)
        b.append(_bdot(cum, la_hi) + _bdot(cum, la_lo))

    qr, kr, kd, qb, b_lasts, v = [], [], [], [], [], []
    for sl, bc in zip(rows, b):
        b_ref = bc[ref_idx:ref_idx + 1]
        b_last = bc[last_idx:last_idx + 1]
        qs = q_ref[0, sl, :] * (GLA_DK ** -0.5)
        k = k_ref[0, sl, :]
        qr.append((qs * jnp.exp(bc - b_ref)).astype(BF16))
        kr.append((k * jnp.exp(b_ref - bc)).astype(BF16))
        kd.append(k * jnp.exp(b_last - bc))
        qb.append((qs * jnp.exp(bc)).astype(BF16))
        b_lasts.append(b_last)
        v.append(v_ref[0, sl, :])

    pairs = range(len(psl))
    lane = lax.broadcasted_iota(jnp.int32, (1, LANES), 1)
    own = [lane < GLA_DK, lane >= GLA_DK]
    keep2 = jnp.concatenate([keep, keep], axis=0)
    rr = lax.broadcasted_iota(jnp.int32, (LANES, 2 * GLA_DV), 0)
    cc = lax.broadcasted_iota(jnp.int32, (LANES, 2 * GLA_DV), 1)
    same_head = (rr < GLA_DK) == (cc < GLA_DV)
    a = [[None] * GLA_HEADS for _ in n]
    d_state = [[None] * len(psl) for _ in n]
    dec_col = [[None] * len(psl) for _ in n]
    for i in n:
        for p, sl in enumerate(psl):
            qr_p = qr[i][:, sl]
            stacked = jnp.concatenate([jnp.where(own[0], qr_p, 0), jnp.where(own[1], qr_p, 0)], axis=0)
            a2 = jnp.where(keep2, _nt_dot(stacked, kr[i][:, sl]), 0.0).astype(BF16)
            a[i][2 * p] = a2[:CHUNK]
            a[i][2 * p + 1] = a2[CHUNK:]
            both = jnp.concatenate([kd[i][:, sl], jnp.broadcast_to(b_lasts[i][:, sl], (CHUNK, LANES))], axis=0).T
            v_pair = v[i][:, p * 2 * GLA_DV:(p + 1) * 2 * GLA_DV]
            d_state[i][p] = jnp.where(same_head, _bdot(both[:, :CHUNK].astype(BF16), v_pair), 0.0)
            dec_col[i][p] = jnp.exp(both[:, CHUNK:CHUNK + 1])
    o_intra = [[_bdot(a[i][h], v[i][:, vsl[h]]) for h in heads] for i in n]

    state = [state_ref[p] for p in pairs]
    entering = []
    for i in n:
        entering.append(state)
        state = [dec_col[i][p] * state[p] + d_state[i][p] for p in pairs]
    for p in pairs:
        state_ref[p] = state[p]

    o_inter = [[_bdot(qb[i][:, psl[p]], entering[i][p].astype(BF16)) for p in pairs] for i in n]
    for i in n:
        o_ref[0, rows[i], :] = jnp.concatenate(o_intra[i], axis=1) + jnp.concatenate(o_inter[i], axis=1)


def _gla_body(qf_ref, kf_ref, vf_ref, laf_ref, qb_ref, kb_ref, vb_ref, lab_ref, of_ref, ob_ref, sf_ref, sb_ref,
              *, n_chunks):
    @pl.when(pl.program_id(1) == 0)
    def _():
        sf_ref[...] = jnp.zeros(sf_ref.shape, F32)
        sb_ref[...] = jnp.zeros(sb_ref.shape, F32)

    r = lax.broadcasted_iota(jnp.int32, (CHUNK, CHUNK), 0)
    c = lax.broadcasted_iota(jnp.int32, (CHUNK, CHUNK), 1)
    _gla_direction(qf_ref, kf_ref, vf_ref, laf_ref, of_ref, sf_ref, range(n_chunks),
                   jnp.where(c <= r, 1.0, 0.0).astype(BF16), c <= r, CHUNK // 2, CHUNK - 1)
    _gla_direction(qb_ref, kb_ref, vb_ref, lab_ref, ob_ref, sb_ref, range(n_chunks - 1, -1, -1),
                   jnp.where(c >= r, 1.0, 0.0).astype(BF16), c > r, CHUNK // 2 - 1, 0)


def _gla(ql, kl, vl, laf, lab):
    b, s, _ = ql.shape
    tb = min(GLA_BLOCK, s)
    nb = s // tb
    fwd = lambda w: pl.BlockSpec((1, tb, w), lambda i, j: (i, j, 0))
    bwd = lambda w: pl.BlockSpec((1, tb, w), lambda i, j: (i, nb - 1 - j, 0))
    return pl.pallas_call(
        functools.partial(_gla_body, n_chunks=tb // CHUNK),
        grid=(b, nb),
        in_specs=[fwd(GLA_KEY_WIDTH), fwd(GLA_KEY_WIDTH), fwd(GLA_WIDTH), fwd(GLA_KEY_WIDTH),
                  bwd(GLA_KEY_WIDTH), bwd(GLA_KEY_WIDTH), bwd(GLA_WIDTH), bwd(GLA_KEY_WIDTH)],
        out_specs=(fwd(GLA_WIDTH), bwd(GLA_WIDTH)),
        out_shape=(jax.ShapeDtypeStruct((b, s, GLA_WIDTH), F32), jax.ShapeDtypeStruct((b, s, GLA_WIDTH), F32)),
        scratch_shapes=[pltpu.VMEM((GLA_HEADS // 2, LANES, 2 * GLA_DV), F32),
                        pltpu.VMEM((GLA_HEADS // 2, LANES, 2 * GLA_DV), F32)],
        compiler_params=_cparams(("parallel", "arbitrary")),
        name="gla",
    )(ql, kl, vl, laf, ql, kl, vl, lab)


def _outproj_ffn_body(h_ref, oa_ref, of_ref, ob_ref, r_ref, gg_ref, wa_ref, wg_ref, g2_ref, wgu_ref, wd_ref, og_ref,
                      o_ref):
    o = of_ref[...] + ob_ref[...]
    gain = gg_ref[...]
    r = r_ref[...]
    parts = []
    for hh in range(GLA_HEADS):
        sl = slice(hh * GLA_DV, (hh + 1) * GLA_DV)
        rh = r[:, sl]
        parts.append((_rms_rows(o[:, sl], gain[:, sl]) * (rh * (1.0 / (1.0 + jnp.exp(-rh))))).astype(BF16))
    og = jnp.concatenate(parts, axis=1)
    h = h_ref[...] + _bdot(oa_ref[...], wa_ref[...]) + _bdot(og, wg_ref[...])
    o_ref[...] = _rms_rows(_swiglu_half_step(h, g2_ref[...], wgu_ref, wd_ref), og_ref[...])


def _outproj_ffn(h, o_att, o_f, o_b, r, p):
    t = h.shape[0]
    const = lambda i: (0, 0)
    tok = lambda w: pl.BlockSpec((TM, w), lambda i: (i, 0))
    return pl.pallas_call(
        _outproj_ffn_body,
        grid=(t // TM,),
        in_specs=[tok(D_MODEL), tok(ATT_WIDTH), tok(GLA_WIDTH), tok(GLA_WIDTH), tok(GLA_WIDTH),
                  pl.BlockSpec((1, GLA_WIDTH), const),
                  _resident((ATT_WIDTH, D_MODEL)),
                  _resident((GLA_WIDTH, D_MODEL)),
                  pl.BlockSpec((1, D_MODEL), const),
                  _resident((D_MODEL, 2 * D_FF)),
                  _resident((D_FF, D_MODEL)),
                  pl.BlockSpec((1, D_MODEL), const)],
        out_specs=tok(D_MODEL),
        out_shape=jax.ShapeDtypeStruct((t, D_MODEL), F32),
        compiler_params=_cparams(("parallel",)),
        name="outproj_ffn",
    )(h, o_att, o_f, o_b, r, p["gla_gain"], p["w_out_att"], p["w_out_gla"],
      p["norm_ffn2"], p["w_ffn2_gu"], p["w_ffn2_down"], p["norm_out"])


def _rope_tables(seq_len):
    rows = seq_len // GRID_W
    row = jnp.repeat(jnp.arange(rows, dtype=F32), GRID_W)
    col = jnp.tile(jnp.arange(GRID_W, dtype=F32), rows)
    inv_freq = 1.0 / (ROPE_THETA ** (jnp.arange(0, AXIS_DIM, 2, dtype=F32) / AXIS_DIM))
    ang_r = row[:, None] * inv_freq[None, :]
    ang_c = col[:, None] * inv_freq[None, :]
    ang = jnp.concatenate([ang_r, ang_r, ang_c, ang_c], axis=-1)
    sign = jnp.where((jnp.arange(HEAD_DIM) % 32) < 16, -1.0, 1.0).astype(F32)
    cos, sin = jnp.cos(ang), jnp.sin(ang) * sign[None, :]
    return {
        "cos_t": cos.T, "sin_t": sin.T,
        "cos_n": jnp.tile(cos, (1, N_KV_HEADS)), "sin_n": jnp.tile(sin, (1, N_KV_HEADS)),
    }


def _prep_layer(l, norm_ffn1, w_ffn1_gu, w_ffn1_down, norm_mix, w_in, q_norm, k_norm, w_gate_f, b_gate_f,
                w_gate_b, b_gate_b, gla_norm, w_out, norm_ffn2, w_ffn2_gu, w_ffn2_down, norm_out):
    w = w_in[l]
    o_q, o_k, o_v = 0, ATT_WIDTH, ATT_WIDTH + KV_W
    o_rest = ATT_WIDTH + 2 * KV_W
    n_rest = 2 * GLA_KEY_WIDTH + 2 * GLA_WIDTH
    gates = jnp.pad(w[:, o_rest + n_rest:], ((0, 0), (0, LANES - 2 * GATE_RANK)))
    w_rest = jnp.concatenate([w[:, o_k:o_v], w[:, o_rest:o_rest + n_rest], gates], axis=1).astype(BF16)
    w_qv_t = jnp.concatenate([w[:, o_q:o_k], w[:, o_v:o_rest]], axis=1).T.astype(BF16)
    w_gate = jnp.zeros((LANES, 2 * GLA_KEY_WIDTH), F32)
    w_gate = w_gate.at[:GATE_RANK, :GLA_KEY_WIDTH].set(w_gate_f[l])
    w_gate = w_gate.at[GATE_RANK:2 * GATE_RANK, GLA_KEY_WIDTH:].set(w_gate_b[l])
    hid = jnp.arange(KV_W) // HEAD_DIM
    grp_avg = jnp.where(hid[:, None] == hid[None, :], 1.0 / HEAD_DIM, 0.0).astype(BF16)
    row = lambda v: v.reshape(1, -1).astype(F32)
    return {
        "norm_ffn1": row(norm_ffn1[l]), "w_ffn1_gu": w_ffn1_gu[l].astype(BF16), "w_ffn1_down": w_ffn1_down[l].astype(BF16),
        "norm_mix": row(norm_mix[l]), "w_qv_t": w_qv_t, "w_rest": w_rest,
        "q_gain_col": q_norm[l].reshape(HEAD_DIM, 1).astype(F32),
        "k_gain_row": row(jnp.tile(k_norm[l], N_KV_HEADS)),
        "grp_avg": grp_avg, "w_gate": w_gate.astype(BF16),
        "score_bound": (HEAD_DIM ** 0.5 * LOG2E * jnp.max(jnp.abs(q_norm[l])) * jnp.max(jnp.abs(k_norm[l]))
                        ).reshape(1).astype(F32),
        "b_gate": row(jnp.concatenate([b_gate_f[l], b_gate_b[l]])),
        "gla_gain": row(jnp.tile(gla_norm[l], GLA_HEADS)),
        "w_out_att": w_out[l][:ATT_WIDTH].astype(BF16), "w_out_gla": w_out[l][ATT_WIDTH:].astype(BF16),
        "norm_ffn2": row(norm_ffn2[l]), "w_ffn2_gu": w_ffn2_gu[l].astype(BF16), "w_ffn2_down": w_ffn2_down[l].astype(BF16),
        "norm_out": row(norm_out[l]),
    }


def _layer(x, p, tabs):
    b, s, d = x.shape
    t = b * s
    h, qt, k, vt, ql, kl, vl, rl, laf, lab = _ffn_inproj(x, p, tabs)
    o_att = _attn(qt, k, vt, p["score_bound"])
    o_f, o_b = _gla(ql, kl, vl, laf, lab)
    y = _outproj_ffn(h.reshape(t, d), o_att.reshape(t, ATT_WIDTH), o_f.reshape(t, GLA_WIDTH),
                     o_b.reshape(t, GLA_WIDTH), rl.reshape(t, GLA_WIDTH), p)
    return y.reshape(b, s, d)


def _trunk(x, layers):
    tabs = _rope_tables(x.shape[1])
    h = x
    for p in layers:
        h = _layer(h, p, tabs)
    return h


def kernel(x_prompt, x_sample, norm_ffn1, w_ffn1_gu, w_ffn1_down, norm_mix, w_in, q_norm, k_norm, w_gate_f, b_gate_f,
           w_gate_b, b_gate_b, gla_norm, w_out, norm_ffn2, w_ffn2_gu, w_ffn2_down, norm_out):
    params = (norm_ffn1, w_ffn1_gu, w_ffn1_down, norm_mix, w_in, q_norm, k_norm, w_gate_f, b_gate_f,
              w_gate_b, b_gate_b, gla_norm, w_out, norm_ffn2, w_ffn2_gu, w_ffn2_down, norm_out)
    layers = [_prep_layer(l, *params) for l in range(norm_ffn1.shape[0])]
    return (_trunk(x_prompt, layers), _trunk(x_sample, layers))
```

```python
import functools

import jax
import jax.numpy as jnp
from jax import lax
from jax.experimental import pallas as pl
from jax.experimental.pallas import tpu as pltpu

F32 = jnp.float32
BF16 = jnp.bfloat16

D_MODEL = 1024
GRID_W = 64
ATT_WIDTH = 512
HEAD_DIM = 64
N_HEADS = 8
N_KV_HEADS = 2
KV_GROUP = 4
AXIS_DIM = 32
ROPE_THETA = 10000.0
GLA_WIDTH = 512
GLA_HEADS = 4
GLA_DV = 128
GLA_DK = 64
GLA_KEY_WIDTH = 256
GATE_RANK = 16
GATE_TAU = 16.0
CHUNK = 64
D_FF = 2816
EPS = 1e-6

LANES = 128
KV_W = N_KV_HEADS * HEAD_DIM
QV_ROWS = ATT_WIDTH + KV_W
REST_W = KV_W + 2 * GLA_KEY_WIDTH + 2 * GLA_WIDTH + LANES
GATE_OFF = REST_W - LANES

TM = 512
TQ = 256
MAX_UNITS_PER_STEP = 256
ATTN_LEAD = 2
MAX_FIXED_SHIFT = 48.0
SCORE_BUFFERS = 4
FF_CHUNK = 256
N_FF_CHUNKS = D_FF // FF_CHUNK
GLA_BLOCK = 512
NEG_BIG = -1e30
LOG2E = 1.4426950408889634
V_ROWS = 2 * HEAD_DIM
VMEM_LIMIT = 56 * 1024 * 1024


def _cparams(sem):
    return pltpu.CompilerParams(dimension_semantics=sem, vmem_limit_bytes=VMEM_LIMIT)


def _rms_rows(x, gain_row):
    ms = jnp.mean(x * x, axis=-1, keepdims=True)
    return x * lax.rsqrt(ms + EPS) * gain_row


def _bdot(a, b):
    return jnp.dot(a, b, preferred_element_type=F32)


def _nt_dot(a, b):
    return lax.dot_general(a, b, (((1,), (1,)), ((), ())), preferred_element_type=F32)


def _swiglu_half_step(x, gain_row, wgu_ref, wd_ref):
    xn = _rms_rows(x, gain_row).astype(BF16)
    acc = None
    for c in range(N_FF_CHUNKS):
        g = _bdot(xn, wgu_ref[:, c * FF_CHUNK:(c + 1) * FF_CHUNK])
        u = _bdot(xn, wgu_ref[:, D_FF + c * FF_CHUNK:D_FF + (c + 1) * FF_CHUNK])
        a = (g * (1.0 / (1.0 + jnp.exp(-g))) * u).astype(BF16)
        d = _bdot(a, wd_ref[c * FF_CHUNK:(c + 1) * FF_CHUNK, :])
        acc = d if acc is None else acc + d
    return x + 0.5 * acc


def _resident(shape):
    return pl.BlockSpec(shape, lambda *_: (0,) * len(shape), pipeline_mode=pl.Buffered(1))


def _swap16(t, axis):
    n = t.shape[axis] // 16
    parts = [lax.slice_in_dim(t, (i ^ 1) * 16, (i ^ 1) * 16 + 16, axis=axis) for i in range(n)]
    return jnp.concatenate(parts, axis=axis)


def _ffn_inproj_body(x_ref, g1_ref, wgu_ref, wd_ref, g_ref, wqv_ref, wrest_ref, qg_ref, kg_ref,
                     cost_ref, sint_ref, cosn_ref, sinn_ref, grp_ref, wgate_ref, bgate_ref,
                     h_ref, qt_ref, k_ref, vt_ref, ql_ref, kl_ref, vl_ref, rl_ref, laf_ref, lab_ref):
    h = _swiglu_half_step(x_ref[0], g1_ref[...], wgu_ref, wd_ref)
    h_ref[0] = h
    xn = _rms_rows(h, g_ref[...]).astype(BF16)
    tm = xn.shape[0]

    ut = _nt_dot(wqv_ref[...], xn)
    q = ut[:ATT_WIDTH].reshape(N_HEADS, HEAD_DIM, tm)
    q = q * lax.rsqrt(jnp.mean(q * q, axis=1, keepdims=True) + EPS) * qg_ref[...][None]
    q = q * cost_ref[...][None] + _swap16(q, 1) * sint_ref[...][None]
    qt_ref[0] = (q * (HEAD_DIM ** -0.5 * LOG2E)).reshape(ATT_WIDTH, tm).astype(BF16)

    row = lax.broadcasted_iota(jnp.int32, (V_ROWS - HEAD_DIM, tm), 0)
    ones_rows = jnp.where(row == 0, 1.0, 0.0).astype(BF16)
    for kvh in range(N_KV_HEADS):
        v = ut[ATT_WIDTH + kvh * HEAD_DIM:ATT_WIDTH + (kvh + 1) * HEAD_DIM].astype(BF16)
        vt_ref[0, kvh, 0] = jnp.concatenate([v, ones_rows], axis=0)

    rest = _bdot(xn, wrest_ref[...])

    k = rest[:, :KV_W]
    ss = k * k
    ss_hi = ss.astype(BF16)
    ss_lo = (ss - ss_hi.astype(F32)).astype(BF16)
    ms = _bdot(ss_hi, grp_ref[...]) + _bdot(ss_lo, grp_ref[...])
    k = k * lax.rsqrt(ms + EPS) * kg_ref[...]
    lane = lax.broadcasted_iota(jnp.int32, k.shape, 1)
    k_sw = jnp.where((lane % 32) < 16, pltpu.roll(k, LANES - 16, 1), pltpu.roll(k, 16, 1))
    k = k * cosn_ref[...] + k_sw * sinn_ref[...]
    tail = jnp.where(lane == HEAD_DIM, 1.0, 0.0)
    k_ref[0, 0] = jnp.where(lane < HEAD_DIM, k, tail).astype(BF16)
    k_ref[0, 1] = jnp.where(lane < HEAD_DIM, pltpu.roll(k, HEAD_DIM, 1), tail).astype(BF16)

    o = KV_W
    ql_ref[0] = rest[:, o:o + GLA_KEY_WIDTH]
    o += GLA_KEY_WIDTH
    kl_ref[0] = rest[:, o:o + GLA_KEY_WIDTH]
    o += GLA_KEY_WIDTH
    vl_ref[0] = rest[:, o:o + GLA_WIDTH].astype(BF16)
    o += GLA_WIDTH
    rl_ref[0] = rest[:, o:o + GLA_WIDTH]

    pre = _bdot(rest[:, GATE_OFF:].astype(BF16), wgate_ref[...]) + bgate_ref[...]
    logsig = jnp.minimum(pre, 0.0) - jnp.log(1.0 + jnp.exp(-jnp.abs(pre)))
    la = logsig * (1.0 / GATE_TAU)
    laf_ref[0] = la[:, :GLA_KEY_WIDTH]
    lab_ref[0] = la[:, GLA_KEY_WIDTH:]


def _ffn_inproj(x, p, tabs):
    b, s, _ = x.shape
    nt = s // TM
    c2 = lambda i, j: (0, 0)
    tok = lambda w: pl.BlockSpec((1, TM, w), lambda i, j: (i, j, 0))
    out_shape = (
        jax.ShapeDtypeStruct((b, s, D_MODEL), F32),
        jax.ShapeDtypeStruct((b, ATT_WIDTH, s), BF16),
        jax.ShapeDtypeStruct((b, N_KV_HEADS, s, LANES), BF16),
        jax.ShapeDtypeStruct((b, N_KV_HEADS, nt, V_ROWS, TM), BF16),
        jax.ShapeDtypeStruct((b, s, GLA_KEY_WIDTH), F32),
        jax.ShapeDtypeStruct((b, s, GLA_KEY_WIDTH), F32),
        jax.ShapeDtypeStruct((b, s, GLA_WIDTH), BF16),
        jax.ShapeDtypeStruct((b, s, GLA_WIDTH), F32),
        jax.ShapeDtypeStruct((b, s, GLA_KEY_WIDTH), F32),
        jax.ShapeDtypeStruct((b, s, GLA_KEY_WIDTH), F32),
    )
    out_specs = (
        tok(D_MODEL),
        pl.BlockSpec((1, ATT_WIDTH, TM), lambda i, j: (i, 0, j)),
        pl.BlockSpec((1, N_KV_HEADS, TM, LANES), lambda i, j: (i, 0, j, 0)),
        pl.BlockSpec((1, N_KV_HEADS, 1, V_ROWS, TM), lambda i, j: (i, 0, j, 0, 0)),
        tok(GLA_KEY_WIDTH), tok(GLA_KEY_WIDTH), tok(GLA_WIDTH), tok(GLA_WIDTH), tok(GLA_KEY_WIDTH),
        tok(GLA_KEY_WIDTH),
    )
    in_specs = [
        tok(D_MODEL),
        pl.BlockSpec((1, D_MODEL), c2),
        _resident((D_MODEL, 2 * D_FF)),
        _resident((D_FF, D_MODEL)),
        pl.BlockSpec((1, D_MODEL), c2),
        _resident((QV_ROWS, D_MODEL)),
        _resident((D_MODEL, REST_W)),
        pl.BlockSpec((HEAD_DIM, 1), c2),
        pl.BlockSpec((1, KV_W), c2),
        pl.BlockSpec((HEAD_DIM, TM), lambda i, j: (0, j)),
        pl.BlockSpec((HEAD_DIM, TM), lambda i, j: (0, j)),
        pl.BlockSpec((TM, KV_W), lambda i, j: (j, 0)),
        pl.BlockSpec((TM, KV_W), lambda i, j: (j, 0)),
        pl.BlockSpec((KV_W, KV_W), c2),
        pl.BlockSpec((LANES, 2 * GLA_KEY_WIDTH), c2),
        pl.BlockSpec((1, 2 * GLA_KEY_WIDTH), c2),
    ]
    return pl.pallas_call(
        _ffn_inproj_body,
        grid=(b, nt),
        in_specs=in_specs,
        out_specs=out_specs,
        out_shape=out_shape,
        compiler_params=_cparams(("parallel", "parallel")),
        name="ffn_inproj",
    )(x, p["norm_ffn1"], p["w_ffn1_gu"], p["w_ffn1_down"],
      p["norm_mix"], p["w_qv_t"], p["w_rest"], p["q_gain_col"], p["k_gain_row"],
      tabs["cos_t"], tabs["sin_t"], tabs["cos_n"], tabs["sin_n"], p["grp_avg"], p["w_gate"], p["b_gate"])


def _query_matrix(qblk, extra_row):
    top = jnp.concatenate([qblk[g * HEAD_DIM:(g + 1) * HEAD_DIM] for g in range(KV_GROUP)], axis=1)
    row = lax.broadcasted_iota(jnp.int32, top.shape, 0)
    bottom = jnp.where(row == 0, extra_row, 0.0).astype(BF16)
    return jnp.concatenate([top, bottom], axis=0)


def _head_outputs(acc, cols):
    out_t = acc[:HEAD_DIM] * (1.0 / acc[HEAD_DIM:HEAD_DIM + 1])
    heads = [out_t[:, sl].T for sl in cols]
    return jnp.concatenate(heads, axis=1).astype(BF16)


def _attn_shifted_body(trips_ref, shift_ref, qt_ref, k_ref, vt_ref, o_ref, qcat_ref, acc_ref, *, n_chunks):
    n_tiles = qcat_ref.shape[0]
    tq = qt_ref.shape[2] // n_tiles
    cols = [slice(g * tq, (g + 1) * tq) for g in range(KV_GROUP)]
    shift_row = jnp.full((1, KV_GROUP * tq), -shift_ref[0], F32)
    for t in range(n_tiles):
        qcat_ref[t] = _query_matrix(qt_ref[0, :, t * tq:(t + 1) * tq], shift_row)
    acc_ref[...] = jnp.zeros(acc_ref.shape, F32)

    def scores(t, c, g):
        return _bdot(k_ref[0, 0, c * TM:(c + 1) * TM, :], qcat_ref[t, :, cols[g]])

    def absorb(t, c, g, s):
        acc_ref[t, :, cols[g]] += _bdot(vt_ref[0, 0, c], jnp.exp2(s).astype(BF16))

    units = [(t, c, g) for t in range(n_tiles) for c in range(n_chunks) for g in range(KV_GROUP)]
    pending = [scores(*u) for u in units[:ATTN_LEAD]]
    for idx, (t, c, g) in enumerate(units):
        if idx + ATTN_LEAD < len(units):
            pending.append(scores(*units[idx + ATTN_LEAD]))
        absorb(t, c, g, pending.pop(0))
        if c == n_chunks - 1 and g == KV_GROUP - 1:
            o_ref[0, t * tq:(t + 1) * tq, :] = _head_outputs(acc_ref[t], cols)


def _attn_online_body(trips_ref, shift_ref, qt_ref, k_ref, vt_ref, o_ref, qcat_ref, acc_ref, *bufs, n_chunks):
    tq = qt_ref.shape[2]
    cols = [slice(g * tq, (g + 1) * tq) for g in range(KV_GROUP)]
    qcat_ref[...] = _query_matrix(qt_ref[0], jnp.zeros((1, KV_GROUP * tq), F32))
    acc_ref[...] = jnp.zeros(acc_ref.shape, F32)

    def scores(c, dst_ref):
        kc = k_ref[0, 0, pl.ds(pl.multiple_of(c * TM, TM), TM), :]
        maxes = []
        for sl in cols:
            s = _bdot(kc, qcat_ref[:, sl])
            dst_ref[:, sl] = s
            maxes.append(jnp.max(s, axis=0, keepdims=True))
        return jnp.concatenate(maxes, axis=1)

    def absorb(c, src_ref, chunk_max, m_old):
        m_new = jnp.maximum(m_old, chunk_max)
        alpha = jnp.exp2(m_old - m_new)
        vt = vt_ref[0, 0, c]
        acc_ref[...] = alpha * acc_ref[...]
        for sl in cols:
            p = jnp.exp2(src_ref[:, sl] - m_new[:, sl]).astype(BF16)
            acc_ref[:, sl] += _bdot(vt, p)
        return m_new

    n_buf = len(bufs)

    def group(first, carry, last_group):
        m, chunk_max = carry
        for j in range(n_buf):
            c = first + j
            next_max = None if last_group and j + 1 == n_buf else scores(c + 1, bufs[(j + 1) % n_buf])
            m = absorb(c, bufs[j], chunk_max, m)
            chunk_max = next_max
        return m, chunk_max

    carry = (jnp.full((1, KV_GROUP * tq), NEG_BIG, F32), scores(0, bufs[0]))
    carry = lax.fori_loop(0, trips_ref[0], lambda i, cr: group(i * n_buf, cr, False), carry)
    group(n_chunks - n_buf, carry, True)
    o_ref[0] = _head_outputs(acc_ref[...], cols)


def _attn_call(body, unrolled, n_score_buffers, score_bound, qt, k, vt):
    b, _, s = qt.shape
    n_chunks = s // TM
    assert n_chunks % SCORE_BUFFERS == 0
    tq = min(TQ, s)
    mq = KV_GROUP * tq
    tiles_per_step = min(s // tq, max(1, MAX_UNITS_PER_STEP // (n_chunks * KV_GROUP))) if unrolled else 1
    step = tiles_per_step * tq
    assert s % step == 0
    lead = (tiles_per_step,) if unrolled else ()
    trips = jnp.full((1,), n_chunks // SCORE_BUFFERS - 1, jnp.int32)
    return pl.pallas_call(
        functools.partial(body, n_chunks=n_chunks),
        grid_spec=pltpu.PrefetchScalarGridSpec(
            num_scalar_prefetch=2,
            grid=(b, N_KV_HEADS, s // step),
            in_specs=[
                pl.BlockSpec((1, KV_GROUP * HEAD_DIM, step), lambda i, h, j, *_: (i, h, j)),
                pl.BlockSpec((1, 1, s, LANES), lambda i, h, j, *_: (i, h, 0, 0)),
                pl.BlockSpec((1, 1, n_chunks, V_ROWS, TM), lambda i, h, j, *_: (i, h, 0, 0, 0)),
            ],
            out_specs=pl.BlockSpec((1, step, KV_GROUP * HEAD_DIM), lambda i, h, j, *_: (i, j, h)),
            scratch_shapes=[
                pltpu.VMEM(lead + (LANES, mq), BF16),
                pltpu.VMEM(lead + (V_ROWS, mq), F32),
            ] + [pltpu.VMEM((TM, mq + LANES), F32)] * n_score_buffers,
        ),
        out_shape=jax.ShapeDtypeStruct((b, s, ATT_WIDTH), BF16),
        compiler_params=_cparams(("parallel", "parallel", "arbitrary")),
        name="attn",
    )(trips, score_bound, qt, k, vt)


def _attn(qt, k, vt, score_bound):
    return lax.cond(score_bound[0] <= MAX_FIXED_SHIFT,
                    functools.partial(_attn_call, _attn_shifted_body, True, 0),
                    functools.partial(_attn_call, _attn_online_body, False, SCORE_BUFFERS),
                    score_bound, qt, k, vt)


def _gla_direction(q_ref, k_ref, v_ref, la_ref, o_ref, state_ref, order, cum, keep, ref_idx, last_idx):
    heads = range(GLA_HEADS)
    psl = [slice(p * LANES, (p + 1) * LANES) for p in range(GLA_HEADS * GLA_DK // LANES)]
    vsl = [slice(h * GLA_DV, (h + 1) * GLA_DV) for h in heads]
    rows = [slice(ci * CHUNK, (ci + 1) * CHUNK) for ci in order]
    n = range(len(rows))

    b = []
    for sl in rows:
        la = la_ref[0, sl, :]
        la_hi = la.astype(BF16)
        la_lo = (la - la_hi.astype(F32)).astype(BF16)
        b.append(_bdot(cum, la_hi) + _bdot(cum, la_lo))

    qr, kr, kd, qb, b_lasts, v = [], [], [], [], [], []
    for sl, bc in zip(rows, b):
        b_ref = bc[ref_idx:ref_idx + 1]
        b_last = bc[last_idx:last_idx + 1]
        qs = q_ref[0, sl, :] * (GLA_DK ** -0.5)
        k = k_ref[0, sl, :]
        qr.append((qs * jnp.exp(bc - b_ref)).astype(BF16))
        kr.append((k * jnp.exp(b_ref - bc)).astype(BF16))
        kd.append(k * jnp.exp(b_last - bc))
        qb.append((qs * jnp.exp(bc)).astype(BF16))
        b_lasts.append(b_last)
        v.append(v_ref[0, sl, :])

    pairs = range(len(psl))
    lane = lax.broadcasted_iota(jnp.int32, (1, LANES), 1)
    own = [lane < GLA_DK, lane >= GLA_DK]
    keep2 = jnp.concatenate([keep, keep], axis=0)
    rr = lax.broadcasted_iota(jnp.int32, (LANES, 2 * GLA_DV), 0)
    cc = lax.broadcasted_iota(jnp.int32, (LANES, 2 * GLA_DV), 1)
    same_head = (rr < GLA_DK) == (cc < GLA_DV)
    a = [[None] * GLA_HEADS for _ in n]
    d_state = [[None] * len(psl) for _ in n]
    dec_col = [[None] * len(psl) for _ in n]
    for i in n:
        for p, sl in enumerate(psl):
            qr_p = qr[i][:, sl]
            stacked = jnp.concatenate([jnp.where(own[0], qr_p, 0), jnp.where(own[1], qr_p, 0)], axis=0)
            a2 = jnp.where(keep2, _nt_dot(stacked, kr[i][:, sl]), 0.0).astype(BF16)
            a[i][2 * p] = a2[:CHUNK]
            a[i][2 * p + 1] = a2[CHUNK:]
            both = jnp.concatenate([kd[i][:, sl], jnp.broadcast_to(b_lasts[i][:, sl], (CHUNK, LANES))], axis=0).T
            v_pair = v[i][:, p * 2 * GLA_DV:(p + 1) * 2 * GLA_DV]
            d_state[i][p] = jnp.where(same_head, _bdot(both[:, :CHUNK].astype(BF16), v_pair), 0.0)
            dec_col[i][p] = jnp.exp(both[:, CHUNK:CHUNK + 1])
    o_intra = [[_bdot(a[i][h], v[i][:, vsl[h]]) for h in heads] for i in n]

    state = [state_ref[p] for p in pairs]
    entering = []
    for i in n:
        entering.append(state)
        state = [dec_col[i][p] * state[p] + d_state[i][p] for p in pairs]
    for p in pairs:
        state_ref[p] = state[p]

    o_inter = [[_bdot(qb[i][:, psl[p]], entering[i][p].astype(BF16)) for p in pairs] for i in n]
    for i in n:
        o_ref[0, rows[i], :] = jnp.concatenate(o_intra[i], axis=1) + jnp.concatenate(o_inter[i], axis=1)


def _gla_body(qf_ref, kf_ref, vf_ref, laf_ref, qb_ref, kb_ref, vb_ref, lab_ref, of_ref, ob_ref, sf_ref, sb_ref,
              *, n_chunks):
    @pl.when(pl.program_id(1) == 0)
    def _():
        sf_ref[...] = jnp.zeros(sf_ref.shape, F32)
        sb_ref[...] = jnp.zeros(sb_ref.shape, F32)

    r = lax.broadcasted_iota(jnp.int32, (CHUNK, CHUNK), 0)
    c = lax.broadcasted_iota(jnp.int32, (CHUNK, CHUNK), 1)
    _gla_direction(qf_ref, kf_ref, vf_ref, laf_ref, of_ref, sf_ref, range(n_chunks),
                   jnp.where(c <= r, 1.0, 0.0).astype(BF16), c <= r, CHUNK // 2, CHUNK - 1)
    _gla_direction(qb_ref, kb_ref, vb_ref, lab_ref, ob_ref, sb_ref, range(n_chunks - 1, -1, -1),
                   jnp.where(c >= r, 1.0, 0.0).astype(BF16), c > r, CHUNK // 2 - 1, 0)


def _gla(ql, kl, vl, laf, lab):
    b, s, _ = ql.shape
    tb = min(GLA_BLOCK, s)
    nb = s // tb
    fwd = lambda w: pl.BlockSpec((1, tb, w), lambda i, j: (i, j, 0))
    bwd = lambda w: pl.BlockSpec((1, tb, w), lambda i, j: (i, nb - 1 - j, 0))
    return pl.pallas_call(
        functools.partial(_gla_body, n_chunks=tb // CHUNK),
        grid=(b, nb),
        in_specs=[fwd(GLA_KEY_WIDTH), fwd(GLA_KEY_WIDTH), fwd(GLA_WIDTH), fwd(GLA_KEY_WIDTH),
                  bwd(GLA_KEY_WIDTH), bwd(GLA_KEY_WIDTH), bwd(GLA_WIDTH), bwd(GLA_KEY_WIDTH)],
        out_specs=(fwd(GLA_WIDTH), bwd(GLA_WIDTH)),
        out_shape=(jax.ShapeDtypeStruct((b, s, GLA_WIDTH), F32), jax.ShapeDtypeStruct((b, s, GLA_WIDTH), F32)),
        scratch_shapes=[pltpu.VMEM((GLA_HEADS // 2, LANES, 2 * GLA_DV), F32),
                        pltpu.VMEM((GLA_HEADS // 2, LANES, 2 * GLA_DV), F32)],
        compiler_params=_cparams(("parallel", "arbitrary")),
        name="gla",
    )(ql, kl, vl, laf, ql, kl, vl, lab)


def _outproj_ffn_body(h_ref, oa_ref, of_ref, ob_ref, r_ref, gg_ref, wa_ref, wg_ref, g2_ref, wgu_ref, wd_ref, og_ref,
                      o_ref):
    o = of_ref[...] + ob_ref[...]
    gain = gg_ref[...]
    r = r_ref[...]
    parts = []
    for hh in range(GLA_HEADS):
        sl = slice(hh * GLA_DV, (hh + 1) * GLA_DV)
        rh = r[:, sl]
        parts.append((_rms_rows(o[:, sl], gain[:, sl]) * (rh * (1.0 / (1.0 + jnp.exp(-rh))))).astype(BF16))
    og = jnp.concatenate(parts, axis=1)
    mixed = jnp.concatenate([oa_ref[...], og], axis=1)
    h = h_ref[...] + _bdot(mixed, jnp.concatenate([wa_ref[...], wg_ref[...]], axis=0))
    o_ref[...] = _rms_rows(_swiglu_half_step(h, g2_ref[...], wgu_ref, wd_ref), og_ref[...])


def _outproj_ffn(h, o_att, o_f, o_b, r, p):
    t = h.shape[0]
    const = lambda i: (0, 0)
    tok = lambda w: pl.BlockSpec((TM, w), lambda i: (i, 0))
    return pl.pallas_call(
        _outproj_ffn_body,
        grid=(t // TM,),
        in_specs=[tok(D_MODEL), tok(ATT_WIDTH), tok(GLA_WIDTH), tok(GLA_WIDTH), tok(GLA_WIDTH),
                  pl.BlockSpec((1, GLA_WIDTH), const),
                  _resident((ATT_WIDTH, D_MODEL)),
                  _resident((GLA_WIDTH, D_MODEL)),
                  pl.BlockSpec((1, D_MODEL), const),
                  _resident((D_MODEL, 2 * D_FF)),
                  _resident((D_FF, D_MODEL)),
                  pl.BlockSpec((1, D_MODEL), const)],
        out_specs=tok(D_MODEL),
        out_shape=jax.ShapeDtypeStruct((t, D_MODEL), F32),
        compiler_params=_cparams(("parallel",)),
        name="outproj_ffn",
    )(h, o_att, o_f, o_b, r, p["gla_gain"], p["w_out_att"], p["w_out_gla"],
      p["norm_ffn2"], p["w_ffn2_gu"], p["w_ffn2_down"], p["norm_out"])


def _rope_tables(seq_len):
    rows = seq_len // GRID_W
    row = jnp.repeat(jnp.arange(rows, dtype=F32), GRID_W)
    col = jnp.tile(jnp.arange(GRID_W, dtype=F32), rows)
    inv_freq = 1.0 / (ROPE_THETA ** (jnp.arange(0, AXIS_DIM, 2, dtype=F32) / AXIS_DIM))
    ang_r = row[:, None] * inv_freq[None, :]
    ang_c = col[:, None] * inv_freq[None, :]
    ang = jnp.concatenate([ang_r, ang_r, ang_c, ang_c], axis=-1)
    sign = jnp.where((jnp.arange(HEAD_DIM) % 32) < 16, -1.0, 1.0).astype(F32)
    cos, sin = jnp.cos(ang), jnp.sin(ang) * sign[None, :]
    return {
        "cos_t": cos.T, "sin_t": sin.T,
        "cos_n": jnp.tile(cos, (1, N_KV_HEADS)), "sin_n": jnp.tile(sin, (1, N_KV_HEADS)),
    }


def _prep_layer(l, norm_ffn1, w_ffn1_gu, w_ffn1_down, norm_mix, w_in, q_norm, k_norm, w_gate_f, b_gate_f,
                w_gate_b, b_gate_b, gla_norm, w_out, norm_ffn2, w_ffn2_gu, w_ffn2_down, norm_out):
    w = w_in[l]
    o_q, o_k, o_v = 0, ATT_WIDTH, ATT_WIDTH + KV_W
    o_rest = ATT_WIDTH + 2 * KV_W
    n_rest = 2 * GLA_KEY_WIDTH + 2 * GLA_WIDTH
    gates = jnp.pad(w[:, o_rest + n_rest:], ((0, 0), (0, LANES - 2 * GATE_RANK)))
    w_rest = jnp.concatenate([w[:, o_k:o_v], w[:, o_rest:o_rest + n_rest], gates], axis=1).astype(BF16)
    w_qv_t = jnp.concatenate([w[:, o_q:o_k], w[:, o_v:o_rest]], axis=1).T.astype(BF16)
    w_gate = jnp.zeros((LANES, 2 * GLA_KEY_WIDTH), F32)
    w_gate = w_gate.at[:GATE_RANK, :GLA_KEY_WIDTH].set(w_gate_f[l])
    w_gate = w_gate.at[GATE_RANK:2 * GATE_RANK, GLA_KEY_WIDTH:].set(w_gate_b[l])
    hid = jnp.arange(KV_W) // HEAD_DIM
    grp_avg = jnp.where(hid[:, None] == hid[None, :], 1.0 / HEAD_DIM, 0.0).astype(BF16)
    row = lambda v: v.reshape(1, -1).astype(F32)
    return {
        "norm_ffn1": row(norm_ffn1[l]), "w_ffn1_gu": w_ffn1_gu[l].astype(BF16), "w_ffn1_down": w_ffn1_down[l].astype(BF16),
        "norm_mix": row(norm_mix[l]), "w_qv_t": w_qv_t, "w_rest": w_rest,
        "q_gain_col": q_norm[l].reshape(HEAD_DIM, 1).astype(F32),
        "k_gain_row": row(jnp.tile(k_norm[l], N_KV_HEADS)),
        "grp_avg": grp_avg, "w_gate": w_gate.astype(BF16),
        "score_bound": (HEAD_DIM ** 0.5 * LOG2E * jnp.max(jnp.abs(q_norm[l])) * jnp.max(jnp.abs(k_norm[l]))
                        ).reshape(1).astype(F32),
        "b_gate": row(jnp.concatenate([b_gate_f[l], b_gate_b[l]])),
        "gla_gain": row(jnp.tile(gla_norm[l], GLA_HEADS)),
        "w_out_att": w_out[l][:ATT_WIDTH].astype(BF16), "w_out_gla": w_out[l][ATT_WIDTH:].astype(BF16),
        "norm_ffn2": row(norm_ffn2[l]), "w_ffn2_gu": w_ffn2_gu[l].astype(BF16), "w_ffn2_down": w_ffn2_down[l].astype(BF16),
        "norm_out": row(norm_out[l]),
    }


def _layer(x, p, tabs):
    b, s, d = x.shape
    t = b * s
    h, qt, k, vt, ql, kl, vl, rl, laf, lab = _ffn_inproj(x, p, tabs)
    o_att = _attn(qt, k, vt, p["score_bound"])
    o_f, o_b = _gla(ql, kl, vl, laf, lab)
    y = _outproj_ffn(h.reshape(t, d), o_att.reshape(t, ATT_WIDTH), o_f.reshape(t, GLA_WIDTH),
                     o_b.reshape(t, GLA_WIDTH), rl.reshape(t, GLA_WIDTH), p)
    return y.reshape(b, s, d)


def _trunk(x, layers):
    tabs = _rope_tables(x.shape[1])
    h = x
    for p in layers:
        h = _layer(h, p, tabs)
    return h


def kernel(x_prompt, x_sample, norm_ffn1, w_ffn1_gu, w_ffn1_down, norm_mix, w_in, q_norm, k_norm, w_gate_f, b_gate_f,
           w_gate_b, b_gate_b, gla_norm, w_out, norm_ffn2, w_ffn2_gu, w_ffn2_down, norm_out):
    params = (norm_ffn1, w_ffn1_gu, w_ffn1_down, norm_mix, w_in, q_norm, k_norm, w_gate_f, b_gate_f,
              w_gate_b, b_gate_b, gla_norm, w_out, norm_ffn2, w_ffn2_gu, w_ffn2_down, norm_out)
    layers = [_prep_layer(l, *params) for l in range(norm_ffn1.shape[0])]
    return (_trunk(x_prompt, layers), _trunk(x_sample, layers))
```
